```python
import math
import jax
import jax.numpy as jnp
from jax import lax
import numpy as np

D_MODEL = 1024
BATCH = 8
SEQ = 8192
DEPTH = 2
DEC_BATCH = 32
DEC_SEQ = 2048
PAST_LEN = 128

HEAD_DIM = 64
GRID_W = 64
ROPE_THETA = 10000.0
EPS = 1e-6
A_HEADS = D_MODEL // (2 * HEAD_DIM)
A_PATTERNS = ((128, 1), (512, 4), (2048, 16))
B_Q_HEADS = D_MODEL // (2 * HEAD_DIM)
B_KV_HEADS = B_Q_HEADS // 4
Q_BLOCK = 128
C_HEADS = D_MODEL // (2 * HEAD_DIM)
C_CHUNK = 64
D_GROUPS = D_MODEL // (2 * HEAD_DIM)
D_GROUP_DIM = HEAD_DIM
D_CHUNK = 128
D_FF = 4 * D_MODEL
N_EVEN = (DEPTH + 1) // 2
N_ODD = DEPTH // 2
A_W = A_HEADS * HEAD_DIM
B_QW = B_Q_HEADS * HEAD_DIM
B_KVW = B_KV_HEADS * HEAD_DIM
AB_IN = 3 * A_W + B_QW + 2 * B_KVW
AB_OUT = A_W + B_QW
C_W = C_HEADS * HEAD_DIM
D_W = D_GROUPS * D_GROUP_DIM
CD_IN = 4 * C_W + 4 * C_HEADS + 2 * D_W
CD_OUT = C_W + D_W

kernel_name = 'hybrid_bidir_encoder_dilated_gqa_mlstm_sgu'


def rmsnorm(x, g):
    xf = x.astype(jnp.float32)
    y = xf * lax.rsqrt(jnp.mean(xf * xf, axis=-1, keepdims=True) + EPS)
    return (y * g.astype(jnp.float32)).astype(x.dtype)


def layernorm(x, g):
    xf = x.astype(jnp.float32)
    mu = jnp.mean(xf, axis=-1, keepdims=True)
    var = jnp.mean(jnp.square(xf - mu), axis=-1, keepdims=True)
    return ((xf - mu) * lax.rsqrt(var + EPS) * g.astype(jnp.float32)).astype(x.dtype)


def rope_angles(pos, dim):
    inv_freq = ROPE_THETA ** (-jnp.arange(0, dim, 2, dtype=jnp.float32) / dim)
    return pos.astype(jnp.float32)[:, None] * inv_freq[None, :]


def apply_rotary(x, ang):
    cos = jnp.cos(ang)[:, None, :]
    sin = jnp.sin(ang)[:, None, :]
    x1, x2 = jnp.split(x.astype(jnp.float32), 2, axis=-1)
    return jnp.concatenate([x1 * cos - x2 * sin, x2 * cos + x1 * sin], axis=-1).astype(x.dtype)


def apply_axial_rotary(x, ang_row, ang_col):
    half = x.shape[-1] // 2
    return jnp.concatenate([apply_rotary(x[..., :half], ang_row),
                            apply_rotary(x[..., half:], ang_col)], axis=-1)


def dilated_branch(q, k, v, window, dilation):
    B, S, H, Dh = q.shape
    R = window // (2 * dilation)
    L = S // dilation
    nb = -(-L // R)
    Lp = nb * R

    def to_sub(t):
        t = t.reshape(B, L, dilation, H, Dh).transpose(0, 2, 3, 1, 4)
        return jnp.pad(t, ((0, 0), (0, 0), (0, 0), (0, Lp - L), (0, 0)))

    def key_blocks(t):
        tp = jnp.pad(to_sub(t), ((0, 0), (0, 0), (0, 0), (R, R), (0, 0)))
        tp = tp.reshape(B, dilation, H, nb + 2, R, Dh)
        return jnp.concatenate([tp[:, :, :, :-2], tp[:, :, :, 1:-1], tp[:, :, :, 2:]], axis=4)

    qb = to_sub(q).reshape(B, dilation, H, nb, R, Dh)
    kb = key_blocks(k)
    vb = key_blocks(v)
    qi = jnp.arange(Lp).reshape(nb, R)
    kj = jnp.arange(-R, Lp + R).reshape(nb + 2, R)
    kj = jnp.concatenate([kj[:-2], kj[1:-1], kj[2:]], axis=1)
    dist = jnp.abs(qi[:, :, None] - kj[:, None, :])
    valid = ((dist <= R) & (kj >= 0)[:, None, :] & (kj < L)[:, None, :]) | (dist == 0)
    s = jnp.einsum('bdhnqe,bdhnke->bdhnqk', qb, kb,
                   preferred_element_type=jnp.float32) * (Dh ** -0.5)
    s = jnp.where(valid, s, -jnp.inf)
    lse = jax.nn.logsumexp(s, axis=-1)
    p = jnp.exp(s - lse[..., None])
    o = jnp.einsum('bdhnqk,bdhnke->bdhnqe', p, vb.astype(jnp.float32))
    o = o.reshape(B, dilation, H, Lp, Dh)[:, :, :, :L].transpose(0, 3, 1, 2, 4).reshape(B, S, H, Dh)
    lse = lse.reshape(B, dilation, H, Lp)[:, :, :, :L].transpose(0, 3, 1, 2).reshape(B, S, H)
    return o, lse


def gqa_full(q, k, v):
    B, S, Hq, Dh = q.shape
    Hkv = k.shape[2]
    G = Hq // Hkv
    nq = S // Q_BLOCK
    qb = q.reshape(B, nq, Q_BLOCK, Hkv, G, Dh).transpose(1, 0, 2, 3, 4, 5)
    vf = v.astype(jnp.float32)

    def block(qblk):
        s = jnp.einsum('bqhgd,bkhd->bhgqk', qblk, k,
                       preferred_element_type=jnp.float32) * (Dh ** -0.5)
        p = jax.nn.softmax(s, axis=-1)
        return jnp.einsum('bhgqk,bkhd->bqhgd', p, vf)

    o = lax.map(block, qb)
    return o.transpose(1, 0, 2, 3, 4, 5).reshape(B, S, Hq, Dh)


def mlstm_chunkwise(q, k, v, i_pre, log_f):
    B, H, S, Dh = q.shape
    L = C_CHUNK
    nc = S // L
    k = k * (Dh ** -0.5)

    def chunks(t):
        return jnp.moveaxis(t.reshape((B, H, nc, L) + t.shape[3:]), 2, 0)

    causal = jnp.tril(jnp.ones((L, L), dtype=bool))

    def step(carry, inp):
        C, n, m = carry
        qq, kk, vv, ii, ff = inp
        b = jnp.cumsum(ff, axis=-1)
        log_d = b[..., :, None] - b[..., None, :] + ii[..., None, :]
        log_d = jnp.where(causal, log_d, -jnp.inf)
        inter = b + m[..., None]
        m_t = jnp.maximum(inter, jnp.max(log_d, axis=-1))
        dmat = jnp.exp(log_d - m_t[..., None])
        w_inter = jnp.exp(inter - m_t)
        s = jnp.einsum('bhtd,bhsd->bhts', qq, kk) * dmat
        num = jnp.einsum('bhts,bhse->bhte', s, vv) + \
            w_inter[..., None] * jnp.einsum('bhed,bhtd->bhte', C, qq)
        den = jnp.sum(s, axis=-1) + w_inter * jnp.einsum('bhd,bhtd->bht', n, qq)
        h = num / jnp.maximum(jnp.abs(den), jnp.exp(-m_t))[..., None]
        b_last = b[..., -1]
        log_w = b_last[..., None] - b + ii
        m_new = jnp.maximum(b_last + m, jnp.max(log_w, axis=-1))
        wk = jnp.exp(log_w - m_new[..., None])
        decay = jnp.exp(b_last + m - m_new)
        C_new = decay[..., None, None] * C + jnp.einsum('bhs,bhse,bhsd->bhed', wk, vv, kk)
        n_new = decay[..., None] * n + jnp.einsum('bhs,bhsd->bhd', wk, kk)
        return (C_new, n_new, m_new), h

    init = (jnp.zeros((B, H, Dh, Dh), jnp.float32),
            jnp.zeros((B, H, Dh), jnp.float32),
            jnp.zeros((B, H), jnp.float32))
    _, h = lax.scan(step, init, (chunks(q), chunks(k), chunks(v), chunks(i_pre), chunks(log_f)))
    return jnp.moveaxis(h, 0, 2).reshape(B, H, S, Dh)


def spatial_gating(u, v, ln_g, w_s, b_s):
    B, S, _ = u.shape
    nc = S // D_CHUNK
    u = jax.nn.gelu(u)
    v = layernorm(jax.nn.gelu(v), ln_g)
    vb = v.reshape(B, nc, D_CHUNK, D_GROUPS, D_GROUP_DIM)
    s = jnp.einsum('gpq,bcqge->bcpge', w_s, vb) + b_s.T[None, None, :, :, None]
    return u * s.reshape(B, S, D_W)


def even_mixer(h, w_in, w_out, qk_g, ang_1d, ang_row, ang_col):
    B, S, _ = h.shape
    z = h @ w_in
    o1 = A_W
    o2 = 2 * A_W
    o3 = 3 * A_W
    o4 = o3 + B_QW
    o5 = o4 + B_KVW
    qa, ka, va, qb, kb, vb = jnp.split(z, [o1, o2, o3, o4, o5], axis=-1)
    qa = apply_rotary(qa.reshape(B, S, A_HEADS, HEAD_DIM), ang_1d)
    ka = apply_rotary(ka.reshape(B, S, A_HEADS, HEAD_DIM), ang_1d)
    va = va.reshape(B, S, A_HEADS, HEAD_DIM)
    outs = []
    lses = []
    for window, dil in A_PATTERNS:
        o_g, l_g = dilated_branch(qa, ka, va, window, dil)
        outs.append(o_g)
        lses.append(l_g)
    alpha = jax.nn.softmax(jnp.stack(lses, axis=0), axis=0)
    oa = jnp.sum(alpha[..., None] * jnp.stack(outs, axis=0), axis=0)
    qb = rmsnorm(qb.reshape(B, S, B_Q_HEADS, HEAD_DIM), qk_g[0])
    kb = rmsnorm(kb.reshape(B, S, B_KV_HEADS, HEAD_DIM), qk_g[1])
    qb = apply_axial_rotary(qb, ang_row, ang_col)
    kb = apply_axial_rotary(kb, ang_row, ang_col)
    ob = gqa_full(qb, kb, vb.reshape(B, S, B_KV_HEADS, HEAD_DIM))
    o = jnp.concatenate([oa.reshape(B, S, A_W), ob.reshape(B, S, B_QW)], axis=-1).astype(h.dtype)
    return o @ w_out


def odd_mixer(h, w_in, w_out, gate_b, mh_g, sg_g, w_s, b_s):
    B, S, _ = h.shape
    z = h @ w_in
    o1 = C_W
    o2 = 2 * C_W
    o3 = 3 * C_W
    o4 = 4 * C_W
    o5 = o4 + 4 * C_HEADS
    o6 = o5 + D_W
    q, k, v, og, gates, u, vd = jnp.split(z, [o1, o2, o3, o4, o5, o6], axis=-1)

    def heads(t):
        return t.reshape(B, S, C_HEADS, HEAD_DIM).transpose(0, 2, 1, 3).astype(jnp.float32)

    q, k, v = heads(q), heads(k), heads(v)
    g = gates.astype(jnp.float32) + gate_b.reshape(-1).astype(jnp.float32)
    g = g.reshape(B, S, 4, C_HEADS).transpose(2, 0, 3, 1)
    i_fw, f_fw, i_bw, f_bw = g[0], g[1], g[2], g[3]
    h_fw = mlstm_chunkwise(q, k, v, i_fw, jax.nn.log_sigmoid(f_fw))
    flip = lambda t: jnp.flip(t, axis=2)
    h_bw = flip(mlstm_chunkwise(flip(q), flip(k), flip(v), flip(i_bw), flip(jax.nn.log_sigmoid(f_bw))))
    hc = (h_fw + h_bw).transpose(0, 2, 1, 3)
    hc = rmsnorm(hc, mh_g) * jax.nn.sigmoid(og.reshape(B, S, C_HEADS, HEAD_DIM).astype(jnp.float32))
    d_out = spatial_gating(u, vd, sg_g, w_s, b_s)
    o = jnp.concatenate([hc.reshape(B, S, C_W).astype(h.dtype), d_out.astype(h.dtype)], axis=-1)
    return o @ w_out


def trunk(x, c, w_in_ab, w_out_ab, qk_norm_g, w_in_cd, w_out_cd, gate_bias, mh_norm_g,
          sg_norm_g, w_spatial, b_spatial, w_ada, b_ada, norm_g, w_ff1, w_ff2):
    S = x.shape[1]
    rows = S // GRID_W
    pos = jnp.arange(S)
    ang_1d = rope_angles(pos, HEAD_DIM)
    row_ids = jnp.repeat(jnp.arange(rows), GRID_W)
    col_ids = jnp.tile(jnp.arange(GRID_W), rows)
    ang_row = rope_angles(row_ids, HEAD_DIM // 2)
    ang_col = rope_angles(col_ids, HEAD_DIM // 2)
    c_act = jax.nn.silu(c)
    for layer in range(DEPTH):
        mod = (c_act @ w_ada[layer] + b_ada[layer])[:, None, :]
        sh1, sc1, g1, sh2, sc2, g2 = jnp.split(mod, 6, axis=-1)
        hm = rmsnorm(x, norm_g[layer, 0]) * (1.0 + sc1) + sh1
        if layer % 2 == 0:
            e = layer // 2
            y = even_mixer(hm, w_in_ab[e], w_out_ab[e], qk_norm_g[e], ang_1d, ang_row, ang_col)
        else:
            o = layer // 2
            y = odd_mixer(hm, w_in_cd[o], w_out_cd[o], gate_bias[o], mh_norm_g[o],
                          sg_norm_g[o], w_spatial[o], b_spatial[o])
        x = x + g1 * rmsnorm(y, norm_g[layer, 1])
        hf = rmsnorm(x, norm_g[layer, 2]) * (1.0 + sc2) + sh2
        y = jnp.square(jax.nn.relu(hf @ w_ff1[layer])) @ w_ff2[layer]
        x = x + g2 * rmsnorm(y, norm_g[layer, 3])
    return x


def setup_inputs(seed: int = 0) -> dict:
    key = jax.random.key(seed)
    ks = jax.random.split(key, 24)

    def nrm(k, shape, scale):
        return jax.random.normal(k, shape, jnp.float32) * scale

    f_bias = jnp.linspace(3.0, 6.0, C_HEADS, dtype=jnp.float32)
    gate_bias = jnp.stack([nrm(ks[8], (N_ODD, C_HEADS), 0.1),
                           f_bias + nrm(ks[9], (N_ODD, C_HEADS), 0.1),
                           nrm(ks[10], (N_ODD, C_HEADS), 0.1),
                           f_bias + nrm(ks[11], (N_ODD, C_HEADS), 0.1)], axis=1)
    return {
        'x_prompt': nrm(ks[0], (BATCH, SEQ, D_MODEL), 1.0),
        'x_sample': nrm(ks[1], (DEC_BATCH, DEC_SEQ, D_MODEL), 1.0),
        'c_prompt': nrm(ks[2], (BATCH, D_MODEL), 1.0),
        'c_sample': nrm(ks[3], (DEC_BATCH, D_MODEL), 1.0),
        'w_in_ab': nrm(ks[4], (N_EVEN, D_MODEL, AB_IN), D_MODEL ** -0.5),
        'w_out_ab': nrm(ks[5], (N_EVEN, AB_OUT, D_MODEL), AB_OUT ** -0.5),
        'qk_norm_g': 1.0 + nrm(ks[6], (N_EVEN, 2, HEAD_DIM), 0.02),
        'w_in_cd': nrm(ks[7], (N_ODD, D_MODEL, CD_IN), D_MODEL ** -0.5),
        'w_out_cd': nrm(ks[12], (N_ODD, CD_OUT, D_MODEL), CD_OUT ** -0.5),
        'gate_bias': gate_bias,
        'mh_norm_g': 1.0 + nrm(ks[13], (N_ODD, C_HEADS, HEAD_DIM), 0.02),
        'sg_norm_g': 1.0 + nrm(ks[14], (N_ODD, D_W), 0.02),
        'w_spatial': nrm(ks[15], (N_ODD, D_GROUPS, D_CHUNK, D_CHUNK), D_CHUNK ** -0.5),
        'b_spatial': nrm(ks[16], (N_ODD, D_GROUPS, D_CHUNK), 0.02),
        'w_ada': nrm(ks[17], (DEPTH, D_MODEL, 6 * D_MODEL), D_MODEL ** -0.5),
        'b_ada': nrm(ks[18], (DEPTH, 6 * D_MODEL), 0.02),
        'norm_g': 1.0 + nrm(ks[19], (DEPTH, 4, D_MODEL), 0.02),
        'w_ff1': nrm(ks[20], (DEPTH, D_MODEL, D_FF), D_MODEL ** -0.5),
        'w_ff2': nrm(ks[21], (DEPTH, D_FF, D_MODEL), D_FF ** -0.5),
    }


def reference(x_prompt, x_sample, c_prompt, c_sample, w_in_ab, w_out_ab, qk_norm_g, w_in_cd,
              w_out_cd, gate_bias, mh_norm_g, sg_norm_g, w_spatial, b_spatial, w_ada, b_ada,
              norm_g, w_ff1, w_ff2):
    y_prompt = trunk(x_prompt, c_prompt, w_in_ab, w_out_ab, qk_norm_g, w_in_cd, w_out_cd,
                     gate_bias, mh_norm_g, sg_norm_g, w_spatial, b_spatial, w_ada, b_ada,
                     norm_g, w_ff1, w_ff2)
    y_sample = trunk(x_sample, c_sample, w_in_ab, w_out_ab, qk_norm_g, w_in_cd, w_out_cd,
                     gate_bias, mh_norm_g, sg_norm_g, w_spatial, b_spatial, w_ada, b_ada,
                     norm_g, w_ff1, w_ff2)
    return (y_prompt, y_sample)
```

```python
import functools

import numpy as np
import jax
import jax.numpy as jnp
from jax import lax
from jax.experimental import pallas as pl
from jax.experimental.pallas import tpu as pltpu

F32 = jnp.float32
BF16 = jnp.bfloat16

D_MODEL = 1024
HEAD_DIM = 64
GRID_W = 64
ROPE_THETA = 10000.0
EPS = 1e-6
A_HEADS = 8
A_PATTERNS = ((128, 1), (512, 4), (2048, 16))
B_Q_HEADS = 8
B_KV_HEADS = 2
C_HEADS = 8
D_GROUPS = 8
D_CHUNK = 128
D_FF = 4 * D_MODEL
HALF_W = 512
AB_IN = 3 * HALF_W + HALF_W + 2 * B_KV_HEADS * HEAD_DIM
CD_MAIN = 6 * HALF_W
N_GATES = 4 * C_HEADS

LANES = 128
MXU_N = 256
VMEM_LIMIT = 48 * 1024 * 1024

NEG = -1e30


def _cparams(*sem):
    return pltpu.CompilerParams(dimension_semantics=sem, vmem_limit_bytes=VMEM_LIMIT)


def _rms_rows(x):
    return x * lax.rsqrt(jnp.mean(x * x, axis=-1, keepdims=True) + EPS)


def _split_dot(a, b):
    hi = a.astype(BF16)
    r1 = a - hi.astype(F32)
    mid = r1.astype(BF16)
    lo = (r1 - mid.astype(F32)).astype(BF16)
    dot = functools.partial(jnp.dot, preferred_element_type=F32)
    return dot(hi, b) + dot(mid, b) + dot(lo, b)


def _split_dot_rhs(b, a):
    hi = a.astype(BF16)
    r1 = a - hi.astype(F32)
    mid = r1.astype(BF16)
    lo = (r1 - mid.astype(F32)).astype(BF16)
    dot = functools.partial(jnp.dot, preferred_element_type=F32)
    return dot(b, hi) + dot(b, mid) + dot(b, lo)


def _head_mean_square(z, ones_blk):
    z2 = z * z
    hi = z2.astype(BF16)
    lo = (z2 - hi.astype(F32)).astype(BF16)
    dot = functools.partial(jnp.dot, preferred_element_type=F32)
    return (dot(hi, ones_blk) + dot(lo, ones_blk)) * (1.0 / HEAD_DIM)


def _head_block_ones():
    r = lax.broadcasted_iota(jnp.int32, (LANES, LANES), 0) // HEAD_DIM
    c = lax.broadcasted_iota(jnp.int32, (LANES, LANES), 1) // HEAD_DIM
    return (r == c).astype(BF16)


def _log_sigmoid(x):
    return jnp.minimum(x, 0.0) - jnp.log1p(jnp.exp(-jnp.abs(x)))


def _dot_nt(a, b):
    return lax.dot_general(a, b, (((1,), (1,)), ((), ())), preferred_element_type=F32)


def _dot_tn(a, b):
    return lax.dot_general(a, b, (((0,), (0,)), ((), ())), preferred_element_type=F32)


def _adaln_kernel(c_ref, w_ref, b_ref, o_ref):
    c = c_ref[...]
    ca = c * jax.nn.sigmoid(c)
    o_ref[...] = jnp.dot(ca.astype(BF16), w_ref[...], preferred_element_type=F32) + b_ref[...]


def _adaln(c, w, b):
    nb, d = c.shape
    n = w.shape[1]
    tn = 2048
    return pl.pallas_call(
        _adaln_kernel,
        grid=(n // tn,),
        in_specs=[pl.BlockSpec((nb, d), lambda j: (0, 0)),
                  pl.BlockSpec((d, tn), lambda j: (0, j)),
                  pl.BlockSpec((1, tn), lambda j: (0, j))],
        out_specs=pl.BlockSpec((nb, tn), lambda j: (0, j)),
        out_shape=jax.ShapeDtypeStruct((nb, n), F32),
        name="adaln",
        compiler_params=_cparams("parallel"),
    )(c, w, b)


def _swap_halves(z, half, upper):
    return jnp.where(upper, pltpu.roll(z, LANES - half, axis=1), pltpu.roll(z, half, axis=1))


def _inproj_even_kernel(x_ref, g_ref, sc_ref, sh_ref, w_ref, ta_ref, tq_ref, tk_ref, z_ref):
    x = x_ref[0]
    h = _rms_rows(x) * (g_ref[...] * (1.0 + sc_ref[0])) + sh_ref[0]
    hb = h.astype(BF16)
    tm = x.shape[0]
    lane = lax.broadcasted_iota(jnp.int32, (tm, LANES), 1)
    up64 = (lane % 64) < 32
    up32 = (lane % 32) < 16
    ones_blk = _head_block_ones()
    q_scale = HEAD_DIM ** -0.5

    def rotary_1d(z):
        return z * ta_ref[:, :LANES] + _swap_halves(z, 32, up64) * ta_ref[:, LANES:]

    def norm_axial(z, t_ref):
        y = z * t_ref[:, :LANES] + _swap_halves(z, 16, up32) * t_ref[:, LANES:]
        return y * lax.rsqrt(_head_mean_square(z, ones_blk) + EPS)

    for j in range(AB_IN // MXU_N):
        zz = jnp.dot(hb, w_ref[:, j * MXU_N:(j + 1) * MXU_N], preferred_element_type=F32)
        for s in range(MXU_N // LANES):
            c = (MXU_N // LANES) * j + s
            z = zz[:, s * LANES:(s + 1) * LANES]
            if c < 4:
                z = rotary_1d(z) * q_scale
            elif c < 8:
                z = rotary_1d(z)
            elif 12 <= c < 16:
                z = norm_axial(z, tq_ref)
            elif c == 16:
                z = norm_axial(z, tk_ref)
            z_ref[0, :, c * LANES:(c + 1) * LANES] = z.astype(BF16)


def _inproj_even(x, g, sc, sh, w, ta, tq, tk, tm):
    b, s, d = x.shape
    n = w.shape[1]
    tab = pl.BlockSpec((tm, 2 * LANES), lambda i, bb: (i, 0))
    mod = pl.BlockSpec((1, 1, d), lambda i, bb: (bb, 0, 0))
    return pl.pallas_call(
        _inproj_even_kernel,
        grid=(s // tm, b),
        in_specs=[pl.BlockSpec((1, tm, d), lambda i, bb: (bb, i, 0)),
                  pl.BlockSpec((1, d), lambda i, bb: (0, 0)),
                  mod, mod,
                  pl.BlockSpec((d, n), lambda i, bb: (0, 0)),
                  tab, tab, tab],
        out_specs=pl.BlockSpec((1, tm, n), lambda i, bb: (bb, i, 0)),
        out_shape=jax.ShapeDtypeStruct((b, s, n), BF16),
        name="inproj_even",
        compiler_params=_cparams("parallel", "parallel"),
    )(x, g, sc, sh, w, ta, tq, tk)


def _rope_tables(s, qk_g):
    pos = jnp.arange(s)

    def angles(p, dim):
        inv_freq = ROPE_THETA ** (-jnp.arange(0, dim, 2, dtype=F32) / dim)
        return p.astype(F32)[:, None] * inv_freq[None, :]

    a1 = angles(pos, HEAD_DIM)
    cos1, sin1 = jnp.cos(a1), jnp.sin(a1)
    direct = jnp.concatenate([cos1, cos1], axis=-1)
    swapped = jnp.concatenate([-sin1, sin1], axis=-1)
    ta = jnp.concatenate([direct, direct, swapped, swapped], axis=-1)

    ar = angles(pos // GRID_W, HEAD_DIM // 2)
    ac = angles(pos % GRID_W, HEAD_DIM // 2)
    cosb = jnp.concatenate([jnp.cos(ar), jnp.cos(ar), jnp.cos(ac), jnp.cos(ac)], axis=-1)
    sinb = jnp.concatenate([-jnp.sin(ar), jnp.sin(ar), -jnp.sin(ac), jnp.sin(ac)], axis=-1)

    def gained(gain, scale):
        gain = gain.astype(F32)
        gswap = gain.reshape(2, 2, HEAD_DIM // 4)[:, ::-1].reshape(HEAD_DIM)
        direct = cosb * gain * scale
        swapped = sinb * gswap * scale
        return jnp.concatenate([direct, direct, swapped, swapped], axis=-1)

    return ta, gained(qk_g[0], HEAD_DIM ** -0.5), gained(qk_g[1], 1.0)


def _flash_step(q, k, v, bias, m_ref, l_ref, acc_ref):
    s = _dot_nt(q, k)
    if bias is not None:
        s = s + bias
    m_prev = m_ref[...]
    m_new = jnp.maximum(m_prev, jnp.max(s, axis=-1, keepdims=True))
    alpha = jnp.exp(m_prev - m_new)
    p = jnp.exp(s - m_new)
    l_ref[...] = alpha * l_ref[...] + jnp.sum(p, axis=-1, keepdims=True)
    acc_ref[...] = alpha * acc_ref[...] + jnp.dot(p.astype(BF16), v, preferred_element_type=F32)
    m_ref[...] = m_new


def _flash_init(m_ref, l_ref, acc_ref):
    m_ref[...] = jnp.full(m_ref.shape, -jnp.inf, F32)
    l_ref[...] = jnp.zeros(l_ref.shape, F32)
    acc_ref[...] = jnp.zeros(acc_ref.shape, F32)


def _gqa_kernel(q_ref, k_ref, v_ref, o_ref, m_ref, l_ref, acc_ref, *, tk):
    g, tq, dh = q_ref.shape[1:]
    q = q_ref[0].reshape(g * tq, dh)
    _flash_init(m_ref, l_ref, acc_ref)

    def body(j, carry):
        rows = pl.ds(pl.multiple_of(j * tk, tk), tk)
        _flash_step(q, k_ref[0, 0, rows, :], v_ref[0, 0, rows, :], None, m_ref, l_ref, acc_ref)
        return carry

    lax.fori_loop(0, k_ref.shape[2] // tk, body, 0)
    o = acc_ref[...] / l_ref[...]
    o_ref[0] = o.reshape(g, tq, dh).astype(o_ref.dtype)


def _gqa_attention(q, k, v, tq, tk):
    b, hq, s, dh = q.shape
    hkv = k.shape[1]
    g = hq // hkv
    kv = pl.BlockSpec((1, 1, s, dh), lambda bb, hh, i: (bb, hh, 0, 0))
    return pl.pallas_call(
        functools.partial(_gqa_kernel, tk=tk),
        grid=(b, hkv, s // tq),
        in_specs=[pl.BlockSpec((1, g, tq, dh), lambda bb, hh, i: (bb, hh, i, 0)), kv, kv],
        out_specs=pl.BlockSpec((1, g, tq, dh), lambda bb, hh, i: (bb, hh, i, 0)),
        out_shape=jax.ShapeDtypeStruct((b, hq, s, dh), BF16),
        scratch_shapes=[pltpu.VMEM((g * tq, 1), F32), pltpu.VMEM((g * tq, 1), F32),
                        pltpu.VMEM((g * tq, dh), F32)],
        name="gqa_attention",
        compiler_params=_cparams("parallel", "parallel", "parallel"),
    )(q, k, v)


def _band_offsets(t):
    reach = max(w // 2 for w, _ in A_PATTERNS)
    nb = -(-reach // t)
    offs = [0]
    for d in range(1, nb + 1):
        offs += [-d, d]
    return tuple(offs)


def _band_bias(t):
    offs = _band_offsets(t)
    a = np.arange(t)[:, None]
    bcol = np.arange(t)[None, :]
    out = np.zeros((len(offs), t, t), np.float64)
    for n, d in enumerate(offs):
        j = d * t + bcol - a
        mult = np.zeros((t, t), np.float64)
        for window, dil in A_PATTERNS:
            mult += ((j % dil) == 0) & (np.abs(j) <= window // 2)
        out[n] = np.where(mult > 0, np.log(np.maximum(mult, 1.0)), NEG)
    return jnp.asarray(out, F32)


def _band_kernel(q_ref, k_ref, v_ref, bias_ref, o_ref, m_ref, l_ref, acc_ref, *, offsets):
    t = q_ref.shape[2]
    nblk = k_ref.shape[2] // t
    i = pl.program_id(2)
    q = q_ref[0, 0]
    _flash_init(m_ref, l_ref, acc_ref)
    for n, d in enumerate(offsets):
        @pl.when(jnp.logical_and(i + d >= 0, i + d < nblk))
        def _():
            rows = pl.ds(pl.multiple_of((i + d) * t, t), t)
            _flash_step(q, k_ref[0, 0, rows, :], v_ref[0, 0, rows, :], bias_ref[n],
                        m_ref, l_ref, acc_ref)
    o_ref[0, 0] = (acc_ref[...] / l_ref[...]).astype(o_ref.dtype)


def _dilated_attention(q, k, v, t):
    b, h, s, dh = q.shape
    offsets = _band_offsets(t)
    bias = _band_bias(t)
    kv = pl.BlockSpec((1, 1, s, dh), lambda bb, hh, i: (bb, hh, 0, 0))
    blk = pl.BlockSpec((1, 1, t, dh), lambda bb, hh, i: (bb, hh, i, 0))
    return pl.pallas_call(
        functools.partial(_band_kernel, offsets=offsets),
        grid=(b, h, s // t),
        in_specs=[blk, kv, kv, pl.BlockSpec(bias.shape, lambda bb, hh, i: (0, 0, 0))],
        out_specs=blk,
        out_shape=jax.ShapeDtypeStruct((b, h, s, dh), BF16),
        scratch_shapes=[pltpu.VMEM((t, 1), F32), pltpu.VMEM((t, 1), F32), pltpu.VMEM((t, dh), F32)],
        name="dilated_attention",
        compiler_params=_cparams("parallel", "parallel", "parallel"),
    )(q, k, v, bias)


def _residual_update(x, y, gate, norm_g):
    return x + gate * (_rms_rows(y) * norm_g)


def _outproj_even_kernel(oa_ref, ob_ref, w_ref, x_ref, gate_ref, ng_ref, o_ref):
    y = jnp.dot(oa_ref[0], w_ref[:HALF_W, :], preferred_element_type=F32)
    y = y + jnp.dot(ob_ref[0], w_ref[HALF_W:, :], preferred_element_type=F32)
    o_ref[0] = _residual_update(x_ref[0], y, gate_ref[0], ng_ref[...])


def _outproj_odd_kernel(hf_ref, hb_ref, og_ref, d_ref, mhg_ref, w_ref, x_ref, gate_ref, ng_ref,
                        o_ref):
    ones_blk = _head_block_ones()
    parts = []
    for c in range(HALF_W // LANES):
        cols = slice(c * LANES, (c + 1) * LANES)
        hs = hf_ref[0, :, cols] + hb_ref[0, :, cols]
        hn = hs * lax.rsqrt(_head_mean_square(hs, ones_blk) + EPS) * mhg_ref[:, cols]
        parts.append((hn * jax.nn.sigmoid(og_ref[0, :, cols].astype(F32))).astype(BF16))
    hc = jnp.concatenate(parts, axis=-1)
    y = jnp.dot(hc, w_ref[:HALF_W, :], preferred_element_type=F32)
    y = y + jnp.dot(d_ref[0], w_ref[HALF_W:, :], preferred_element_type=F32)
    o_ref[0] = _residual_update(x_ref[0], y, gate_ref[0], ng_ref[...])


def _outproj_specs(b, s, d, tm):
    half = pl.BlockSpec((1, tm, HALF_W), lambda bb, i: (bb, i, 0))
    full = pl.BlockSpec((1, tm, d), lambda bb, i: (bb, i, 0))
    w = pl.BlockSpec((d, d), lambda bb, i: (0, 0))
    mod = pl.BlockSpec((1, 1, d), lambda bb, i: (bb, 0, 0))
    vec = pl.BlockSpec((1, d), lambda bb, i: (0, 0))
    return half, full, w, mod, vec


def _outproj_even(oa, ob, w, x, gate, ng, tm):
    b, s, d = x.shape
    half, full, wspec, mod, vec = _outproj_specs(b, s, d, tm)
    return pl.pallas_call(
        _outproj_even_kernel,
        grid=(b, s // tm),
        in_specs=[half, half, wspec, full, mod, vec],
        out_specs=full,
        out_shape=jax.ShapeDtypeStruct((b, s, d), F32),
        name="outproj_even",
        compiler_params=_cparams("parallel", "parallel"),
    )(oa, ob, w, x, gate, ng)


def _outproj_odd(hf, hb, z, dout, mhg, w, x, gate, ng, tm):
    b, s, d = x.shape
    half, full, wspec, mod, vec = _outproj_specs(b, s, d, tm)
    og = pl.BlockSpec((1, tm, HALF_W), lambda bb, i: (bb, i, 3))
    return pl.pallas_call(
        _outproj_odd_kernel,
        grid=(b, s // tm),
        in_specs=[half, half, og, half, pl.BlockSpec((1, HALF_W), lambda bb, i: (0, 0)),
                  wspec, full, mod, vec],
        out_specs=full,
        out_shape=jax.ShapeDtypeStruct((b, s, d), F32),
        name="outproj_odd",
        compiler_params=_cparams("parallel", "parallel"),
    )(hf, hb, z, dout, mhg, w, x, gate, ng)


def _ffn_kernel(x_ref, g_ref, sc_ref, sh_ref, w1_ref, w2_ref, gate_ref, ng_ref, o_ref,
                h_ref, acc_ref):
    kk = pl.program_id(2)

    @pl.when(kk == 0)
    def _():
        h = _rms_rows(x_ref[0]) * (g_ref[...] * (1.0 + sc_ref[0])) + sh_ref[0]
        h_ref[...] = h.astype(BF16)
        acc_ref[...] = jnp.zeros(acc_ref.shape, F32)

    a = jnp.maximum(jnp.dot(h_ref[...], w1_ref[...], preferred_element_type=F32), 0.0)
    acc_ref[...] += jnp.dot((a * a).astype(BF16), w2_ref[...], preferred_element_type=F32)

    @pl.when(kk == pl.num_programs(2) - 1)
    def _():
        o_ref[0] = _residual_update(x_ref[0], acc_ref[...], gate_ref[0], ng_ref[...])


def _ffn(x, g, sc, sh, w1, w2, gate, ng, tm, tf):
    b, s, d = x.shape
    f = w1.shape[1]
    full = pl.BlockSpec((1, tm, d), lambda bb, i, k: (bb, i, 0))
    mod = pl.BlockSpec((1, 1, d), lambda bb, i, k: (bb, 0, 0))
    vec = pl.BlockSpec((1, d), lambda bb, i, k: (0, 0))
    return pl.pallas_call(
        _ffn_kernel,
        grid=(b, s // tm, f // tf),
        in_specs=[full, vec, mod, mod,
                  pl.BlockSpec((d, tf), lambda bb, i, k: (0, k)),
                  pl.BlockSpec((tf, d), lambda bb, i, k: (k, 0)),
                  mod, vec],
        out_specs=full,
        out_shape=jax.ShapeDtypeStruct((b, s, d), F32),
        scratch_shapes=[pltpu.VMEM((tm, d), BF16), pltpu.VMEM((tm, d), F32)],
        name="ffn",
        compiler_params=_cparams("parallel", "parallel", "arbitrary"),
    )(x, g, sc, sh, w1, w2, gate, ng)


def _inproj_odd_kernel(x_ref, g_ref, sc_ref, sh_ref, w_ref, gb_ref, z_ref, gates_ref):
    x = x_ref[0]
    h = _rms_rows(x) * (g_ref[...] * (1.0 + sc_ref[0])) + sh_ref[0]
    hb = h.astype(BF16)
    k_scale = HEAD_DIM ** -0.5
    for j in range(CD_MAIN // MXU_N):
        cols = slice(j * MXU_N, (j + 1) * MXU_N)
        zz = jnp.dot(hb, w_ref[:, cols], preferred_element_type=F32)
        if HALF_W <= j * MXU_N < 2 * HALF_W:
            zz = zz * k_scale
        z_ref[0, :, cols] = zz.astype(BF16)
    gates_ref[0] = jnp.dot(hb, w_ref[:, CD_MAIN:], preferred_element_type=F32) + gb_ref[...]


def _inproj_odd(x, g, sc, sh, w, gb, tm):
    b, s, d = x.shape
    n = w.shape[1]
    mod = pl.BlockSpec((1, 1, d), lambda i, bb: (bb, 0, 0))
    return pl.pallas_call(
        _inproj_odd_kernel,
        grid=(s // tm, b),
        in_specs=[pl.BlockSpec((1, tm, d), lambda i, bb: (bb, i, 0)),
                  pl.BlockSpec((1, d), lambda i, bb: (0, 0)),
                  mod, mod,
                  pl.BlockSpec((d, n), lambda i, bb: (0, 0)),
                  pl.BlockSpec((1, LANES), lambda i, bb: (0, 0))],
        out_specs=[pl.BlockSpec((1, tm, CD_MAIN), lambda i, bb: (bb, i, 0)),
                   pl.BlockSpec((1, tm, LANES), lambda i, bb: (bb, i, 0))],
        out_shape=[jax.ShapeDtypeStruct((b, s, CD_MAIN), BF16),
                   jax.ShapeDtypeStruct((b, s, LANES), F32)],
        name="inproj_odd",
        compiler_params=_cparams("parallel", "parallel"),
    )(x, g, sc, sh, w, gb)


def _mlstm_kernel(q_ref, k_ref, v_ref, g_ref, o_ref, c_sc, n_sc, m_sc, *, rev, lc):
    @pl.when(pl.program_id(1) == 0)
    def _():
        c_sc[...] = jnp.zeros(c_sc.shape, F32)
        n_sc[...] = jnp.zeros(n_sc.shape, F32)
        m_sc[...] = jnp.zeros(m_sc.shape, F32)

    nch = q_ref.shape[1] // lc
    r = lax.broadcasted_iota(jnp.int32, (lc, lc), 0)
    c = lax.broadcasted_iota(jnp.int32, (lc, lc), 1)
    sees = (c >= r) if rev else (c <= r)
    sees_b = sees.astype(BF16)
    sees_t_b = ((r >= c) if rev else (r <= c)).astype(BF16)
    base = 2 * C_HEADS if rev else 0
    edge = 0 if rev else lc - 1

    def chunk(ci, carry):
        cc = (nch - 1 - ci) if rev else ci
        rows = pl.ds(pl.multiple_of(cc * lc, lc), lc)
        gt = g_ref[0, rows, :]
        gtt = gt.T
        bcum = _split_dot_rhs(sees_b, _log_sigmoid(gt))
        bcum_t = _split_dot(_log_sigmoid(gtt), sees_t_b)
        for h in range(C_HEADS):
            il, fl = base + h, base + C_HEADS + h
            cols = slice(h * HEAD_DIM, (h + 1) * HEAD_DIM)
            q = q_ref[0, rows, cols]
            k = k_ref[0, rows, cols]
            v = v_ref[0, rows, cols]
            b_c = bcum[:, fl:fl + 1]
            i_c = gt[:, il:il + 1]
            b_r = bcum_t[fl:fl + 1, :]
            i_r = gtt[il:il + 1, :]
            m_prev = m_sc[h]
            log_d = jnp.where(sees, b_c - b_r + i_r, -jnp.inf)
            inter = b_c + m_prev
            m_t = jnp.maximum(inter, jnp.max(log_d, axis=-1, keepdims=True))
            dmat = jnp.exp(log_d - m_t)
            w_inter = jnp.exp(inter - m_t)
            sc = _dot_nt(q, k) * dmat
            c_prev = c_sc[h]
            n_prev = n_sc[h]
            num = jnp.dot(sc.astype(BF16), v, preferred_element_type=F32)
            num = num + w_inter * _dot_nt(q, c_prev.astype(BF16))
            den = jnp.sum(sc, axis=-1, keepdims=True)
            den = den + w_inter * jnp.sum(q.astype(F32) * n_prev, axis=-1, keepdims=True)
            o_ref[0, rows, cols] = num / jnp.maximum(jnp.abs(den), jnp.exp(-m_t))
            b_all = bcum[edge:edge + 1, fl:fl + 1]
            log_w = b_all - b_c + i_c
            m_new = jnp.maximum(b_all + m_prev, jnp.max(log_w, axis=0, keepdims=True))
            wk = jnp.exp(log_w - m_new)
            decay = jnp.exp(b_all + m_prev - m_new)
            vw = (wk * v.astype(F32)).astype(BF16)
            c_sc[h] = decay * c_prev + _dot_tn(vw, k)
            n_sc[h] = decay * n_prev + jnp.sum(wk * k.astype(F32), axis=0, keepdims=True)
            m_sc[h] = m_new
        return carry

    lax.fori_loop(0, nch, chunk, 0)


def _mlstm(z, gates, rev, tb, lc):
    b, s, _ = z.shape
    nblk = s // tb
    if rev:
        pos = lambda bb, i: nblk - 1 - i
    else:
        pos = lambda bb, i: i
    col = lambda j: pl.BlockSpec((1, tb, HALF_W), lambda bb, i: (bb, pos(bb, i), j))
    return pl.pallas_call(
        functools.partial(_mlstm_kernel, rev=rev, lc=lc),
        grid=(b, nblk),
        in_specs=[col(0), col(1), col(2),
                  pl.BlockSpec((1, tb, LANES), lambda bb, i: (bb, pos(bb, i), 0))],
        out_specs=col(0),
        out_shape=jax.ShapeDtypeStruct((b, s, HALF_W), F32),
        scratch_shapes=[pltpu.VMEM((C_HEADS, HEAD_DIM, HEAD_DIM), F32),
                        pltpu.VMEM((C_HEADS, 1, HEAD_DIM), F32),
                        pltpu.VMEM((C_HEADS, 1, 1), F32)],
        name="mlstm_bwd" if rev else "mlstm_fwd",
        compiler_params=_cparams("parallel", "arbitrary"),
    )(z, z, z, gates)


def _sgu_kernel(u_ref, v_ref, lng_ref, w_ref, b_ref, o_ref):
    tb = u_ref.shape[1]
    group = lax.broadcasted_iota(jnp.int32, (D_CHUNK, HALF_W), 1) // HEAD_DIM
    for ch in range(tb // D_CHUNK):
        rows = slice(ch * D_CHUNK, (ch + 1) * D_CHUNK)
        u = jax.nn.gelu(u_ref[0, rows, :].astype(F32))
        v = jax.nn.gelu(v_ref[0, rows, :].astype(F32))
        dv = v - jnp.mean(v, axis=-1, keepdims=True)
        vn = dv * lax.rsqrt(jnp.mean(dv * dv, axis=-1, keepdims=True) + EPS) * lng_ref[...]
        full = jnp.dot(w_ref[...], vn.astype(BF16), preferred_element_type=F32)
        sg = b_ref[...]
        for g in range(D_GROUPS):
            sg = sg + jnp.where(group == g, full[g * D_CHUNK:(g + 1) * D_CHUNK, :], 0.0)
        o_ref[0, rows, :] = (u * sg).astype(BF16)


def _sgu(z, lng, w, bexp, tb):
    b, s, _ = z.shape
    col = lambda j: pl.BlockSpec((1, tb, HALF_W), lambda bb, i: (bb, i, j))
    return pl.pallas_call(
        _sgu_kernel,
        grid=(b, s // tb),
        in_specs=[col(4), col(5),
                  pl.BlockSpec((1, HALF_W), lambda bb, i: (0, 0)),
                  pl.BlockSpec((D_GROUPS * D_CHUNK, D_CHUNK), lambda bb, i: (0, 0)),
                  pl.BlockSpec((D_CHUNK, HALF_W), lambda bb, i: (0, 0))],
        out_specs=col(0),
        out_shape=jax.ShapeDtypeStruct((b, s, HALF_W), BF16),
        name="sgu",
        compiler_params=_cparams("parallel", "parallel"),
    )(z, z, lng, w, bexp)


def _tiles(s):
    return dict(tm=min(512, s), tf=1024, tq=min(256, s), tk=min(512, s), tband=min(256, s),
                tb_mlstm=min(1024, s), lc=128, tb_sgu=min(512, s))


def _to_heads(t, nh):
    b, s, _ = t.shape
    return t.reshape(b, s, nh, HEAD_DIM).transpose(0, 2, 1, 3)


def _from_heads(t):
    b, nh, s, dh = t.shape
    return t.transpose(0, 2, 1, 3).reshape(b, s, nh * dh)


def _trunk(x, c, p):
    b, s, d = x.shape
    t = _tiles(s)
    mod = _adaln(c, p["w_ada"], p["b_ada"])
    ta, tq, tk = _rope_tables(s, p["qk_norm_g"])
    for layer in range(2):
        m = mod[:, layer * 6 * d:(layer + 1) * 6 * d].reshape(b, 6, 1, d)
        sh1, sc1, g1, sh2, sc2, g2 = (m[:, j] for j in range(6))
        ng = p["norm_g"][layer].reshape(4, 1, d)
        if layer == 0:
            z = _inproj_even(x, ng[0], sc1, sh1, p["w_in_ab"], ta, tq, tk, t["tm"])
            qa = _to_heads(z[..., 0:512], A_HEADS)
            ka = _to_heads(z[..., 512:1024], A_HEADS)
            va = _to_heads(z[..., 1024:1536], A_HEADS)
            qb = _to_heads(z[..., 1536:2048], B_Q_HEADS)
            kb = _to_heads(z[..., 2048:2176], B_KV_HEADS)
            vb = _to_heads(z[..., 2176:2304], B_KV_HEADS)
            oa = _from_heads(_dilated_attention(qa, ka, va, t["tband"]))
            ob = _from_heads(_gqa_attention(qb, kb, vb, t["tq"], t["tk"]))
            x = _outproj_even(oa, ob, p["w_out_ab"], x, g1, ng[1], t["tm"])
        else:
            z, gates = _inproj_odd(x, ng[0], sc1, sh1, p["w_in_cd"], p["gate_bias"], t["tm"])
            hf = _mlstm(z, gates, False, t["tb_mlstm"], t["lc"])
            hb = _mlstm(z, gates, True, t["tb_mlstm"], t["lc"])
            dout = _sgu(z, p["sg_norm_g"], p["w_spatial"], p["b_spatial"], t["tb_sgu"])
            x = _outproj_odd(hf, hb, z, dout, p["mh_norm_g"], p["w_out_cd"], x, g1, ng[1], t["tm"])
        x = _ffn(x, ng[2], sc2, sh2, p["w_ff1"][layer], p["w_ff2"][layer], g2, ng[3],
                 t["tm"], t["tf"])
    return x


def _prepare_params(w_in_ab, w_out_ab, qk_norm_g, w_in_cd, w_out_cd, gate_bias, mh_norm_g,
                    sg_norm_g, w_spatial, b_spatial, w_ada, b_ada, norm_g, w_ff1, w_ff2):
    depth, d, _ = w_ada.shape
    wcd = w_in_cd[0]
    gate_lo = 4 * HALF_W
    gate_hi = gate_lo + N_GATES
    wcd = jnp.concatenate([wcd[:, :gate_lo], wcd[:, gate_hi:], wcd[:, gate_lo:gate_hi],
                           jnp.zeros((d, LANES - N_GATES), wcd.dtype)], axis=1)
    gb = jnp.concatenate([gate_bias[0].reshape(1, N_GATES).astype(F32),
                          jnp.zeros((1, LANES - N_GATES), F32)], axis=1)
    bexp = jnp.repeat(b_spatial[0].astype(F32).T, HEAD_DIM, axis=1)
    return dict(
        w_ada=jnp.concatenate([w_ada[l] for l in range(depth)], axis=1).astype(BF16),
        b_ada=b_ada.reshape(1, -1).astype(F32),
        w_in_ab=w_in_ab[0].astype(BF16), w_out_ab=w_out_ab[0].astype(BF16),
        qk_norm_g=qk_norm_g[0],
        w_in_cd=wcd.astype(BF16), w_out_cd=w_out_cd[0].astype(BF16), gate_bias=gb,
        mh_norm_g=mh_norm_g[0].reshape(1, HALF_W).astype(F32),
        sg_norm_g=sg_norm_g[0].reshape(1, HALF_W).astype(F32),
        w_spatial=w_spatial[0].reshape(D_GROUPS * D_CHUNK, D_CHUNK).astype(BF16),
        b_spatial=bexp,
        norm_g=norm_g.astype(F32), w_ff1=w_ff1.astype(BF16), w_ff2=w_ff2.astype(BF16))


def kernel(x_prompt, x_sample, c_prompt, c_sample, w_in_ab, w_out_ab, qk_norm_g, w_in_cd, w_out_cd, gate_bias, mh_norm_g, sg_norm_g, w_spatial, b_spatial, w_ada, b_ada, norm_g, w_ff1, w_ff2):
    p = _prepare_params(w_in_ab, w_out_ab, qk_norm_g, w_in_cd, w_out_cd, gate_bias, mh_norm_g,
                        sg_norm_g, w_spatial, b_spatial, w_ada, b_ada, norm_g, w_ff1, w_ff2)
    return (_trunk(x_prompt, c_prompt, p), _trunk(x_sample, c_sample, p))
```

```python
import functools

import numpy as np
import jax
import jax.numpy as jnp
from jax import lax
from jax.experimental import pallas as pl
from jax.experimental.pallas import tpu as pltpu

F32 = jnp.float32
BF16 = jnp.bfloat16

D_MODEL = 1024
HEAD_DIM = 64
GRID_W = 64
ROPE_THETA = 10000.0
EPS = 1e-6
A_HEADS = 8
A_PATTERNS = ((128, 1), (512, 4), (2048, 16))
B_Q_HEADS = 8
B_KV_HEADS = 2
C_HEADS = 8
D_GROUPS = 8
D_CHUNK = 128
D_FF = 4 * D_MODEL
HALF_W = 512
AB_IN = 3 * HALF_W + HALF_W + 2 * B_KV_HEADS * HEAD_DIM
CD_MAIN = 6 * HALF_W
N_GATES = 4 * C_HEADS

LANES = 128
MXU_N = 256
VMEM_LIMIT = 48 * 1024 * 1024

NEG = -1e30


def _cparams(*sem):
    return pltpu.CompilerParams(dimension_semantics=sem, vmem_limit_bytes=VMEM_LIMIT)


def _rms_rows(x):
    return x * lax.rsqrt(jnp.mean(x * x, axis=-1, keepdims=True) + EPS)


def _split_dot(a, b):
    hi = a.astype(BF16)
    r1 = a - hi.astype(F32)
    mid = r1.astype(BF16)
    lo = (r1 - mid.astype(F32)).astype(BF16)
    dot = functools.partial(jnp.dot, preferred_element_type=F32)
    return dot(hi, b) + dot(mid, b) + dot(lo, b)


def _split_dot_rhs(b, a):
    hi = a.astype(BF16)
    r1 = a - hi.astype(F32)
    mid = r1.astype(BF16)
    lo = (r1 - mid.astype(F32)).astype(BF16)
    dot = functools.partial(jnp.dot, preferred_element_type=F32)
    return dot(b, hi) + dot(b, mid) + dot(b, lo)


def _head_mean_square(z, ones_blk):
    z2 = z * z
    hi = z2.astype(BF16)
    lo = (z2 - hi.astype(F32)).astype(BF16)
    dot = functools.partial(jnp.dot, preferred_element_type=F32)
    return (dot(hi, ones_blk) + dot(lo, ones_blk)) * (1.0 / HEAD_DIM)


def _head_block_ones():
    r = lax.broadcasted_iota(jnp.int32, (LANES, LANES), 0) // HEAD_DIM
    c = lax.broadcasted_iota(jnp.int32, (LANES, LANES), 1) // HEAD_DIM
    return (r == c).astype(BF16)


def _log_sigmoid(x):
    return jnp.minimum(x, 0.0) - jnp.log1p(jnp.exp(-jnp.abs(x)))


def _dot_nt(a, b):
    return lax.dot_general(a, b, (((1,), (1,)), ((), ())), preferred_element_type=F32)


def _dot_tn(a, b):
    return lax.dot_general(a, b, (((0,), (0,)), ((), ())), preferred_element_type=F32)


def _adaln_kernel(c_ref, w_ref, b_ref, o_ref):
    c = c_ref[...]
    ca = c * jax.nn.sigmoid(c)
    o_ref[...] = jnp.dot(ca.astype(BF16), w_ref[...], preferred_element_type=F32) + b_ref[...]


def _adaln(c, w, b):
    nb, d = c.shape
    n = w.shape[1]
    tn = 2048
    return pl.pallas_call(
        _adaln_kernel,
        grid=(n // tn,),
        in_specs=[pl.BlockSpec((nb, d), lambda j: (0, 0)),
                  pl.BlockSpec((d, tn), lambda j: (0, j)),
                  pl.BlockSpec((1, tn), lambda j: (0, j))],
        out_specs=pl.BlockSpec((nb, tn), lambda j: (0, j)),
        out_shape=jax.ShapeDtypeStruct((nb, n), F32),
        name="adaln",
        compiler_params=_cparams("parallel"),
    )(c, w, b)


def _swap_halves(z, half, upper):
    return jnp.where(upper, pltpu.roll(z, LANES - half, axis=1), pltpu.roll(z, half, axis=1))


def _inproj_even_kernel(x_ref, g_ref, sc_ref, sh_ref, w_ref, ta_ref, tq_ref, tk_ref, z_ref):
    x = x_ref[0]
    h = _rms_rows(x) * (g_ref[...] * (1.0 + sc_ref[0])) + sh_ref[0]
    hb = h.astype(BF16)
    tm = x.shape[0]
    lane = lax.broadcasted_iota(jnp.int32, (tm, LANES), 1)
    up64 = (lane % 64) < 32
    up32 = (lane % 32) < 16
    ones_blk = _head_block_ones()
    q_scale = HEAD_DIM ** -0.5

    def rotary_1d(z):
        return z * ta_ref[:, :LANES] + _swap_halves(z, 32, up64) * ta_ref[:, LANES:]

    def norm_axial(z, t_ref):
        y = z * t_ref[:, :LANES] + _swap_halves(z, 16, up32) * t_ref[:, LANES:]
        return y * lax.rsqrt(_head_mean_square(z, ones_blk) + EPS)

    for j in range(AB_IN // MXU_N):
        zz = jnp.dot(hb, w_ref[:, j * MXU_N:(j + 1) * MXU_N], preferred_element_type=F32)
        for s in range(MXU_N // LANES):
            c = (MXU_N // LANES) * j + s
            z = zz[:, s * LANES:(s + 1) * LANES]
            if c < 4:
                z = rotary_1d(z) * q_scale
            elif c < 8:
                z = rotary_1d(z)
            elif 12 <= c < 16:
                z = norm_axial(z, tq_ref)
            elif c == 16:
                z = norm_axial(z, tk_ref)
            z_ref[0, :, c * LANES:(c + 1) * LANES] = z.astype(BF16)


def _inproj_even(x, g, sc, sh, w, ta, tq, tk, tm):
    b, s, d = x.shape
    n = w.shape[1]
    tab = pl.BlockSpec((tm, 2 * LANES), lambda i, bb: (i, 0))
    mod = pl.BlockSpec((1, 1, d), lambda i, bb: (bb, 0, 0))
    return pl.pallas_call(
        _inproj_even_kernel,
        grid=(s // tm, b),
        in_specs=[pl.BlockSpec((1, tm, d), lambda i, bb: (bb, i, 0)),
                  pl.BlockSpec((1, d), lambda i, bb: (0, 0)),
                  mod, mod,
                  pl.BlockSpec((d, n), lambda i, bb: (0, 0)),
                  tab, tab, tab],
        out_specs=pl.BlockSpec((1, tm, n), lambda i, bb: (bb, i, 0)),
        out_shape=jax.ShapeDtypeStruct((b, s, n), BF16),
        name="inproj_even",
        compiler_params=_cparams("parallel", "parallel"),
    )(x, g, sc, sh, w, ta, tq, tk)


def _rope_tables(s, qk_g):
    pos = jnp.arange(s)

    def angles(p, dim):
        inv_freq = ROPE_THETA ** (-jnp.arange(0, dim, 2, dtype=F32) / dim)
        return p.astype(F32)[:, None] * inv_freq[None, :]

    a1 = angles(pos, HEAD_DIM)
    cos1, sin1 = jnp.cos(a1), jnp.sin(a1)
    direct = jnp.concatenate([cos1, cos1], axis=-1)
    swapped = jnp.concatenate([-sin1, sin1], axis=-1)
    ta = jnp.concatenate([direct, direct, swapped, swapped], axis=-1)

    ar = angles(pos // GRID_W, HEAD_DIM // 2)
    ac = angles(pos % GRID_W, HEAD_DIM // 2)
    cosb = jnp.concatenate([jnp.cos(ar), jnp.cos(ar), jnp.cos(ac), jnp.cos(ac)], axis=-1)
    sinb = jnp.concatenate([-jnp.sin(ar), jnp.sin(ar), -jnp.sin(ac), jnp.sin(ac)], axis=-1)

    def gained(gain, scale):
        gain = gain.astype(F32)
        gswap = gain.reshape(2, 2, HEAD_DIM // 4)[:, ::-1].reshape(HEAD_DIM)
        direct = cosb * gain * scale
        swapped = sinb * gswap * scale
        return jnp.concatenate([direct, direct, swapped, swapped], axis=-1)

    return ta, gained(qk_g[0], HEAD_DIM ** -0.5), gained(qk_g[1], 1.0)


LOG2E = 1.4426950408889634
LN2 = 0.6931471805599453
BOUND_SLACK = 1.02
MIN_DENOM = 2.0 ** -60


def _sq_row_sums(x):
    return jnp.dot(x * x, jnp.ones((LANES, LANES), BF16), preferred_element_type=F32)


def _max_key_norm(k2d, tk):
    def body(j, mx):
        rows = pl.ds(pl.multiple_of(j * tk, tk), tk)
        return jnp.maximum(mx, jnp.max(_sq_row_sums(k2d[rows, :]), axis=0, keepdims=True))

    return jnp.sqrt(lax.fori_loop(0, k2d.shape[0] // tk, body, jnp.zeros((1, LANES), F32)))


def _bounded_queries(q, kmax, extra):
    bound = BOUND_SLACK * jnp.sqrt(_sq_row_sums(q)) * kmax + extra
    lane = lax.broadcasted_iota(jnp.int32, q.shape, 1)
    return jnp.where(lane == HEAD_DIM, -LOG2E * bound, LOG2E * q.astype(F32)).astype(BF16)


def _online_step(q, k, v, bias, m_ref, acc_ref):
    s = _dot_nt(q, k)
    if bias is not None:
        s = s + bias
    m_prev = m_ref[...]
    m_new = jnp.maximum(m_prev, jnp.max(s, axis=-1, keepdims=True))
    p = jnp.exp(s - m_new).astype(BF16)
    acc_ref[...] = jnp.exp(m_prev - m_new) * acc_ref[...] + jnp.dot(p, v, preferred_element_type=F32)
    m_ref[...] = m_new


def _online_init(m_ref, acc_ref):
    m_ref[...] = jnp.full(m_ref.shape, -jnp.inf, F32)
    acc_ref[...] = jnp.zeros(acc_ref.shape, F32)


def _denominators(acc_ref):
    return acc_ref[:, HEAD_DIM:HEAD_DIM + 1]


def _gqa_kernel(q_ref, k_ref, v_ref, o_ref, qa_ref, acc_ref, m_ref, kmax_ref, *, tk, rsub):
    g, tq, dp = q_ref.shape[1:]
    rows = g * tq
    nkv = k_ref.shape[2] // tk
    k2d = k_ref.at[0, 0]
    v2d = v_ref.at[0, 0]

    @pl.when(pl.program_id(2) == 0)
    def _():
        kmax_ref[...] = _max_key_norm(k2d, tk)

    q = q_ref[0].reshape(rows, dp)
    qa_ref[...] = _bounded_queries(q, kmax_ref[...], 0.0)
    acc_ref[...] = jnp.zeros(acc_ref.shape, F32)

    def bounded(j, carry):
        keys = pl.ds(pl.multiple_of(j * tk, tk), tk)
        kb = k2d[keys, :]
        vb = v2d[keys, :]
        for r in range(rows // rsub):
            rr = pl.ds(r * rsub, rsub)
            p = jnp.exp2(_dot_nt(qa_ref[rr, :], kb)).astype(BF16)
            acc_ref[rr, :] += jnp.dot(p, vb, preferred_element_type=F32)
        return carry

    lax.fori_loop(0, nkv, bounded, 0)

    @pl.when(jnp.min(_denominators(acc_ref)) < MIN_DENOM)
    def _():
        _online_init(m_ref, acc_ref)

        def online(j, carry):
            keys = pl.ds(pl.multiple_of(j * tk, tk), tk)
            _online_step(q, k2d[keys, :], v2d[keys, :], None, m_ref, acc_ref)
            return carry

        lax.fori_loop(0, nkv, online, 0)

    o = acc_ref[:, :HEAD_DIM] / _denominators(acc_ref)
    o_ref[0] = o.reshape(g, tq, HEAD_DIM).astype(o_ref.dtype)


def _gqa_attention(q, k, v, tq, tk):
    b, hq, s, dp = q.shape
    hkv = k.shape[1]
    g = hq // hkv
    rows = g * tq
    kv = pl.BlockSpec((1, 1, s, dp), lambda bb, hh, i: (bb, hh, 0, 0))
    return pl.pallas_call(
        functools.partial(_gqa_kernel, tk=tk, rsub=min(256, rows)),
        grid=(b, hkv, s // tq),
        in_specs=[pl.BlockSpec((1, g, tq, dp), lambda bb, hh, i: (bb, hh, i, 0)), kv, kv],
        out_specs=pl.BlockSpec((1, g, tq, HEAD_DIM), lambda bb, hh, i: (bb, hh, i, 0)),
        out_shape=jax.ShapeDtypeStruct((b, hq, s, HEAD_DIM), BF16),
        scratch_shapes=[pltpu.VMEM((rows, dp), BF16), pltpu.VMEM((rows, dp), F32),
                        pltpu.VMEM((rows, 1), F32), pltpu.VMEM((1, dp), F32)],
        name="gqa_attention",
        compiler_params=_cparams("parallel", "parallel", "arbitrary"),
    )(q, k, v)


def _band_offsets(t):
    reach = max(w // 2 for w, _ in A_PATTERNS)
    nb = -(-reach // t)
    offs = [0]
    for d in range(1, nb + 1):
        offs += [-d, d]
    return tuple(offs)


def _band_bias(t):
    offs = _band_offsets(t)
    a = np.arange(t)[:, None]
    bcol = np.arange(t)[None, :]
    out = np.zeros((len(offs), t, t), np.float64)
    for n, d in enumerate(offs):
        j = d * t + bcol - a
        mult = np.zeros((t, t), np.float64)
        for window, dil in A_PATTERNS:
            mult += ((j % dil) == 0) & (np.abs(j) <= window // 2)
        out[n] = np.where(mult > 0, LOG2E * np.log(np.maximum(mult, 1.0)), NEG)
    return jnp.asarray(out, F32)


def _band_kernel(q_ref, k_ref, v_ref, bias_ref, o_ref, qa_ref, acc_ref, m_ref, kmax_ref, *,
                 offsets, rsub):
    hb, t, dp = q_ref.shape[1:]
    nblk = k_ref.shape[2] // t
    i = pl.program_id(2)
    max_bias = float(np.log(len(A_PATTERNS)))

    def in_range(d):
        return jnp.logical_and(i + d >= 0, i + d < nblk)

    for h in range(hb):
        k2d = k_ref.at[0, h]
        v2d = v_ref.at[0, h]

        @pl.when(i == 0)
        def _():
            kmax_ref[h] = _max_key_norm(k2d, t)

        q = q_ref[0, h]
        qa_ref[...] = _bounded_queries(q, kmax_ref[h], max_bias)
        acc_ref[...] = jnp.zeros(acc_ref.shape, F32)
        for n, d in enumerate(offsets):
            @pl.when(in_range(d))
            def _():
                keys = pl.ds(pl.multiple_of((i + d) * t, t), t)
                kb = k2d[keys, :]
                vb = v2d[keys, :]
                for r in range(t // rsub):
                    rr = pl.ds(r * rsub, rsub)
                    s2 = _dot_nt(qa_ref[rr, :], kb) + bias_ref[n, rr, :]
                    acc_ref[rr, :] += jnp.dot(jnp.exp2(s2).astype(BF16), vb,
                                              preferred_element_type=F32)

        @pl.when(jnp.min(_denominators(acc_ref)) < MIN_DENOM)
        def _():
            _online_init(m_ref, acc_ref)
            for n, d in enumerate(offsets):
                @pl.when(in_range(d))
                def _():
                    keys = pl.ds(pl.multiple_of((i + d) * t, t), t)
                    _online_step(q, k2d[keys, :], v2d[keys, :], LN2 * bias_ref[n], m_ref, acc_ref)

        o = acc_ref[:, :HEAD_DIM] / _denominators(acc_ref)
        o_ref[0, h] = o.astype(o_ref.dtype)


def _dilated_attention(q, k, v, t, hb):
    b, h, s, dp = q.shape
    offsets = _band_offsets(t)
    bias = _band_bias(t)
    kv = pl.BlockSpec((1, hb, s, dp), lambda bb, hh, i: (bb, hh, 0, 0))
    return pl.pallas_call(
        functools.partial(_band_kernel, offsets=offsets, rsub=min(256, t)),
        grid=(b, h // hb, s // t),
        in_specs=[pl.BlockSpec((1, hb, t, dp), lambda bb, hh, i: (bb, hh, i, 0)), kv, kv,
                  pl.BlockSpec(bias.shape, lambda bb, hh, i: (0, 0, 0))],
        out_specs=pl.BlockSpec((1, hb, t, HEAD_DIM), lambda bb, hh, i: (bb, hh, i, 0)),
        out_shape=jax.ShapeDtypeStruct((b, h, s, HEAD_DIM), BF16),
        scratch_shapes=[pltpu.VMEM((t, dp), BF16), pltpu.VMEM((t, dp), F32),
                        pltpu.VMEM((t, 1), F32), pltpu.VMEM((hb, 1, dp), F32)],
        name="dilated_attention",
        compiler_params=_cparams("parallel", "parallel", "arbitrary"),
    )(q, k, v, bias)


def _residual_update(x, y, gate, norm_g):
    return x + gate * (_rms_rows(y) * norm_g)


def _outproj_even_kernel(oa_ref, ob_ref, w_ref, x_ref, gate_ref, ng_ref, o_ref):
    y = jnp.dot(oa_ref[0], w_ref[:HALF_W, :], preferred_element_type=F32)
    y = y + jnp.dot(ob_ref[0], w_ref[HALF_W:, :], preferred_element_type=F32)
    o_ref[0] = _residual_update(x_ref[0], y, gate_ref[0], ng_ref[...])


def _outproj_odd_kernel(hf_ref, hb_ref, og_ref, d_ref, mhg_ref, w_ref, x_ref, gate_ref, ng_ref,
                        o_ref):
    ones_blk = _head_block_ones()
    parts = []
    for c in range(HALF_W // LANES):
        cols = slice(c * LANES, (c + 1) * LANES)
        hs = hf_ref[0, :, cols] + hb_ref[0, :, cols]
        hn = hs * lax.rsqrt(_head_mean_square(hs, ones_blk) + EPS) * mhg_ref[:, cols]
        parts.append((hn * jax.nn.sigmoid(og_ref[0, :, cols].astype(F32))).astype(BF16))
    hc = jnp.concatenate(parts, axis=-1)
    y = jnp.dot(hc, w_ref[:HALF_W, :], preferred_element_type=F32)
    y = y + jnp.dot(d_ref[0], w_ref[HALF_W:, :], preferred_element_type=F32)
    o_ref[0] = _residual_update(x_ref[0], y, gate_ref[0], ng_ref[...])


def _outproj_specs(b, s, d, tm):
    half = pl.BlockSpec((1, tm, HALF_W), lambda bb, i: (bb, i, 0))
    full = pl.BlockSpec((1, tm, d), lambda bb, i: (bb, i, 0))
    w = pl.BlockSpec((d, d), lambda bb, i: (0, 0))
    mod = pl.BlockSpec((1, 1, d), lambda bb, i: (bb, 0, 0))
    vec = pl.BlockSpec((1, d), lambda bb, i: (0, 0))
    return half, full, w, mod, vec


def _outproj_even(oa, ob, w, x, gate, ng, tm):
    b, s, d = x.shape
    half, full, wspec, mod, vec = _outproj_specs(b, s, d, tm)
    return pl.pallas_call(
        _outproj_even_kernel,
        grid=(b, s // tm),
        in_specs=[half, half, wspec, full, mod, vec],
        out_specs=full,
        out_shape=jax.ShapeDtypeStruct((b, s, d), F32),
        name="outproj_even",
        compiler_params=_cparams("parallel", "parallel"),
    )(oa, ob, w, x, gate, ng)


def _outproj_odd(hf, hb, z, dout, mhg, w, x, gate, ng, tm):
    b, s, d = x.shape
    half, full, wspec, mod, vec = _outproj_specs(b, s, d, tm)
    og = pl.BlockSpec((1, tm, HALF_W), lambda bb, i: (bb, i, 3))
    return pl.pallas_call(
        _outproj_odd_kernel,
        grid=(b, s // tm),
        in_specs=[half, half, og, half, pl.BlockSpec((1, HALF_W), lambda bb, i: (0, 0)),
                  wspec, full, mod, vec],
        out_specs=full,
        out_shape=jax.ShapeDtypeStruct((b, s, d), F32),
        name="outproj_odd",
        compiler_params=_cparams("parallel", "parallel"),
    )(hf, hb, z, dout, mhg, w, x, gate, ng)


def _ffn_kernel(x_ref, g_ref, sc_ref, sh_ref, w1_ref, w2_ref, gate_ref, ng_ref, o_ref,
                h_ref, acc_ref):
    kk = pl.program_id(2)

    @pl.when(kk == 0)
    def _():
        h = _rms_rows(x_ref[0]) * (g_ref[...] * (1.0 + sc_ref[0])) + sh_ref[0]
        h_ref[...] = h.astype(BF16)
        acc_ref[...] = jnp.zeros(acc_ref.shape, F32)

    a = jnp.maximum(jnp.dot(h_ref[...], w1_ref[...], preferred_element_type=F32), 0.0)
    acc_ref[...] += jnp.dot((a * a).astype(BF16), w2_ref[...], preferred_element_type=F32)

    @pl.when(kk == pl.num_programs(2) - 1)
    def _():
        o_ref[0] = _residual_update(x_ref[0], acc_ref[...], gate_ref[0], ng_ref[...])


def _ffn(x, g, sc, sh, w1, w2, gate, ng, tm, tf):
    b, s, d = x.shape
    f = w1.shape[1]
    full = pl.BlockSpec((1, tm, d), lambda bb, i, k: (bb, i, 0))
    mod = pl.BlockSpec((1, 1, d), lambda bb, i, k: (bb, 0, 0))
    vec = pl.BlockSpec((1, d), lambda bb, i, k: (0, 0))
    return pl.pallas_call(
        _ffn_kernel,
        grid=(b, s // tm, f // tf),
        in_specs=[full, vec, mod, mod,
                  pl.BlockSpec((d, tf), lambda bb, i, k: (0, k)),
                  pl.BlockSpec((tf, d), lambda bb, i, k: (k, 0)),
                  mod, vec],
        out_specs=full,
        out_shape=jax.ShapeDtypeStruct((b, s, d), F32),
        scratch_shapes=[pltpu.VMEM((tm, d), BF16), pltpu.VMEM((tm, d), F32)],
        name="ffn",
        compiler_params=_cparams("parallel", "parallel", "arbitrary"),
    )(x, g, sc, sh, w1, w2, gate, ng)


def _inproj_odd_kernel(x_ref, g_ref, sc_ref, sh_ref, w_ref, gb_ref, z_ref, gates_ref):
    x = x_ref[0]
    h = _rms_rows(x) * (g_ref[...] * (1.0 + sc_ref[0])) + sh_ref[0]
    hb = h.astype(BF16)
    k_scale = HEAD_DIM ** -0.5
    for j in range(CD_MAIN // MXU_N):
        cols = slice(j * MXU_N, (j + 1) * MXU_N)
        zz = jnp.dot(hb, w_ref[:, cols], preferred_element_type=F32)
        if HALF_W <= j * MXU_N < 2 * HALF_W:
            zz = zz * k_scale
        z_ref[0, :, cols] = zz.astype(BF16)
    gates_ref[0] = jnp.dot(hb, w_ref[:, CD_MAIN:], preferred_element_type=F32) + gb_ref[...]


def _inproj_odd(x, g, sc, sh, w, gb, tm):
    b, s, d = x.shape
    n = w.shape[1]
    mod = pl.BlockSpec((1, 1, d), lambda i, bb: (bb, 0, 0))
    return pl.pallas_call(
        _inproj_odd_kernel,
        grid=(s // tm, b),
        in_specs=[pl.BlockSpec((1, tm, d), lambda i, bb: (bb, i, 0)),
                  pl.BlockSpec((1, d), lambda i, bb: (0, 0)),
                  mod, mod,
                  pl.BlockSpec((d, n), lambda i, bb: (0, 0)),
                  pl.BlockSpec((1, LANES), lambda i, bb: (0, 0))],
        out_specs=[pl.BlockSpec((1, tm, CD_MAIN), lambda i, bb: (bb, i, 0)),
                   pl.BlockSpec((1, tm, LANES), lambda i, bb: (bb, i, 0))],
        out_shape=[jax.ShapeDtypeStruct((b, s, CD_MAIN), BF16),
                   jax.ShapeDtypeStruct((b, s, LANES), F32)],
        name="inproj_odd",
        compiler_params=_cparams("parallel", "parallel"),
    )(x, g, sc, sh, w, gb)


def _mlstm_kernel(q_ref, k_ref, v_ref, g_ref, o_ref, c_sc, n_sc, m_sc, *, rev, lc):
    @pl.when(pl.program_id(1) == 0)
    def _():
        c_sc[...] = jnp.zeros(c_sc.shape, F32)
        n_sc[...] = jnp.zeros(n_sc.shape, F32)
        m_sc[...] = jnp.zeros(m_sc.shape, F32)

    nch = q_ref.shape[1] // lc
    r = lax.broadcasted_iota(jnp.int32, (lc, lc), 0)
    c = lax.broadcasted_iota(jnp.int32, (lc, lc), 1)
    sees = (c >= r) if rev else (c <= r)
    sees_b = sees.astype(BF16)
    sees_t_b = ((r >= c) if rev else (r <= c)).astype(BF16)
    base = 2 * C_HEADS if rev else 0
    edge = 0 if rev else lc - 1

    def chunk(ci, carry):
        cc = (nch - 1 - ci) if rev else ci
        rows = pl.ds(pl.multiple_of(cc * lc, lc), lc)
        gt = g_ref[0, rows, :]
        gtt = gt.T
        bcum = _split_dot_rhs(sees_b, _log_sigmoid(gt))
        bcum_t = _split_dot(_log_sigmoid(gtt), sees_t_b)
        for h in range(C_HEADS):
            il, fl = base + h, base + C_HEADS + h
            cols = slice(h * HEAD_DIM, (h + 1) * HEAD_DIM)
            q = q_ref[0, rows, cols]
            k = k_ref[0, rows, cols]
            v = v_ref[0, rows, cols]
            b_c = bcum[:, fl:fl + 1]
            i_c = gt[:, il:il + 1]
            b_r = bcum_t[fl:fl + 1, :]
            i_r = gtt[il:il + 1, :]
            m_prev = m_sc[h]
            log_d = jnp.where(sees, b_c - b_r + i_r, -jnp.inf)
            inter = b_c + m_prev
            m_t = jnp.maximum(inter, jnp.max(log_d, axis=-1, keepdims=True))
            dmat = jnp.exp(log_d - m_t)
            w_inter = jnp.exp(inter - m_t)
            sc = _dot_nt(q, k) * dmat
            c_prev = c_sc[h]
            n_prev = n_sc[h]
            num = jnp.dot(sc.astype(BF16), v, preferred_element_type=F32)
            num = num + w_inter * _dot_nt(q, c_prev.astype(BF16))
            den = jnp.sum(sc, axis=-1, keepdims=True)
            den = den + w_inter * jnp.sum(q.astype(F32) * n_prev, axis=-1, keepdims=True)
            o_ref[0, rows, cols] = num / jnp.maximum(jnp.abs(den), jnp.exp(-m_t))
            b_all = bcum[edge:edge + 1, fl:fl + 1]
            log_w = b_all - b_c + i_c
            m_new = jnp.maximum(b_all + m_prev, jnp.max(log_w, axis=0, keepdims=True))
            wk = jnp.exp(log_w - m_new)
            decay = jnp.exp(b_all + m_prev - m_new)
            vw = (wk * v.astype(F32)).astype(BF16)
            c_sc[h] = decay * c_prev + _dot_tn(vw, k)
            n_sc[h] = decay * n_prev + jnp.sum(wk * k.astype(F32), axis=0, keepdims=True)
            m_sc[h] = m_new
        return carry

    lax.fori_loop(0, nch, chunk, 0)


def _mlstm(z, gates, rev, tb, lc):
    b, s, _ = z.shape
    nblk = s // tb
    if rev:
        pos = lambda bb, i: nblk - 1 - i
    else:
        pos = lambda bb, i: i
    col = lambda j: pl.BlockSpec((1, tb, HALF_W), lambda bb, i: (bb, pos(bb, i), j))
    return pl.pallas_call(
        functools.partial(_mlstm_kernel, rev=rev, lc=lc),
        grid=(b, nblk),
        in_specs=[col(0), col(1), col(2),
                  pl.BlockSpec((1, tb, LANES), lambda bb, i: (bb, pos(bb, i), 0))],
        out_specs=col(0),
        out_shape=jax.ShapeDtypeStruct((b, s, HALF_W), F32),
        scratch_shapes=[pltpu.VMEM((C_HEADS, HEAD_DIM, HEAD_DIM), F32),
                        pltpu.VMEM((C_HEADS, 1, HEAD_DIM), F32),
                        pltpu.VMEM((C_HEADS, 1, 1), F32)],
        name="mlstm_bwd" if rev else "mlstm_fwd",
        compiler_params=_cparams("parallel", "arbitrary"),
    )(z, z, z, gates)


def _sgu_kernel(u_ref, v_ref, lng_ref, w_ref, b_ref, o_ref):
    tb = u_ref.shape[1]
    group = lax.broadcasted_iota(jnp.int32, (D_CHUNK, HALF_W), 1) // HEAD_DIM
    for ch in range(tb // D_CHUNK):
        rows = slice(ch * D_CHUNK, (ch + 1) * D_CHUNK)
        u = jax.nn.gelu(u_ref[0, rows, :].astype(F32))
        v = jax.nn.gelu(v_ref[0, rows, :].astype(F32))
        dv = v - jnp.mean(v, axis=-1, keepdims=True)
        vn = dv * lax.rsqrt(jnp.mean(dv * dv, axis=-1, keepdims=True) + EPS) * lng_ref[...]
        full = jnp.dot(w_ref[...], vn.astype(BF16), preferred_element_type=F32)
        sg = b_ref[...]
        for g in range(D_GROUPS):
            sg = sg + jnp.where(group == g, full[g * D_CHUNK:(g + 1) * D_CHUNK, :], 0.0)
        o_ref[0, rows, :] = (u * sg).astype(BF16)


def _sgu(z, lng, w, bexp, tb):
    b, s, _ = z.shape
    col = lambda j: pl.BlockSpec((1, tb, HALF_W), lambda bb, i: (bb, i, j))
    return pl.pallas_call(
        _sgu_kernel,
        grid=(b, s // tb),
        in_specs=[col(4), col(5),
                  pl.BlockSpec((1, HALF_W), lambda bb, i: (0, 0)),
                  pl.BlockSpec((D_GROUPS * D_CHUNK, D_CHUNK), lambda bb, i: (0, 0)),
                  pl.BlockSpec((D_CHUNK, HALF_W), lambda bb, i: (0, 0))],
        out_specs=col(0),
        out_shape=jax.ShapeDtypeStruct((b, s, HALF_W), BF16),
        name="sgu",
        compiler_params=_cparams("parallel", "parallel"),
    )(z, z, lng, w, bexp)


def _tiles(s):
    return dict(tm=min(512, s), tf=1024, tq=min(256, s), tk=min(512, s), tband=min(512, s),
                hband=2, tb_mlstm=min(1024, s), lc=128, tb_sgu=min(512, s))


def _to_heads(t, nh, ones_column):
    b, s, _ = t.shape
    t = t.reshape(b, s, nh, HEAD_DIM).transpose(0, 2, 1, 3)
    first = jnp.full((b, nh, s, 1), 1.0 if ones_column else 0.0, t.dtype)
    rest = jnp.zeros((b, nh, s, LANES - HEAD_DIM - 1), t.dtype)
    return jnp.concatenate([t, first, rest], axis=-1)


def _from_heads(t):
    b, nh, s, dh = t.shape
    return t.transpose(0, 2, 1, 3).reshape(b, s, nh * dh)


def _trunk(x, c, p):
    b, s, d = x.shape
    t = _tiles(s)
    mod = _adaln(c, p["w_ada"], p["b_ada"])
    ta, tq, tk = _rope_tables(s, p["qk_norm_g"])
    for layer in range(2):
        m = mod[:, layer * 6 * d:(layer + 1) * 6 * d].reshape(b, 6, 1, d)
        sh1, sc1, g1, sh2, sc2, g2 = (m[:, j] for j in range(6))
        ng = p["norm_g"][layer].reshape(4, 1, d)
        if layer == 0:
            z = _inproj_even(x, ng[0], sc1, sh1, p["w_in_ab"], ta, tq, tk, t["tm"])
            qa = _to_heads(z[..., 0:512], A_HEADS, False)
            ka = _to_heads(z[..., 512:1024], A_HEADS, True)
            va = _to_heads(z[..., 1024:1536], A_HEADS, True)
            qb = _to_heads(z[..., 1536:2048], B_Q_HEADS, False)
            kb = _to_heads(z[..., 2048:2176], B_KV_HEADS, True)
            vb = _to_heads(z[..., 2176:2304], B_KV_HEADS, True)
            oa = _from_heads(_dilated_attention(qa, ka, va, t["tband"], t["hband"]))
            ob = _from_heads(_gqa_attention(qb, kb, vb, t["tq"], t["tk"]))
            x = _outproj_even(oa, ob, p["w_out_ab"], x, g1, ng[1], t["tm"])
        else:
            z, gates = _inproj_odd(x, ng[0], sc1, sh1, p["w_in_cd"], p["gate_bias"], t["tm"])
            hf = _mlstm(z, gates, False, t["tb_mlstm"], t["lc"])
            hb = _mlstm(z, gates, True, t["tb_mlstm"], t["lc"])
            dout = _sgu(z, p["sg_norm_g"], p["w_spatial"], p["b_spatial"], t["tb_sgu"])
            x = _outproj_odd(hf, hb, z, dout, p["mh_norm_g"], p["w_out_cd"], x, g1, ng[1], t["tm"])
        x = _ffn(x, ng[2], sc2, sh2, p["w_ff1"][layer], p["w_ff2"][layer], g2, ng[3],
                 t["tm"], t["tf"])
    return x


def _prepare_params(w_in_ab, w_out_ab, qk_norm_g, w_in_cd, w_out_cd, gate_bias, mh_norm_g,
                    sg_norm_g, w_spatial, b_spatial, w_ada, b_ada, norm_g, w_ff1, w_ff2):
    depth, d, _ = w_ada.shape
    wcd = w_in_cd[0]
    gate_lo = 4 * HALF_W
    gate_hi = gate_lo + N_GATES
    wcd = jnp.concatenate([wcd[:, :gate_lo], wcd[:, gate_hi:], wcd[:, gate_lo:gate_hi],
                           jnp.zeros((d, LANES - N_GATES), wcd.dtype)], axis=1)
    gb = jnp.concatenate([gate_bias[0].reshape(1, N_GATES).astype(F32),
                          jnp.zeros((1, LANES - N_GATES), F32)], axis=1)
    bexp = jnp.repeat(b_spatial[0].astype(F32).T, HEAD_DIM, axis=1)
    return dict(
        w_ada=jnp.concatenate([w_ada[l] for l in range(depth)], axis=1).astype(BF16),
        b_ada=b_ada.reshape(1, -1).astype(F32),
        w_in_ab=w_in_ab[0].astype(BF16), w_out_ab=w_out_ab[0].astype(BF16),
        qk_norm_g=qk_norm_g[0],
        w_in_cd=wcd.astype(BF16), w_out_cd=w_out_cd[0].astype(BF16), gate_bias=gb,
        mh_norm_g=mh_norm_g[0].reshape(1, HALF_W).astype(F32),
        sg_norm_g=sg_norm_g[0].reshape(1, HALF_W).astype(F32),
        w_spatial=w_spatial[0].reshape(D_GROUPS * D_CHUNK, D_CHUNK).astype(BF16),
        b_spatial=bexp,
        norm_g=norm_g.astype(F32), w_ff1=w_ff1.astype(BF16), w_ff2=w_ff2.astype(BF16))


def kernel(x_prompt, x_sample, c_prompt, c_sample, w_in_ab, w_out_ab, qk_norm_g, w_in_cd, w_out_cd, gate_bias, mh_norm_g, sg_norm_g, w_spatial, b_spatial, w_ada, b_ada, norm_g, w_ff1, w_ff2):
    p = _prepare_params(w_in_ab, w_out_ab, qk_norm_g, w_in_cd, w_out_cd, gate_bias, mh_norm_g,
                        sg_norm_g, w_spatial, b_spatial, w_ada, b_ada, norm_g, w_ff1, w_ff2)
    return (_trunk(x_prompt, c_prompt, p), _trunk(x_sample, c_sample, p))
```

```python
import functools

import numpy as np
import jax
import jax.numpy as jnp
from jax import lax
from jax.experimental import pallas as pl
from jax.experimental.pallas import tpu as pltpu

F32 = jnp.float32
BF16 = jnp.bfloat16

D_MODEL = 1024
HEAD_DIM = 64
GRID_W = 64
ROPE_THETA = 10000.0
EPS = 1e-6
A_HEADS = 8
A_PATTERNS = ((128, 1), (512, 4), (2048, 16))
B_Q_HEADS = 8
B_KV_HEADS = 2
C_HEADS = 8
D_GROUPS = 8
D_CHUNK = 128
D_FF = 4 * D_MODEL
HALF_W = 512
AB_IN = 3 * HALF_W + HALF_W + 2 * B_KV_HEADS * HEAD_DIM
CD_MAIN = 6 * HALF_W
N_GATES = 4 * C_HEADS
GATE_W = 256
QUAD = 4
ONES_LANE_LOWER = HEAD_DIM
ONES_LANE_UPPER = 0

LANES = 128
MXU_N = 256
VMEM_LIMIT = 48 * 1024 * 1024

NEG = -1e30


def _cparams(*sem):
    return pltpu.CompilerParams(dimension_semantics=sem, vmem_limit_bytes=VMEM_LIMIT)


def _rms_rows(x):
    return x * lax.rsqrt(jnp.mean(x * x, axis=-1, keepdims=True) + EPS)


def _split_dot(a, b):
    hi = a.astype(BF16)
    r1 = a - hi.astype(F32)
    mid = r1.astype(BF16)
    lo = (r1 - mid.astype(F32)).astype(BF16)
    dot = functools.partial(jnp.dot, preferred_element_type=F32)
    return dot(hi, b) + dot(mid, b) + dot(lo, b)


def _split_dot_rhs(b, a):
    hi = a.astype(BF16)
    r1 = a - hi.astype(F32)
    mid = r1.astype(BF16)
    lo = (r1 - mid.astype(F32)).astype(BF16)
    dot = functools.partial(jnp.dot, preferred_element_type=F32)
    return dot(b, hi) + dot(b, mid) + dot(b, lo)


def _head_mean_square(z, ones_blk):
    z2 = z * z
    hi = z2.astype(BF16)
    lo = (z2 - hi.astype(F32)).astype(BF16)
    dot = functools.partial(jnp.dot, preferred_element_type=F32)
    return (dot(hi, ones_blk) + dot(lo, ones_blk)) * (1.0 / HEAD_DIM)


def _head_block_ones():
    r = lax.broadcasted_iota(jnp.int32, (LANES, LANES), 0) // HEAD_DIM
    c = lax.broadcasted_iota(jnp.int32, (LANES, LANES), 1) // HEAD_DIM
    return (r == c).astype(BF16)


def _log_sigmoid(x):
    return jnp.minimum(x, 0.0) - jnp.log1p(jnp.exp(-jnp.abs(x)))


def _dot_nt(a, b):
    return lax.dot_general(a, b, (((1,), (1,)), ((), ())), preferred_element_type=F32)


def _dot_tn(a, b):
    return lax.dot_general(a, b, (((0,), (0,)), ((), ())), preferred_element_type=F32)


def _adaln_kernel(c_ref, w_ref, b_ref, o_ref):
    c = c_ref[...]
    ca = c * jax.nn.sigmoid(c)
    o_ref[...] = jnp.dot(ca.astype(BF16), w_ref[...], preferred_element_type=F32) + b_ref[...]


def _adaln(c, w, b):
    nb, d = c.shape
    n = w.shape[1]
    tn = 2048
    return pl.pallas_call(
        _adaln_kernel,
        grid=(n // tn,),
        in_specs=[pl.BlockSpec((nb, d), lambda j: (0, 0)),
                  pl.BlockSpec((d, tn), lambda j: (0, j)),
                  pl.BlockSpec((1, tn), lambda j: (0, j))],
        out_specs=pl.BlockSpec((nb, tn), lambda j: (0, j)),
        out_shape=jax.ShapeDtypeStruct((nb, n), F32),
        name="adaln",
        compiler_params=_cparams("parallel"),
    )(c, w, b)


def _swap_halves(z, half, upper):
    return jnp.where(upper, pltpu.roll(z, LANES - half, axis=1), pltpu.roll(z, half, axis=1))


def _inproj_even_kernel(x_ref, g_ref, sc_ref, sh_ref, w_ref, ta_ref, tq_ref, tk_ref,
                        qa_ref, ka_ref, va_ref, qb_ref, kb_ref, vb_ref):
    x = x_ref[0]
    h = _rms_rows(x) * (g_ref[...] * (1.0 + sc_ref[0])) + sh_ref[0]
    hb = h.astype(BF16)
    tm = x.shape[0]
    lane = lax.broadcasted_iota(jnp.int32, (tm, LANES), 1)
    up64 = (lane % 64) < 32
    up32 = (lane % 32) < 16
    ones_blk = _head_block_ones()
    q_scale = HEAD_DIM ** -0.5

    def rotary_1d(z):
        return z * ta_ref[:, :LANES] + _swap_halves(z, 32, up64) * ta_ref[:, LANES:]

    def norm_axial(z, t_ref):
        y = z * t_ref[:, :LANES] + _swap_halves(z, 16, up32) * t_ref[:, LANES:]
        return y * lax.rsqrt(_head_mean_square(z, ones_blk) + EPS)

    lower = lane < HEAD_DIM

    def store_pair(ref, first, second, z, ones_column):
        fill_lo = (lane == ONES_LANE_LOWER).astype(F32) if ones_column else 0.0
        fill_up = (lane == ONES_LANE_UPPER).astype(F32) if ones_column else 0.0
        ref[0, first] = jnp.where(lower, z, fill_lo).astype(BF16)
        ref[0, second] = jnp.where(lower, fill_up, z).astype(BF16)

    for j in range(AB_IN // MXU_N):
        zz = jnp.dot(hb, w_ref[:, j * MXU_N:(j + 1) * MXU_N], preferred_element_type=F32)
        for s in range(MXU_N // LANES):
            c = (MXU_N // LANES) * j + s
            z = zz[:, s * LANES:(s + 1) * LANES]
            if c < 4:
                store_pair(qa_ref, 2 * c, 2 * c + 1, rotary_1d(z) * q_scale, False)
            elif c < 8:
                store_pair(ka_ref, 2 * (c - 4), 2 * (c - 4) + 1, rotary_1d(z), True)
            elif c < 12:
                store_pair(va_ref, 2 * (c - 8), 2 * (c - 8) + 1, z, True)
            elif c < 16:
                store_pair(qb_ref, c - 12, c - 12 + B_Q_HEADS // 2, norm_axial(z, tq_ref), False)
            elif c == 16:
                store_pair(kb_ref, 0, 1, norm_axial(z, tk_ref), True)
            else:
                store_pair(vb_ref, 0, 1, z, True)


def _inproj_even(x, g, sc, sh, w, ta, tq, tk, tm):
    b, s, d = x.shape
    n = w.shape[1]
    tab = pl.BlockSpec((tm, 2 * LANES), lambda i, bb: (i, 0))
    mod = pl.BlockSpec((1, 1, d), lambda i, bb: (bb, 0, 0))
    heads = (A_HEADS, A_HEADS, A_HEADS, B_Q_HEADS, B_KV_HEADS, B_KV_HEADS)
    return pl.pallas_call(
        _inproj_even_kernel,
        grid=(s // tm, b),
        in_specs=[pl.BlockSpec((1, tm, d), lambda i, bb: (bb, i, 0)),
                  pl.BlockSpec((1, d), lambda i, bb: (0, 0)),
                  mod, mod,
                  pl.BlockSpec((d, n), lambda i, bb: (0, 0)),
                  tab, tab, tab],
        out_specs=[pl.BlockSpec((1, nh, tm, LANES), lambda i, bb: (bb, 0, i, 0)) for nh in heads],
        out_shape=[jax.ShapeDtypeStruct((b, nh, s, LANES), BF16) for nh in heads],
        name="inproj_even",
        compiler_params=_cparams("parallel", "parallel"),
    )(x, g, sc, sh, w, ta, tq, tk)


def _rope_tables(s, qk_g):
    pos = jnp.arange(s)

    def angles(p, dim):
        inv_freq = ROPE_THETA ** (-jnp.arange(0, dim, 2, dtype=F32) / dim)
        return p.astype(F32)[:, None] * inv_freq[None, :]

    a1 = angles(pos, HEAD_DIM)
    cos1, sin1 = jnp.cos(a1), jnp.sin(a1)
    direct = jnp.concatenate([cos1, cos1], axis=-1)
    swapped = jnp.concatenate([-sin1, sin1], axis=-1)
    ta = jnp.concatenate([direct, direct, swapped, swapped], axis=-1)

    ar = angles(pos // GRID_W, HEAD_DIM // 2)
    ac = angles(pos % GRID_W, HEAD_DIM // 2)
    cosb = jnp.concatenate([jnp.cos(ar), jnp.cos(ar), jnp.cos(ac), jnp.cos(ac)], axis=-1)
    sinb = jnp.concatenate([-jnp.sin(ar), jnp.sin(ar), -jnp.sin(ac), jnp.sin(ac)], axis=-1)

    def gained(gain, scale):
        gain = gain.astype(F32)
        gswap = gain.reshape(2, 2, HEAD_DIM // 4)[:, ::-1].reshape(HEAD_DIM)
        direct = cosb * gain * scale
        swapped = sinb * gswap * scale
        return jnp.concatenate([direct, direct, swapped, swapped], axis=-1)

    return ta, gained(qk_g[0], HEAD_DIM ** -0.5), gained(qk_g[1], 1.0)


LOG2E = 1.4426950408889634
LN2 = 0.6931471805599453
BOUND_SLACK = 1.02
MIN_DENOM = 2.0 ** -60


def _sq_row_sums(x):
    return jnp.dot(x * x, jnp.ones((LANES, LANES), BF16), preferred_element_type=F32)


def _max_key_norm(k2d, tk):
    def body(j, mx):
        rows = pl.ds(pl.multiple_of(j * tk, tk), tk)
        return jnp.maximum(mx, jnp.max(_sq_row_sums(k2d[rows, :]), axis=0, keepdims=True))

    return jnp.sqrt(lax.fori_loop(0, k2d.shape[0] // tk, body, jnp.zeros((1, LANES), F32)))


def _ones_lane(upper):
    return ONES_LANE_UPPER if upper else ONES_LANE_LOWER


def _bounded_queries(q, kmax, extra, upper):
    bound = BOUND_SLACK * jnp.sqrt(_sq_row_sums(q)) * kmax + extra
    lane = lax.broadcasted_iota(jnp.int32, q.shape, 1)
    return jnp.where(lane == _ones_lane(upper), -LOG2E * bound, LOG2E * q.astype(F32)).astype(BF16)


def _online_step(q, k, v, bias, m_ref, acc_ref):
    s = _dot_nt(q, k)
    if bias is not None:
        s = s + bias
    m_prev = m_ref[...]
    m_new = jnp.maximum(m_prev, jnp.max(s, axis=-1, keepdims=True))
    p = jnp.exp(s - m_new).astype(BF16)
    acc_ref[...] = jnp.exp(m_prev - m_new) * acc_ref[...] + jnp.dot(p, v, preferred_element_type=F32)
    m_ref[...] = m_new


def _online_init(m_ref, acc_ref):
    m_ref[...] = jnp.full(m_ref.shape, -jnp.inf, F32)
    acc_ref[...] = jnp.zeros(acc_ref.shape, F32)


def _denominators(acc, upper):
    lane = _ones_lane(upper)
    return acc[:, lane:lane + 1]


def _normalised_pair(acc_lower, acc_upper):
    lane = lax.broadcasted_iota(jnp.int32, acc_lower.shape, 1)
    return jnp.where(lane < HEAD_DIM, acc_lower / _denominators(acc_lower, False),
                     acc_upper / _denominators(acc_upper, True))


def _gqa_kernel(q_ref, k_ref, v_ref, o_ref, qa_ref, acc_ref, m_ref, kmax_ref, *, tk, rsub):
    nh, tq, dp = q_ref.shape[1:]
    gsz = nh // 2
    grows = gsz * tq
    nkv = k_ref.shape[2] // tk

    def group_q(g):
        return q_ref[0, g * gsz:(g + 1) * gsz].reshape(grows, dp)

    @pl.when(pl.program_id(1) == 0)
    def _():
        for g in range(2):
            kmax_ref[g] = _max_key_norm(k_ref.at[0, g], tk)

    for g in range(2):
        qa_ref[g * grows:(g + 1) * grows, :] = _bounded_queries(group_q(g), kmax_ref[g], 0.0, g == 1)
    acc_ref[...] = jnp.zeros(acc_ref.shape, F32)

    def bounded(j, carry):
        keys = pl.ds(pl.multiple_of(j * tk, tk), tk)
        for g in range(2):
            kb = k_ref[0, g, keys, :]
            vb = v_ref[0, g, keys, :]
            for r in range(grows // rsub):
                rr = pl.ds(g * grows + r * rsub, rsub)
                p = jnp.exp2(_dot_nt(qa_ref[rr, :], kb)).astype(BF16)
                acc_ref[rr, :] += jnp.dot(p, vb, preferred_element_type=F32)
        return carry

    lax.fori_loop(0, nkv, bounded, 0, unroll=2 if nkv % 2 == 0 else 1)

    smallest =jnp.minimum(jnp.min(_denominators(acc_ref[:grows, :], False)),
                           jnp.min(_denominators(acc_ref[grows:, :], True)))

    @pl.when(smallest < MIN_DENOM)
    def _():
        _online_init(m_ref, acc_ref)

        def online(j, carry):
            keys = pl.ds(pl.multiple_of(j * tk, tk), tk)
            for g in range(2):
                rr = pl.ds(g * grows, grows)
                _online_step(group_q(g), k_ref[0, g, keys, :], v_ref[0, g, keys, :], None,
                             m_ref.at[rr, :], acc_ref.at[rr, :])
            return carry

        lax.fori_loop(0, nkv, online, 0)

    for j in range(gsz):
        lo = acc_ref[j * tq:(j + 1) * tq, :]
        up = acc_ref[grows + j * tq:grows + (j + 1) * tq, :]
        o_ref[0, :, j * LANES:(j + 1) * LANES] = _normalised_pair(lo, up).astype(o_ref.dtype)


def _gqa_attention(q, k, v, tq, tk):
    b, hq, s, dp = q.shape
    assert k.shape[1] == 2 and hq % 2 == 0
    rows = hq * tq
    kv = pl.BlockSpec((1, 2, s, dp), lambda bb, i: (bb, 0, 0, 0))
    return pl.pallas_call(
        functools.partial(_gqa_kernel, tk=tk, rsub=min(256, rows // 2)),
        grid=(b, s // tq),
        in_specs=[pl.BlockSpec((1, hq, tq, dp), lambda bb, i: (bb, 0, i, 0)), kv, kv],
        out_specs=pl.BlockSpec((1, tq, hq * HEAD_DIM), lambda bb, i: (bb, i, 0)),
        out_shape=jax.ShapeDtypeStruct((b, s, hq * HEAD_DIM), BF16),
        scratch_shapes=[pltpu.VMEM((rows, dp), BF16), pltpu.VMEM((rows, dp), F32),
                        pltpu.VMEM((rows, 1), F32), pltpu.VMEM((2, 1, dp), F32)],
        name="gqa_attention",
        compiler_params=_cparams("parallel", "arbitrary"),
    )(q, k, v)


def _band_offsets(t):
    reach = max(w // 2 for w, _ in A_PATTERNS)
    nb = -(-reach // t)
    offs = [0]
    for d in range(1, nb + 1):
        offs += [-d, d]
    return tuple(offs)


def _band_bias(t):
    offs = _band_offsets(t)
    a = np.arange(t)[:, None]
    bcol = np.arange(t)[None, :]
    out = np.zeros((len(offs), t, t), np.float64)
    for n, d in enumerate(offs):
        j = d * t + bcol - a
        mult = np.zeros((t, t), np.float64)
        for window, dil in A_PATTERNS:
            mult += ((j % dil) == 0) & (np.abs(j) <= window // 2)
        out[n] = np.where(mult > 0, LOG2E * np.log(np.maximum(mult, 1.0)), NEG)
    return jnp.asarray(out, F32)


def _band_kernel(q_ref, k_ref, v_ref, bias_ref, o_ref, qa_ref, acc_ref, m_ref, kmax_ref, *,
                 offsets, rsub):
    t, dp = q_ref.shape[2:]
    nblk = k_ref.shape[2] // t
    i = pl.program_id(2)
    max_bias = float(np.log(len(A_PATTERNS)))

    def in_range(d):
        return jnp.logical_and(i + d >= 0, i + d < nblk)

    for h in range(2):
        upper = h == 1
        k2d = k_ref.at[0, h]
        v2d = v_ref.at[0, h]
        acc = acc_ref.at[h]

        @pl.when(i == 0)
        def _():
            kmax_ref[h] = _max_key_norm(k2d, t)

        q = q_ref[0, h]
        qa_ref[...] = _bounded_queries(q, kmax_ref[h], max_bias, upper)
        acc[...] = jnp.zeros(acc.shape, F32)
        for n, d in enumerate(offsets):
            @pl.when(in_range(d))
            def _():
                keys = pl.ds(pl.multiple_of((i + d) * t, t), t)
                kb = k2d[keys, :]
                vb = v2d[keys, :]
                for r in range(t // rsub):
                    rr = pl.ds(r * rsub, rsub)
                    s2 = _dot_nt(qa_ref[rr, :], kb) + bias_ref[n, rr, :]
                    acc[rr, :] += jnp.dot(jnp.exp2(s2).astype(BF16), vb,
                                          preferred_element_type=F32)

        @pl.when(jnp.min(_denominators(acc[...], upper)) < MIN_DENOM)
        def _():
            _online_init(m_ref, acc)
            for n, d in enumerate(offsets):
                @pl.when(in_range(d))
                def _():
                    keys = pl.ds(pl.multiple_of((i + d) * t, t), t)
                    _online_step(q, k2d[keys, :], v2d[keys, :], LN2 * bias_ref[n], m_ref, acc)

    o_ref[0] = _normalised_pair(acc_ref[0], acc_ref[1]).astype(o_ref.dtype)


def _dilated_attention(q, k, v, t):
    b, h, s, dp = q.shape
    offsets = _band_offsets(t)
    bias = _band_bias(t)
    kv = pl.BlockSpec((1, 2, s, dp), lambda bb, hh, i: (bb, hh, 0, 0))
    return pl.pallas_call(
        functools.partial(_band_kernel, offsets=offsets, rsub=min(256, t)),
        grid=(b, h // 2, s // t),
        in_specs=[pl.BlockSpec((1, 2, t, dp), lambda bb, hh, i: (bb, hh, i, 0)), kv, kv,
                  pl.BlockSpec(bias.shape, lambda bb, hh, i: (0, 0, 0))],
        out_specs=pl.BlockSpec((1, t, LANES), lambda bb, hh, i: (bb, i, hh)),
        out_shape=jax.ShapeDtypeStruct((b, s, h * HEAD_DIM), BF16),
        scratch_shapes=[pltpu.VMEM((t, dp), BF16), pltpu.VMEM((2, t, dp), F32),
                        pltpu.VMEM((t, 1), F32), pltpu.VMEM((2, 1, dp), F32)],
        name="dilated_attention",
        compiler_params=_cparams("parallel", "parallel", "arbitrary"),
    )(q, k, v, bias)


def _residual_update(x, y, gate, norm_g):
    return x + gate * (_rms_rows(y) * norm_g)


def _outproj_even_kernel(oa_ref, ob_ref, w_ref, x_ref, gate_ref, ng_ref, o_ref):
    y = jnp.dot(oa_ref[0], w_ref[:HALF_W, :], preferred_element_type=F32)
    y = y + jnp.dot(ob_ref[0], w_ref[HALF_W:, :], preferred_element_type=F32)
    o_ref[0] = _residual_update(x_ref[0], y, gate_ref[0], ng_ref[...])


def _outproj_odd_kernel(hf_ref, hb_ref, og_ref, d_ref, mhg_ref, w_ref, x_ref, gate_ref, ng_ref,
                        o_ref):
    ones_blk = _head_block_ones()
    parts = []
    for c in range(HALF_W // LANES):
        cols = slice(c * LANES, (c + 1) * LANES)
        hs = hf_ref[0, :, cols] + hb_ref[0, :, cols]
        hn = hs * lax.rsqrt(_head_mean_square(hs, ones_blk) + EPS) * mhg_ref[:, cols]
        parts.append((hn * jax.nn.sigmoid(og_ref[0, :, cols].astype(F32))).astype(BF16))
    hc = jnp.concatenate(parts, axis=-1)
    y = jnp.dot(hc, w_ref[:HALF_W, :], preferred_element_type=F32)
    y = y + jnp.dot(d_ref[0], w_ref[HALF_W:, :], preferred_element_type=F32)
    o_ref[0] = _residual_update(x_ref[0], y, gate_ref[0], ng_ref[...])


def _outproj_specs(b, s, d, tm):
    half = pl.BlockSpec((1, tm, HALF_W), lambda bb, i: (bb, i, 0))
    full = pl.BlockSpec((1, tm, d), lambda bb, i: (bb, i, 0))
    w = pl.BlockSpec((d, d), lambda bb, i: (0, 0))
    mod = pl.BlockSpec((1, 1, d), lambda bb, i: (bb, 0, 0))
    vec = pl.BlockSpec((1, d), lambda bb, i: (0, 0))
    return half, full, w, mod, vec


def _outproj_even(oa, ob, w, x, gate, ng, tm):
    b, s, d = x.shape
    half, full, wspec, mod, vec = _outproj_specs(b, s, d, tm)
    return pl.pallas_call(
        _outproj_even_kernel,
        grid=(b, s // tm),
        in_specs=[half, half, wspec, full, mod, vec],
        out_specs=full,
        out_shape=jax.ShapeDtypeStruct((b, s, d), F32),
        name="outproj_even",
        compiler_params=_cparams("parallel", "parallel"),
    )(oa, ob, w, x, gate, ng)


def _outproj_odd(hf, hb, z, dout, mhg, w, x, gate, ng, tm):
    b, s, d = x.shape
    half, full, wspec, mod, vec = _outproj_specs(b, s, d, tm)
    og = pl.BlockSpec((1, tm, HALF_W), lambda bb, i: (bb, i, 3))
    return pl.pallas_call(
        _outproj_odd_kernel,
        grid=(b, s // tm),
        in_specs=[half, half, og, half, pl.BlockSpec((1, HALF_W), lambda bb, i: (0, 0)),
                  wspec, full, mod, vec],
        out_specs=full,
        out_shape=jax.ShapeDtypeStruct((b, s, d), F32),
        name="outproj_odd",
        compiler_params=_cparams("parallel", "parallel"),
    )(hf, hb, z, dout, mhg, w, x, gate, ng)


def _ffn_kernel(x_ref, g_ref, sc_ref, sh_ref, w1_ref, w2_ref, gate_ref, ng_ref, o_ref,
                h_ref, acc_ref):
    kk = pl.program_id(2)

    @pl.when(kk == 0)
    def _():
        h = _rms_rows(x_ref[0]) * (g_ref[...] * (1.0 + sc_ref[0])) + sh_ref[0]
        h_ref[...] = h.astype(BF16)
        acc_ref[...] = jnp.zeros(acc_ref.shape, F32)

    a = jnp.maximum(jnp.dot(h_ref[...], w1_ref[...], preferred_element_type=F32), 0.0)
    acc_ref[...] += jnp.dot((a * a).astype(BF16), w2_ref[...], preferred_element_type=F32)

    @pl.when(kk == pl.num_programs(2) - 1)
    def _():
        o_ref[0] = _residual_update(x_ref[0], acc_ref[...], gate_ref[0], ng_ref[...])


def _ffn(x, g, sc, sh, w1, w2, gate, ng, tm, tf):
    b, s, d = x.shape
    f = w1.shape[1]
    full = pl.BlockSpec((1, tm, d), lambda bb, i, k: (bb, i, 0))
    mod = pl.BlockSpec((1, 1, d), lambda bb, i, k: (bb, 0, 0))
    vec = pl.BlockSpec((1, d), lambda bb, i, k: (0, 0))
    return pl.pallas_call(
        _ffn_kernel,
        grid=(b, s // tm, f // tf),
        in_specs=[full, vec, mod, mod,
                  pl.BlockSpec((d, tf), lambda bb, i, k: (0, k)),
                  pl.BlockSpec((tf, d), lambda bb, i, k: (k, 0)),
                  mod, vec],
        out_specs=full,
        out_shape=jax.ShapeDtypeStruct((b, s, d), F32),
        scratch_shapes=[pltpu.VMEM((tm, d), BF16), pltpu.VMEM((tm, d), F32)],
        name="ffn",
        compiler_params=_cparams("parallel", "parallel", "arbitrary"),
    )(x, g, sc, sh, w1, w2, gate, ng)


def _inproj_odd_kernel(x_ref, g_ref, sc_ref, sh_ref, w_ref, gb_ref, z_ref, gates_ref):
    x = x_ref[0]
    h = _rms_rows(x) * (g_ref[...] * (1.0 + sc_ref[0])) + sh_ref[0]
    hb = h.astype(BF16)
    k_scale = HEAD_DIM ** -0.5
    for j in range(CD_MAIN // MXU_N):
        cols = slice(j * MXU_N, (j + 1) * MXU_N)
        zz = jnp.dot(hb, w_ref[:, cols], preferred_element_type=F32)
        if HALF_W <= j * MXU_N < 2 * HALF_W:
            zz = zz * k_scale
        z_ref[0, :, cols] = zz.astype(BF16)
    gates_ref[0] = jnp.dot(hb, w_ref[:, CD_MAIN:], preferred_element_type=F32) + gb_ref[...]


def _inproj_odd(x, g, sc, sh, w, gb, tm):
    b, s, d = x.shape
    n = w.shape[1]
    mod = pl.BlockSpec((1, 1, d), lambda i, bb: (bb, 0, 0))
    return pl.pallas_call(
        _inproj_odd_kernel,
        grid=(s // tm, b),
        in_specs=[pl.BlockSpec((1, tm, d), lambda i, bb: (bb, i, 0)),
                  pl.BlockSpec((1, d), lambda i, bb: (0, 0)),
                  mod, mod,
                  pl.BlockSpec((d, n), lambda i, bb: (0, 0)),
                  pl.BlockSpec((1, GATE_W), lambda i, bb: (0, 0))],
        out_specs=[pl.BlockSpec((1, tm, CD_MAIN), lambda i, bb: (bb, i, 0)),
                   pl.BlockSpec((1, tm, GATE_W), lambda i, bb: (bb, i, 0))],
        out_shape=[jax.ShapeDtypeStruct((b, s, CD_MAIN), BF16),
                   jax.ShapeDtypeStruct((b, s, GATE_W), F32)],
        name="inproj_odd",
        compiler_params=_cparams("parallel", "parallel"),
    )(x, g, sc, sh, w, gb)


def _gate_lane(rev, head):
    return (C_HEADS if rev else 0) + head


def _mlstm_constants(rev, lc):
    wq = QUAD * HEAD_DIM
    t = np.arange(lc)
    sees = (t[None, :] >= t[:, None]) if rev else (t[None, :] <= t[:, None])
    nq = C_HEADS // QUAD
    sel_s = np.zeros((nq, LANES, QUAD * lc), np.float32)
    sel_v = np.zeros((nq, LANES, wq), np.float32)
    for qd in range(nq):
        for hh in range(QUAD):
            lane = _gate_lane(rev, QUAD * qd + hh)
            sel_s[qd, lane, hh * lc:(hh + 1) * lc] = 1.0
            sel_v[qd, lane, hh * HEAD_DIM:(hh + 1) * HEAD_DIM] = 1.0
    head_of_row = np.arange(QUAD * lc) // lc
    head_of_col = np.arange(wq) // HEAD_DIM
    return dict(
        tri=jnp.asarray(sees, BF16),
        sel_s=jnp.asarray(sel_s, BF16), sel_v=jnp.asarray(sel_v, BF16),
        ones_bd=jnp.asarray(np.transpose(sel_s, (0, 2, 1)), BF16),
        row_head=jnp.asarray(head_of_row[:, None] == head_of_col[None, :], BF16),
        diag=jnp.asarray(head_of_col[:, None] == head_of_col[None, :], F32),
        causal=jnp.asarray(np.tile(sees, (1, QUAD)), F32))


def _scan_max(x, rev):
    n = x.shape[0]
    row = lax.broadcasted_iota(jnp.int32, x.shape, 0)
    sh = 1
    while sh < n:
        if rev:
            x = jnp.maximum(x, jnp.where(row < n - sh, pltpu.roll(x, n - sh, axis=0), -jnp.inf))
        else:
            x = jnp.maximum(x, jnp.where(row >= sh, pltpu.roll(x, sh, axis=0), -jnp.inf))
        sh *= 2
    return x


def _mlstm_kernel(q_ref, k_ref, v_ref, g_ref, tri_ref, sels_ref, selv_ref, onesbd_ref,
                  rowhead_ref, diag_ref, causal_ref, o_ref, c_sc, n_sc, m_sc, *, rev, lc):
    @pl.when(pl.program_id(1) == 0)
    def _():
        c_sc[...] = jnp.zeros(c_sc.shape, F32)
        n_sc[...] = jnp.zeros(n_sc.shape, F32)
        m_sc[...] = jnp.zeros(m_sc.shape, F32)

    nch = q_ref.shape[1] // lc
    wq = QUAD * HEAD_DIM
    edge = 0 if rev else lc - 1
    dot = functools.partial(jnp.dot, preferred_element_type=F32)

    def chunk(ci, carry):
        cc = (nch - 1 - ci) if rev else ci
        rows = pl.ds(pl.multiple_of(cc * lc, lc), lc)
        gi = g_ref[0, rows, :LANES]
        bcum = _split_dot_rhs(tri_ref[...], _log_sigmoid(g_ref[0, rows, LANES:]))
        a = gi - bcum
        m_prev = m_sc[...]
        mt_b = jnp.maximum(m_prev, _scan_max(a, rev)).astype(BF16)
        mt = mt_b.astype(F32)
        w_b = jnp.exp(m_prev - mt).astype(BF16)
        w = w_b.astype(F32)
        floor = jnp.exp(-(bcum + mt))
        b_all = bcum[edge:edge + 1, :]
        log_w = b_all + a
        m_new = jnp.maximum(b_all + m_prev, jnp.max(log_w, axis=0, keepdims=True))
        wk_b = jnp.exp(log_w - m_new).astype(BF16)
        decay = jnp.broadcast_to(jnp.exp(b_all + m_prev - m_new), (8, LANES))
        a_t = a.T
        for qd in range(C_HEADS // QUAD):
            cols = slice(qd * wq, (qd + 1) * wq)
            q = q_ref[0, rows, cols]
            k = k_ref[0, rows, cols]
            v = v_ref[0, rows, cols]
            row_head = rowhead_ref[...]
            sel_v = selv_ref[qd]
            kbd = jnp.concatenate([k] * QUAD, axis=0) * row_head
            a_rows = jnp.concatenate(
                [jnp.broadcast_to(a_t[ln:ln + 1, :], (lc, lc))
                 for ln in (_gate_lane(rev, QUAD * qd + hh) for hh in range(QUAD))],
                axis=1)
            log_e = jnp.where(causal_ref[...] > 0.0, a_rows - dot(mt_b, sels_ref[qd]), NEG)
            p = (_dot_nt(q, kbd) * jnp.exp(log_e)).astype(BF16)
            vaug = jnp.concatenate(
                [jnp.concatenate([v] * QUAD, axis=0) * row_head, onesbd_ref[qd]], axis=1)
            pv = dot(p, vaug)
            c_prev = c_sc[qd]
            n_prev = n_sc[qd]
            caug = jnp.concatenate(
                [c_prev.astype(BF16), (sel_v.astype(F32) * n_prev).astype(BF16)], axis=0)
            qc = _dot_nt(q, caug)
            den = pv[:, wq:] + w * qc[:, wq:]
            scale_b = (1.0 / jnp.maximum(jnp.abs(den), floor)).astype(BF16)
            o_ref[0, rows, cols] = dot(scale_b, sel_v) * (pv[:, :wq] + dot(w_b, sel_v) * qc[:, :wq])
            wk_v = dot(wk_b, sel_v)
            dec_v = _split_dot(decay, sel_v)[:1]
            vw = (wk_v * v.astype(F32)).astype(BF16)
            c_sc[qd] = dec_v * c_prev + diag_ref[...] * _dot_tn(vw, k)
            n_sc[qd] = dec_v * n_prev + jnp.sum(wk_v * k.astype(F32), axis=0, keepdims=True)
        m_sc[...] = m_new
        return carry

    lax.fori_loop(0, nch, chunk, 0)


def _mlstm(z, gates, rev, tb, lc):
    b, s, _ = z.shape
    nblk = s // tb
    if rev:
        pos = lambda bb, i: nblk - 1 - i
    else:
        pos = lambda bb, i: i
    col = lambda j: pl.BlockSpec((1, tb, HALF_W), lambda bb, i: (bb, pos(bb, i), j))
    consts = _mlstm_constants(rev, lc)
    names = ("tri", "sel_s", "sel_v", "ones_bd", "row_head", "diag", "causal")
    const_specs = [pl.BlockSpec(consts[n].shape, lambda bb, i, nd=consts[n].ndim: (0,) * nd)
                   for n in names]
    nq = C_HEADS // QUAD
    wq = QUAD * HEAD_DIM
    return pl.pallas_call(
        functools.partial(_mlstm_kernel, rev=rev, lc=lc),
        grid=(b, nblk),
        in_specs=[col(0), col(1), col(2),
                  pl.BlockSpec((1, tb, GATE_W), lambda bb, i: (bb, pos(bb, i), 0))] + const_specs,
        out_specs=col(0),
        out_shape=jax.ShapeDtypeStruct((b, s, HALF_W), F32),
        scratch_shapes=[pltpu.VMEM((nq, wq, wq), F32), pltpu.VMEM((nq, 1, wq), F32),
                        pltpu.VMEM((1, LANES), F32)],
        name="mlstm_bwd" if rev else "mlstm_fwd",
        compiler_params=_cparams("parallel", "arbitrary"),
    )(z, z, z, gates, *[consts[n] for n in names])


def _sgu_kernel(u_ref, v_ref, lng_ref, w_ref, b_ref, o_ref):
    tb = u_ref.shape[1]
    group = lax.broadcasted_iota(jnp.int32, (D_CHUNK, HALF_W), 1) // HEAD_DIM
    for ch in range(tb // D_CHUNK):
        rows = slice(ch * D_CHUNK, (ch + 1) * D_CHUNK)
        u = jax.nn.gelu(u_ref[0, rows, :].astype(F32))
        v = jax.nn.gelu(v_ref[0, rows, :].astype(F32))
        dv = v - jnp.mean(v, axis=-1, keepdims=True)
        vn = dv * lax.rsqrt(jnp.mean(dv * dv, axis=-1, keepdims=True) + EPS) * lng_ref[...]
        full = jnp.dot(w_ref[...], vn.astype(BF16), preferred_element_type=F32)
        sg = b_ref[...]
        for g in range(D_GROUPS):
            sg = sg + jnp.where(group == g, full[g * D_CHUNK:(g + 1) * D_CHUNK, :], 0.0)
        o_ref[0, rows, :] = (u * sg).astype(BF16)


def _sgu(z, lng, w, bexp, tb):
    b, s, _ = z.shape
    col = lambda j: pl.BlockSpec((1, tb, HALF_W), lambda bb, i: (bb, i, j))
    return pl.pallas_call(
        _sgu_kernel,
        grid=(b, s // tb),
        in_specs=[col(4), col(5),
                  pl.BlockSpec((1, HALF_W), lambda bb, i: (0, 0)),
                  pl.BlockSpec((D_GROUPS * D_CHUNK, D_CHUNK), lambda bb, i: (0, 0)),
                  pl.BlockSpec((D_CHUNK, HALF_W), lambda bb, i: (0, 0))],
        out_specs=col(0),
        out_shape=jax.ShapeDtypeStruct((b, s, HALF_W), BF16),
        name="sgu",
        compiler_params=_cparams("parallel", "parallel"),
    )(z, z, lng, w, bexp)


def _tiles(s):
    return dict(tm=min(512, s), tf=1024, tq=min(128, s), tk=min(512, s), tband=min(512, s),
                tb_mlstm=min(1024, s), lc=128, tb_sgu=min(512, s))


def _gqa_head_order():
    half = B_Q_HEADS // 2
    return [h for j in range(half) for h in (j, half + j)]


def _trunk(x, c, p):
    b, s, d = x.shape
    t = _tiles(s)
    mod = _adaln(c, p["w_ada"], p["b_ada"])
    ta, tq, tk = _rope_tables(s, p["qk_norm_g"])
    for layer in range(2):
        m = mod[:, layer * 6 * d:(layer + 1) * 6 * d].reshape(b, 6, 1, d)
        sh1, sc1, g1, sh2, sc2, g2 = (m[:, j] for j in range(6))
        ng = p["norm_g"][layer].reshape(4, 1, d)
        if layer == 0:
            qa, ka, va, qb, kb, vb = _inproj_even(x, ng[0], sc1, sh1, p["w_in_ab"], ta, tq, tk,
                                                  t["tm"])
            oa = _dilated_attention(qa, ka, va, t["tband"])
            ob = _gqa_attention(qb, kb, vb, t["tq"], t["tk"])
            x = _outproj_even(oa, ob, p["w_out_ab"], x, g1, ng[1], t["tm"])
        else:
            z, gates = _inproj_odd(x, ng[0], sc1, sh1, p["w_in_cd"], p["gate_bias"], t["tm"])
            hf = _mlstm(z, gates, False, t["tb_mlstm"], t["lc"])
            hb = _mlstm(z, gates, True, t["tb_mlstm"], t["lc"])
            dout = _sgu(z, p["sg_norm_g"], p["w_spatial"], p["b_spatial"], t["tb_sgu"])
            x = _outproj_odd(hf, hb, z, dout, p["mh_norm_g"], p["w_out_cd"], x, g1, ng[1], t["tm"])
        x = _ffn(x, ng[2], sc2, sh2, p["w_ff1"][layer], p["w_ff2"][layer], g2, ng[3],
                 t["tm"], t["tf"])
    return x


def _prepare_params(w_in_ab, w_out_ab, qk_norm_g, w_in_cd, w_out_cd, gate_bias, mh_norm_g,
                    sg_norm_g, w_spatial, b_spatial, w_ada, b_ada, norm_g, w_ff1, w_ff2):
    depth, d, _ = w_ada.shape
    wcd = w_in_cd[0]
    gate_lo = 4 * HALF_W
    gate_hi = gate_lo + N_GATES

    def gate_tiles(g):
        i_fw, f_fw, i_bw, f_bw = (g[:, j * C_HEADS:(j + 1) * C_HEADS] for j in range(4))
        pad = jnp.zeros((g.shape[0], LANES - 2 * C_HEADS), g.dtype)
        return jnp.concatenate([i_fw, i_bw, pad, f_fw, f_bw, pad], axis=1)

    wcd = jnp.concatenate([wcd[:, :gate_lo], wcd[:, gate_hi:],
                           gate_tiles(wcd[:, gate_lo:gate_hi])], axis=1)
    gb = gate_tiles(gate_bias[0].reshape(1, N_GATES).astype(F32))
    bexp = jnp.repeat(b_spatial[0].astype(F32).T, HEAD_DIM, axis=1)

    qb_lo = 3 * HALF_W
    order = np.asarray(_gqa_head_order())
    head_cols = (order[:, None] * HEAD_DIM + np.arange(HEAD_DIM)[None, :]).reshape(-1)
    wab = w_in_ab[0]
    wab = jnp.concatenate([wab[:, :qb_lo], wab[:, qb_lo + head_cols], wab[:, qb_lo + HALF_W:]],
                          axis=1)
    wout_ab = w_out_ab[0]
    wout_ab = jnp.concatenate([wout_ab[:HALF_W], wout_ab[HALF_W + head_cols]], axis=0)
    return dict(
        w_ada=jnp.concatenate([w_ada[l] for l in range(depth)], axis=1).astype(BF16),
        b_ada=b_ada.reshape(1, -1).astype(F32),
        w_in_ab=wab.astype(BF16), w_out_ab=wout_ab.astype(BF16),
        qk_norm_g=qk_norm_g[0],
        w_in_cd=wcd.astype(BF16), w_out_cd=w_out_cd[0].astype(BF16), gate_bias=gb,
        mh_norm_g=mh_norm_g[0].reshape(1, HALF_W).astype(F32),
        sg_norm_g=sg_norm_g[0].reshape(1, HALF_W).astype(F32),
        w_spatial=w_spatial[0].reshape(D_GROUPS * D_CHUNK, D_CHUNK).astype(BF16),
        b_spatial=bexp,
        norm_g=norm_g.astype(F32), w_ff1=w_ff1.astype(BF16), w_ff2=w_ff2.astype(BF16))


def kernel(x_prompt, x_sample, c_prompt, c_sample, w_in_ab, w_out_ab, qk_norm_g, w_in_cd, w_out_cd, gate_bias, mh_norm_g, sg_norm_g, w_spatial, b_spatial, w_ada, b_ada, norm_g, w_ff1, w_ff2):
    p = _prepare_params(w_in_ab, w_out_ab, qk_norm_g, w_in_cd, w_out_cd, gate_bias, mh_norm_g,
                        sg_norm_g, w_spatial, b_spatial, w_ada, b_ada, norm_g, w_ff1, w_ff2)
    return (_trunk(x_prompt, c_prompt, p), _trunk(x_sample, c_sample, p))
```

```python
import functools

import numpy as np
import jax
import jax.numpy as jnp
from jax import lax
from jax.experimental import pallas as pl
from jax.experimental.pallas import tpu as pltpu

F32 = jnp.float32
BF16 = jnp.bfloat16

D_MODEL = 1024
HEAD_DIM = 64
GRID_W = 64
ROPE_THETA = 10000.0
EPS = 1e-6
A_HEADS = 8
A_PATTERNS = ((128, 1), (512, 4), (2048, 16))
B_Q_HEADS = 8
B_KV_HEADS = 2
C_HEADS = 8
D_GROUPS = 8
D_CHUNK = 128
D_FF = 4 * D_MODEL
HALF_W = 512
AB_IN = 3 * HALF_W + HALF_W + 2 * B_KV_HEADS * HEAD_DIM
CD_MAIN = 6 * HALF_W
N_GATES = 4 * C_HEADS
GATE_W = 256
QUAD = 4
ONES_LANE_LOWER = HEAD_DIM
ONES_LANE_UPPER = 0

LANES = 128
MXU_N = 256
VMEM_LIMIT = 48 * 1024 * 1024

NEG = -1e30


def _cparams(*sem):
    return pltpu.CompilerParams(dimension_semantics=sem, vmem_limit_bytes=VMEM_LIMIT)


def _rms_rows(x):
    return x * lax.rsqrt(jnp.mean(x * x, axis=-1, keepdims=True) + EPS)


def _split_dot(a, b):
    hi = a.astype(BF16)
    r1 = a - hi.astype(F32)
    mid = r1.astype(BF16)
    lo = (r1 - mid.astype(F32)).astype(BF16)
    dot = functools.partial(jnp.dot, preferred_element_type=F32)
    return dot(hi, b) + dot(mid, b) + dot(lo, b)


def _split_dot_rhs(b, a):
    hi = a.astype(BF16)
    r1 = a - hi.astype(F32)
    mid = r1.astype(BF16)
    lo = (r1 - mid.astype(F32)).astype(BF16)
    dot = functools.partial(jnp.dot, preferred_element_type=F32)
    return dot(b, hi) + dot(b, mid) + dot(b, lo)


def _head_mean_square(z, ones_blk):
    z2 = z * z
    hi = z2.astype(BF16)
    lo = (z2 - hi.astype(F32)).astype(BF16)
    dot = functools.partial(jnp.dot, preferred_element_type=F32)
    return (dot(hi, ones_blk) + dot(lo, ones_blk)) * (1.0 / HEAD_DIM)


def _head_block_ones():
    r = lax.broadcasted_iota(jnp.int32, (LANES, LANES), 0) // HEAD_DIM
    c = lax.broadcasted_iota(jnp.int32, (LANES, LANES), 1) // HEAD_DIM
    return (r == c).astype(BF16)


def _log_sigmoid(x):
    return jnp.minimum(x, 0.0) - jnp.log1p(jnp.exp(-jnp.abs(x)))


def _dot_nt(a, b):
    return lax.dot_general(a, b, (((1,), (1,)), ((), ())), preferred_element_type=F32)


def _dot_tn(a, b):
    return lax.dot_general(a, b, (((0,), (0,)), ((), ())), preferred_element_type=F32)


def _adaln_kernel(c_ref, w_ref, b_ref, o_ref):
    c = c_ref[...]
    ca = c * jax.nn.sigmoid(c)
    o_ref[...] = jnp.dot(ca.astype(BF16), w_ref[...], preferred_element_type=F32) + b_ref[...]


def _adaln(c, w, b):
    nb, d = c.shape
    n = w.shape[1]
    tn = 2048
    return pl.pallas_call(
        _adaln_kernel,
        grid=(n // tn,),
        in_specs=[pl.BlockSpec((nb, d), lambda j: (0, 0)),
                  pl.BlockSpec((d, tn), lambda j: (0, j)),
                  pl.BlockSpec((1, tn), lambda j: (0, j))],
        out_specs=pl.BlockSpec((nb, tn), lambda j: (0, j)),
        out_shape=jax.ShapeDtypeStruct((nb, n), F32),
        name="adaln",
        compiler_params=_cparams("parallel"),
    )(c, w, b)


def _swap_halves(z, half, upper):
    return jnp.where(upper, pltpu.roll(z, LANES - half, axis=1), pltpu.roll(z, half, axis=1))


def _inproj_even_kernel(x_ref, g_ref, sc_ref, sh_ref, w_ref, ta_ref, tq_ref, tk_ref,
                        qa_ref, ka_ref, va_ref, qb_ref, kb_ref, vb_ref):
    x = x_ref[0]
    h = _rms_rows(x) * (g_ref[...] * (1.0 + sc_ref[0])) + sh_ref[0]
    hb = h.astype(BF16)
    tm = x.shape[0]
    lane = lax.broadcasted_iota(jnp.int32, (tm, LANES), 1)
    up64 = (lane % 64) < 32
    up32 = (lane % 32) < 16
    ones_blk = _head_block_ones()
    q_scale = HEAD_DIM ** -0.5

    def rotary_1d(z):
        return z * ta_ref[:, :LANES] + _swap_halves(z, 32, up64) * ta_ref[:, LANES:]

    def norm_axial(z, t_ref):
        y = z * t_ref[:, :LANES] + _swap_halves(z, 16, up32) * t_ref[:, LANES:]
        return y * lax.rsqrt(_head_mean_square(z, ones_blk) + EPS)

    lower = lane < HEAD_DIM

    def store_pair(ref, first, second, z, ones_column):
        fill_lo = (lane == ONES_LANE_LOWER).astype(F32) if ones_column else 0.0
        fill_up = (lane == ONES_LANE_UPPER).astype(F32) if ones_column else 0.0
        ref[0, first] = jnp.where(lower, z, fill_lo).astype(BF16)
        ref[0, second] = jnp.where(lower, fill_up, z).astype(BF16)

    for j in range(AB_IN // MXU_N):
        zz = jnp.dot(hb, w_ref[:, j * MXU_N:(j + 1) * MXU_N], preferred_element_type=F32)
        for s in range(MXU_N // LANES):
            c = (MXU_N // LANES) * j + s
            z = zz[:, s * LANES:(s + 1) * LANES]
            if c < 4:
                store_pair(qa_ref, 2 * c, 2 * c + 1, rotary_1d(z) * q_scale, False)
            elif c < 8:
                store_pair(ka_ref, 2 * (c - 4), 2 * (c - 4) + 1, rotary_1d(z), True)
            elif c < 12:
                store_pair(va_ref, 2 * (c - 8), 2 * (c - 8) + 1, z, True)
            elif c < 16:
                store_pair(qb_ref, c - 12, c - 12 + B_Q_HEADS // 2, norm_axial(z, tq_ref), False)
            elif c == 16:
                store_pair(kb_ref, 0, 1, norm_axial(z, tk_ref), True)
            else:
                store_pair(vb_ref, 0, 1, z, True)


def _inproj_even(x, g, sc, sh, w, ta, tq, tk, tm):
    b, s, d = x.shape
    n = w.shape[1]
    tab = pl.BlockSpec((tm, 2 * LANES), lambda i, bb: (i, 0))
    mod = pl.BlockSpec((1, 1, d), lambda i, bb: (bb, 0, 0))
    heads = (A_HEADS, A_HEADS, A_HEADS, B_Q_HEADS, B_KV_HEADS, B_KV_HEADS)
    return pl.pallas_call(
        _inproj_even_kernel,
        grid=(s // tm, b),
        in_specs=[pl.BlockSpec((1, tm, d), lambda i, bb: (bb, i, 0)),
                  pl.BlockSpec((1, d), lambda i, bb: (0, 0)),
                  mod, mod,
                  pl.BlockSpec((d, n), lambda i, bb: (0, 0)),
                  tab, tab, tab],
        out_specs=[pl.BlockSpec((1, nh, tm, LANES), lambda i, bb: (bb, 0, i, 0)) for nh in heads],
        out_shape=[jax.ShapeDtypeStruct((b, nh, s, LANES), BF16) for nh in heads],
        name="inproj_even",
        compiler_params=_cparams("parallel", "parallel"),
    )(x, g, sc, sh, w, ta, tq, tk)


def _rope_tables(s, qk_g):
    pos = jnp.arange(s)

    def angles(p, dim):
        inv_freq = ROPE_THETA ** (-jnp.arange(0, dim, 2, dtype=F32) / dim)
        return p.astype(F32)[:, None] * inv_freq[None, :]

    a1 = angles(pos, HEAD_DIM)
    cos1, sin1 = jnp.cos(a1), jnp.sin(a1)
    direct = jnp.concatenate([cos1, cos1], axis=-1)
    swapped = jnp.concatenate([-sin1, sin1], axis=-1)
    ta = jnp.concatenate([direct, direct, swapped, swapped], axis=-1)

    ar = angles(pos // GRID_W, HEAD_DIM // 2)
    ac = angles(pos % GRID_W, HEAD_DIM // 2)
    cosb = jnp.concatenate([jnp.cos(ar), jnp.cos(ar), jnp.cos(ac), jnp.cos(ac)], axis=-1)
    sinb = jnp.concatenate([-jnp.sin(ar), jnp.sin(ar), -jnp.sin(ac), jnp.sin(ac)], axis=-1)

    def gained(gain, scale):
        gain = gain.astype(F32)
        gswap = gain.reshape(2, 2, HEAD_DIM // 4)[:, ::-1].reshape(HEAD_DIM)
        direct = cosb * gain * scale
        swapped = sinb * gswap * scale
        return jnp.concatenate([direct, direct, swapped, swapped], axis=-1)

    return ta, gained(qk_g[0], HEAD_DIM ** -0.5), gained(qk_g[1], 1.0)


LOG2E = 1.4426950408889634
LN2 = 0.6931471805599453
BOUND_SLACK = 1.02
MIN_DENOM = 2.0 ** -60


def _sq_row_sums(x):
    return jnp.dot(x * x, jnp.ones((LANES, LANES), BF16), preferred_element_type=F32)


def _max_key_norm(k2d, tk):
    def body(j, mx):
        rows = pl.ds(pl.multiple_of(j * tk, tk), tk)
        return jnp.maximum(mx, jnp.max(_sq_row_sums(k2d[rows, :]), axis=0, keepdims=True))

    return jnp.sqrt(lax.fori_loop(0, k2d.shape[0] // tk, body, jnp.zeros((1, LANES), F32)))


def _ones_lane(upper):
    return ONES_LANE_UPPER if upper else ONES_LANE_LOWER


def _bounded_queries(q, kmax, extra, upper):
    bound = BOUND_SLACK * jnp.sqrt(_sq_row_sums(q)) * kmax + extra
    lane = lax.broadcasted_iota(jnp.int32, q.shape, 1)
    return jnp.where(lane == _ones_lane(upper), -LOG2E * bound, LOG2E * q.astype(F32)).astype(BF16)


def _online_step(q, k, v, bias, m_ref, acc_ref):
    s = _dot_nt(q, k)
    if bias is not None:
        s = s + bias
    m_prev = m_ref[...]
    m_new = jnp.maximum(m_prev, jnp.max(s, axis=-1, keepdims=True))
    p = jnp.exp(s - m_new).astype(BF16)
    acc_ref[...] = jnp.exp(m_prev - m_new) * acc_ref[...] + jnp.dot(p, v, preferred_element_type=F32)
    m_ref[...] = m_new


def _online_init(m_ref, acc_ref):
    m_ref[...] = jnp.full(m_ref.shape, -jnp.inf, F32)
    acc_ref[...] = jnp.zeros(acc_ref.shape, F32)


def _denominators(acc, upper):
    lane = _ones_lane(upper)
    return acc[:, lane:lane + 1]


def _normalised_pair(acc_lower, acc_upper):
    lane = lax.broadcasted_iota(jnp.int32, acc_lower.shape, 1)
    return jnp.where(lane < HEAD_DIM, acc_lower / _denominators(acc_lower, False),
                     acc_upper / _denominators(acc_upper, True))


def _gqa_kernel(q_ref, k_ref, v_ref, o_ref, qa_ref, acc_ref, m_ref, kmax_ref, *, tk, rsub):
    nh, tq, dp = q_ref.shape[1:]
    gsz = nh // 2
    grows = gsz * tq
    nkv = k_ref.shape[2] // tk

    def group_q(g):
        return q_ref[0, g * gsz:(g + 1) * gsz].reshape(grows, dp)

    @pl.when(pl.program_id(1) == 0)
    def _():
        for g in range(2):
            kmax_ref[g] = _max_key_norm(k_ref.at[0, g], tk)

    for g in range(2):
        qa_ref[g * grows:(g + 1) * grows, :] = _bounded_queries(group_q(g), kmax_ref[g], 0.0, g == 1)
    acc_ref[...] = jnp.zeros(acc_ref.shape, F32)

    def bounded(j, carry):
        keys = pl.ds(pl.multiple_of(j * tk, tk), tk)
        subtiles = [(g, pl.ds(g * grows + r * rsub, rsub))
                    for g in range(2) for r in range(grows // rsub)]
        scores = [_dot_nt(qa_ref[rr, :], k_ref[0, g, keys, :]) for g, rr in subtiles]
        weights = [jnp.exp2(s).astype(BF16) for s in scores]
        for (g, rr), p in zip(subtiles, weights):
            acc_ref[rr, :] += jnp.dot(p, v_ref[0, g, keys, :], preferred_element_type=F32)
        return carry

    lax.fori_loop(0, nkv, bounded, 0, unroll=2 if nkv % 2 == 0 else 1)

    smallest =jnp.minimum(jnp.min(_denominators(acc_ref[:grows, :], False)),
                           jnp.min(_denominators(acc_ref[grows:, :], True)))

    @pl.when(smallest < MIN_DENOM)
    def _():
        _online_init(m_ref, acc_ref)

        def online(j, carry):
            keys = pl.ds(pl.multiple_of(j * tk, tk), tk)
            for g in range(2):
                rr = pl.ds(g * grows, grows)
                _online_step(group_q(g), k_ref[0, g, keys, :], v_ref[0, g, keys, :], None,
                             m_ref.at[rr, :], acc_ref.at[rr, :])
            return carry

        lax.fori_loop(0, nkv, online, 0)

    for j in range(gsz):
        lo = acc_ref[j * tq:(j + 1) * tq, :]
        up = acc_ref[grows + j * tq:grows + (j + 1) * tq, :]
        o_ref[0, :, j * LANES:(j + 1) * LANES] = _normalised_pair(lo, up).astype(o_ref.dtype)


def _gqa_attention(q, k, v, tq, tk):
    b, hq, s, dp = q.shape
    assert k.shape[1] == 2 and hq % 2 == 0
    rows = hq * tq
    kv = pl.BlockSpec((1, 2, s, dp), lambda bb, i: (bb, 0, 0, 0))
    return pl.pallas_call(
        functools.partial(_gqa_kernel, tk=tk, rsub=min(256, rows // 2)),
        grid=(b, s // tq),
        in_specs=[pl.BlockSpec((1, hq, tq, dp), lambda bb, i: (bb, 0, i, 0)), kv, kv],
        out_specs=pl.BlockSpec((1, tq, hq * HEAD_DIM), lambda bb, i: (bb, i, 0)),
        out_shape=jax.ShapeDtypeStruct((b, s, hq * HEAD_DIM), BF16),
        scratch_shapes=[pltpu.VMEM((rows, dp), BF16), pltpu.VMEM((rows, dp), F32),
                        pltpu.VMEM((rows, 1), F32), pltpu.VMEM((2, 1, dp), F32)],
        name="gqa_attention",
        compiler_params=_cparams("parallel", "arbitrary"),
    )(q, k, v)


def _band_reach(t):
    return -(-max(w // 2 for w, _ in A_PATTERNS) // t)


def _band_bias(t, nproc):
    offs = range(-(nproc - 1), nproc)
    a = np.arange(t)[:, None]
    bcol = np.arange(t)[None, :]
    out = np.zeros((len(offs), t, t), np.float64)
    for n, d in enumerate(offs):
        j = d * t + bcol - a
        mult = np.zeros((t, t), np.float64)
        for window, dil in A_PATTERNS:
            mult += ((j % dil) == 0) & (np.abs(j) <= window // 2)
        out[n] = np.where(mult > 0, LOG2E * np.log(np.maximum(mult, 1.0)), NEG)
    return jnp.asarray(out, F32)


def _band_kernel(q_ref, k_ref, v_ref, bias_ref, o_ref, qa_ref, acc_ref, m_ref, kmax_ref, *,
                 nproc, rsub):
    t, dp = q_ref.shape[2:]
    nblk = k_ref.shape[2] // t
    i = pl.program_id(2)
    first = jnp.clip(i - _band_reach(t), 0, nblk - nproc)
    max_bias = float(np.log(len(A_PATTERNS)))

    @pl.when(i == 0)
    def _():
        for h in range(2):
            kmax_ref[h] = _max_key_norm(k_ref.at[0, h], t)

    for h in range(2):
        qa_ref[h] = _bounded_queries(q_ref[0, h], kmax_ref[h], max_bias, h == 1)
    acc_ref[...] = jnp.zeros(acc_ref.shape, F32)
    subtiles = [(h, pl.ds(r * rsub, rsub)) for h in range(2) for r in range(t // rsub)]

    def key_block(n):
        kb = first + n
        return pl.ds(pl.multiple_of(kb * t, t), t), bias_ref.at[kb - i + nproc - 1]

    def bounded(n, carry):
        keys, bias = key_block(n)
        scores = [_dot_nt(qa_ref[h, rr, :], k_ref[0, h, keys, :]) + bias[rr, :]
                  for h, rr in subtiles]
        weights = [jnp.exp2(s).astype(BF16) for s in scores]
        for (h, rr), p in zip(subtiles, weights):
            acc_ref[h, rr, :] += jnp.dot(p, v_ref[0, h, keys, :], preferred_element_type=F32)
        return carry

    lax.fori_loop(0, nproc, bounded, 0, unroll=True)

    smallest = jnp.minimum(jnp.min(_denominators(acc_ref[0], False)),
                           jnp.min(_denominators(acc_ref[1], True)))

    @pl.when(smallest < MIN_DENOM)
    def _():
        for h in range(2):
            acc = acc_ref.at[h]
            _online_init(m_ref, acc)

            def online(n, carry):
                keys, bias = key_block(n)
                _online_step(q_ref[0, h], k_ref[0, h, keys, :], v_ref[0, h, keys, :],
                             LN2 * bias[...], m_ref, acc)
                return carry

            lax.fori_loop(0, nproc, online, 0)

    o_ref[0] = _normalised_pair(acc_ref[0], acc_ref[1]).astype(o_ref.dtype)


def _dilated_attention(q, k, v, t):
    b, h, s, dp = q.shape
    nproc = min(2 * _band_reach(t) + 1, s // t)
    bias = _band_bias(t, nproc)
    kv = pl.BlockSpec((1, 2, s, dp), lambda bb, hh, i: (bb, hh, 0, 0))
    return pl.pallas_call(
        functools.partial(_band_kernel, nproc=nproc, rsub=min(256, t)),
        grid=(b, h // 2, s // t),
        in_specs=[pl.BlockSpec((1, 2, t, dp), lambda bb, hh, i: (bb, hh, i, 0)), kv, kv,
                  pl.BlockSpec(bias.shape, lambda bb, hh, i: (0, 0, 0),
                               pipeline_mode=pl.Buffered(1))],
        out_specs=pl.BlockSpec((1, t, LANES), lambda bb, hh, i: (bb, i, hh)),
        out_shape=jax.ShapeDtypeStruct((b, s, h * HEAD_DIM), BF16),
        scratch_shapes=[pltpu.VMEM((2, t, dp), BF16), pltpu.VMEM((2, t, dp), F32),
                        pltpu.VMEM((t, 1), F32), pltpu.VMEM((2, 1, dp), F32)],
        name="dilated_attention",
        compiler_params=_cparams("parallel", "parallel", "arbitrary"),
    )(q, k, v, bias)


def _residual_update(x, y, gate, norm_g):
    return x + gate * (_rms_rows(y) * norm_g)


def _mixer_output_even(oa_ref, ob_ref):
    return oa_ref[0], ob_ref[0]


def _mixer_output_odd(hf_ref, hb_ref, og_ref, d_ref, mhg_ref):
    ones_blk = _head_block_ones()
    parts = []
    for c in range(HALF_W // LANES):
        cols = slice(c * LANES, (c + 1) * LANES)
        hs = hf_ref[0, :, cols] + hb_ref[0, :, cols]
        hn = hs * lax.rsqrt(_head_mean_square(hs, ones_blk) + EPS) * mhg_ref[:, cols]
        parts.append((hn * jax.nn.sigmoid(og_ref[0, :, cols].astype(F32))).astype(BF16))
    return jnp.concatenate(parts, axis=-1), d_ref[0]


def _layer_tail_kernel(*refs, odd, tf):
    n_mix = 5 if odd else 2
    mix = (_mixer_output_odd if odd else _mixer_output_even)(*refs[:n_mix])
    (wo_ref, x_ref, g1_ref, ng1_ref, ng2_ref, sc_ref, sh_ref, w1_ref, w2_ref, g2_ref, ng3_ref,
     o_ref, a_ref) = refs[n_mix:]
    y = jnp.dot(mix[0], wo_ref[:HALF_W, :], preferred_element_type=F32)
    y = y + jnp.dot(mix[1], wo_ref[HALF_W:, :], preferred_element_type=F32)
    x1 = _residual_update(x_ref[0], y, g1_ref[0], ng1_ref[...])
    h = (_rms_rows(x1) * (ng2_ref[...] * (1.0 + sc_ref[0])) + sh_ref[0]).astype(BF16)
    for kc in range(w1_ref.shape[1] // tf):
        cols = slice(kc * tf, (kc + 1) * tf)
        a = jnp.maximum(jnp.dot(h, w1_ref[:, cols], preferred_element_type=F32), 0.0)
        a_ref[:, cols] = (a * a).astype(BF16)
    y2 = jnp.dot(a_ref[...], w2_ref[...], preferred_element_type=F32)
    o_ref[0] = _residual_update(x1, y2, g2_ref[0], ng3_ref[...])


def _layer_tail(mix_inputs, mix_specs, odd, wo, x, g1, ng1, ng2, sc, sh, w1, w2, g2, ng3, tm, tf):
    b, s, d = x.shape
    f = w1.shape[1]
    full = pl.BlockSpec((1, tm, d), lambda bb, i: (bb, i, 0))
    mod = pl.BlockSpec((1, 1, d), lambda bb, i: (bb, 0, 0))
    vec = pl.BlockSpec((1, d), lambda bb, i: (0, 0))

    def resident(shape):
        return pl.BlockSpec(shape, lambda bb, i: (0, 0), pipeline_mode=pl.Buffered(1))

    return pl.pallas_call(
        functools.partial(_layer_tail_kernel, odd=odd, tf=tf),
        grid=(b, s // tm),
        in_specs=mix_specs + [resident((d, d)), full, mod, vec, vec, mod, mod,
                              resident((d, f)), resident((f, d)), mod, vec],
        out_specs=full,
        out_shape=jax.ShapeDtypeStruct((b, s, d), F32),
        scratch_shapes=[pltpu.VMEM((tm, f), BF16)],
        name="layer_tail_odd" if odd else "layer_tail_even",
        compiler_params=_cparams("parallel", "parallel"),
    )(*mix_inputs, wo, x, g1, ng1, ng2, sc, sh, w1, w2, g2, ng3)


def _layer_tail_even(oa, ob, *rest, tm, tf):
    half = pl.BlockSpec((1, tm, HALF_W), lambda bb, i: (bb, i, 0))
    return _layer_tail([oa, ob], [half, half], False, *rest, tm, tf)


def _layer_tail_odd(hf, hb, z, dout, mhg, *rest, tm, tf):
    half = pl.BlockSpec((1, tm, HALF_W), lambda bb, i: (bb, i, 0))
    og = pl.BlockSpec((1, tm, HALF_W), lambda bb, i: (bb, i, 3))
    return _layer_tail([hf, hb, z, dout, mhg],
                       [half, half, og, half, pl.BlockSpec((1, HALF_W), lambda bb, i: (0, 0))],
                       True, *rest, tm, tf)


def _inproj_odd_kernel(x_ref, g_ref, sc_ref, sh_ref, w_ref, gb_ref, z_ref, gates_ref):
    x = x_ref[0]
    h = _rms_rows(x) * (g_ref[...] * (1.0 + sc_ref[0])) + sh_ref[0]
    hb = h.astype(BF16)
    k_scale = HEAD_DIM ** -0.5
    for j in range(CD_MAIN // MXU_N):
        cols = slice(j * MXU_N, (j + 1) * MXU_N)
        zz = jnp.dot(hb, w_ref[:, cols], preferred_element_type=F32)
        if HALF_W <= j * MXU_N < 2 * HALF_W:
            zz = zz * k_scale
        z_ref[0, :, cols] = zz.astype(BF16)
    gates_ref[0] = jnp.dot(hb, w_ref[:, CD_MAIN:], preferred_element_type=F32) + gb_ref[...]


def _inproj_odd(x, g, sc, sh, w, gb, tm):
    b, s, d = x.shape
    n = w.shape[1]
    mod = pl.BlockSpec((1, 1, d), lambda i, bb: (bb, 0, 0))
    return pl.pallas_call(
        _inproj_odd_kernel,
        grid=(s // tm, b),
        in_specs=[pl.BlockSpec((1, tm, d), lambda i, bb: (bb, i, 0)),
                  pl.BlockSpec((1, d), lambda i, bb: (0, 0)),
                  mod, mod,
                  pl.BlockSpec((d, n), lambda i, bb: (0, 0)),
                  pl.BlockSpec((1, GATE_W), lambda i, bb: (0, 0))],
        out_specs=[pl.BlockSpec((1, tm, CD_MAIN), lambda i, bb: (bb, i, 0)),
                   pl.BlockSpec((1, tm, GATE_W), lambda i, bb: (bb, i, 0))],
        out_shape=[jax.ShapeDtypeStruct((b, s, CD_MAIN), BF16),
                   jax.ShapeDtypeStruct((b, s, GATE_W), F32)],
        name="inproj_odd",
        compiler_params=_cparams("parallel", "parallel"),
    )(x, g, sc, sh, w, gb)


def _gate_lane(rev, head):
    return (C_HEADS if rev else 0) + head


def _mlstm_constants(rev, lc):
    wq = QUAD * HEAD_DIM
    t = np.arange(lc)
    sees = (t[None, :] >= t[:, None]) if rev else (t[None, :] <= t[:, None])
    nq = C_HEADS // QUAD
    sel_s = np.zeros((nq, LANES, QUAD * lc), np.float32)
    sel_v = np.zeros((nq, LANES, wq), np.float32)
    for qd in range(nq):
        for hh in range(QUAD):
            lane = _gate_lane(rev, QUAD * qd + hh)
            sel_s[qd, lane, hh * lc:(hh + 1) * lc] = 1.0
            sel_v[qd, lane, hh * HEAD_DIM:(hh + 1) * HEAD_DIM] = 1.0
    head_of_row = np.arange(QUAD * lc) // lc
    head_of_col = np.arange(wq) // HEAD_DIM
    return dict(
        tri=jnp.asarray(sees, BF16),
        sel_s=jnp.asarray(sel_s, BF16), sel_v=jnp.asarray(sel_v, BF16),
        ones_bd=jnp.asarray(np.transpose(sel_s, (0, 2, 1)), BF16),
        row_head=jnp.asarray(head_of_row[:, None] == head_of_col[None, :], BF16),
        diag=jnp.asarray(head_of_col[:, None] == head_of_col[None, :], F32),
        causal=jnp.asarray(np.tile(sees, (1, QUAD)), F32))


def _scan_max(x, rev):
    n = x.shape[0]
    row = lax.broadcasted_iota(jnp.int32, x.shape, 0)
    sh = 1
    while sh < n:
        if rev:
            x = jnp.maximum(x, jnp.where(row < n - sh, pltpu.roll(x, n - sh, axis=0), -jnp.inf))
        else:
            x = jnp.maximum(x, jnp.where(row >= sh, pltpu.roll(x, sh, axis=0), -jnp.inf))
        sh *= 2
    return x


def _mlstm_kernel(*refs, lc):
    nd = len(_MLSTM_DIR_CONSTS)
    blocks = (refs[0:4], refs[4:8])
    rowhead_ref, diag_ref = refs[8:10]
    consts = (refs[10:10 + nd], refs[10 + nd:10 + 2 * nd])
    outs = refs[10 + 2 * nd:12 + 2 * nd]
    c_sc, n_sc, m_sc = refs[12 + 2 * nd:]

    @pl.when(pl.program_id(1) == 0)
    def _():
        c_sc[...] = jnp.zeros(c_sc.shape, F32)
        n_sc[...] = jnp.zeros(n_sc.shape, F32)
        m_sc[...] = jnp.zeros(m_sc.shape, F32)

    nch = outs[0].shape[1] // lc

    wq = QUAD * HEAD_DIM
    edges = (lc - 1, 0)
    jobs = [(rev, qd) for rev in (0, 1) for qd in range(C_HEADS // QUAD)]
    dot = functools.partial(jnp.dot, preferred_element_type=F32)

    def chunk(ci, carry):
        rows = (pl.ds(pl.multiple_of(ci * lc, lc), lc),
                pl.ds(pl.multiple_of((nch - 1 - ci) * lc, lc), lc))
        gate = []
        for rev in (0, 1):
            g_ref = blocks[rev][3]
            tri_ref = consts[rev][0]
            gi = g_ref[0, rows[rev], :LANES]
            bcum = _split_dot_rhs(tri_ref[...], _log_sigmoid(g_ref[0, rows[rev], LANES:]))
            a = gi - bcum
            m_prev = m_sc[rev]
            mt_b = jnp.maximum(m_prev, _scan_max(a, bool(rev))).astype(BF16)
            mt = mt_b.astype(F32)
            w_b = jnp.exp(m_prev - mt).astype(BF16)
            b_all = bcum[edges[rev]:edges[rev] + 1, :]
            log_w = b_all + a
            m_new = jnp.maximum(b_all + m_prev, jnp.max(log_w, axis=0, keepdims=True))
            gate.append(dict(
                mt_b=mt_b, w_b=w_b, w=w_b.astype(F32),
                floor=jnp.exp(-(bcum + mt)),
                wk_b=jnp.exp(log_w - m_new).astype(BF16),
                decay=jnp.broadcast_to(jnp.exp(b_all + m_prev - m_new), (8, LANES)),
                a_t=a.T))
            m_sc[rev] = m_new

        row_head = rowhead_ref[...]

        def operand(rev, qd, j):
            return blocks[rev][j][0, rows[rev], qd * wq:(qd + 1) * wq]

        def sel_v(rev, qd):
            return consts[rev][2][qd]

        scores, log_e = [], []
        for rev, qd in jobs:
            kbd = jnp.concatenate([operand(rev, qd, 1)] * QUAD, axis=0) * row_head
            scores.append(_dot_nt(operand(rev, qd, 0), kbd))
            a_rows = jnp.concatenate(
                [jnp.broadcast_to(gate[rev]["a_t"][ln:ln + 1, :], (lc, lc))
                 for ln in (_gate_lane(bool(rev), QUAD * qd + hh) for hh in range(QUAD))], axis=1)
            log_e.append(jnp.where(consts[rev][4][...] > 0.0,
                                   a_rows - dot(gate[rev]["mt_b"], consts[rev][1][qd]), NEG))
        probs = [(s * jnp.exp(e)).astype(BF16) for s, e in zip(scores, log_e)]

        pvs, qcs = [], []
        for (rev, qd), p in zip(jobs, probs):
            v = operand(rev, qd, 2)
            vaug = jnp.concatenate(
                [jnp.concatenate([v] * QUAD, axis=0) * row_head, consts[rev][3][qd]], axis=1)
            pvs.append(dot(p, vaug))
            caug = jnp.concatenate(
                [c_sc[rev, qd].astype(BF16),
                 (sel_v(rev, qd).astype(F32) * n_sc[rev, qd]).astype(BF16)], axis=0)
            qcs.append(_dot_nt(operand(rev, qd, 0), caug))

        for (rev, qd), pv, qc in zip(jobs, pvs, qcs):
            g = gate[rev]
            den = pv[:, wq:] + g["w"] * qc[:, wq:]
            scale_b = (1.0 / jnp.maximum(jnp.abs(den), g["floor"])).astype(BF16)
            outs[rev][0, rows[rev], qd * wq:(qd + 1) * wq] = (
                dot(scale_b, sel_v(rev, qd)) * (pv[:, :wq] + dot(g["w_b"], sel_v(rev, qd)) * qc[:, :wq]))

        for rev, qd in jobs:
            g = gate[rev]
            k = operand(rev, qd, 1)
            wk_v = dot(g["wk_b"], sel_v(rev, qd))
            dec_v = _split_dot(g["decay"], sel_v(rev, qd))[:1]
            vw = (wk_v * operand(rev, qd, 2).astype(F32)).astype(BF16)
            c_sc[rev, qd] = dec_v * c_sc[rev, qd] + diag_ref[...] * _dot_tn(vw, k)
            n_sc[rev, qd] = (dec_v * n_sc[rev, qd]
                             + jnp.sum(wk_v * k.astype(F32), axis=0, keepdims=True))
        return carry

    lax.fori_loop(0, nch, chunk, 0, unroll=2 if nch % 2 == 0 else 1)


_MLSTM_DIR_CONSTS = ("tri", "sel_s", "sel_v", "ones_bd", "causal")


def _mlstm(z, gates, tb, lc):
    b, s, _ = z.shape
    nblk = s // tb
    position = (lambda i: i, lambda i: nblk - 1 - i)

    def col(rev, j):
        return pl.BlockSpec((1, tb, HALF_W), lambda bb, i: (bb, position[rev](i), j))

    def block_specs(rev):
        return [col(rev, 0), col(rev, 1), col(rev, 2),
                pl.BlockSpec((1, tb, GATE_W), lambda bb, i: (bb, position[rev](i), 0))]

    def const_spec(a):
        return pl.BlockSpec(a.shape, lambda bb, i: (0,) * a.ndim)

    consts = (_mlstm_constants(False, lc), _mlstm_constants(True, lc))
    shared = [consts[0]["row_head"], consts[0]["diag"]]
    per_dir = [consts[rev][n] for rev in (0, 1) for n in _MLSTM_DIR_CONSTS]
    nq = C_HEADS // QUAD
    wq = QUAD * HEAD_DIM
    return pl.pallas_call(
        functools.partial(_mlstm_kernel, lc=lc),
        grid=(b, nblk),
        in_specs=block_specs(0) + block_specs(1) + [const_spec(a) for a in shared + per_dir],
        out_specs=[col(0, 0), col(1, 0)],
        out_shape=[jax.ShapeDtypeStruct((b, s, HALF_W), F32)] * 2,
        scratch_shapes=[pltpu.VMEM((2, nq, wq, wq), F32), pltpu.VMEM((2, nq, 1, wq), F32),
                        pltpu.VMEM((2, 1, LANES), F32)],
        name="mlstm",
        compiler_params=_cparams("parallel", "arbitrary"),
    )(z, z, z, gates, z, z, z, gates, *shared, *per_dir)


def _sgu_kernel(u_ref, v_ref, lng_ref, w_ref, b_ref, o_ref):
    tb = u_ref.shape[1]
    group = lax.broadcasted_iota(jnp.int32, (D_CHUNK, HALF_W), 1) // HEAD_DIM
    for ch in range(tb // D_CHUNK):
        rows = slice(ch * D_CHUNK, (ch + 1) * D_CHUNK)
        u = jax.nn.gelu(u_ref[0, rows, :].astype(F32))
        v = jax.nn.gelu(v_ref[0, rows, :].astype(F32))
        dv = v - jnp.mean(v, axis=-1, keepdims=True)
        vn = dv * lax.rsqrt(jnp.mean(dv * dv, axis=-1, keepdims=True) + EPS) * lng_ref[...]
        full = jnp.dot(w_ref[...], vn.astype(BF16), preferred_element_type=F32)
        sg = b_ref[...]
        for g in range(D_GROUPS):
            sg = sg + jnp.where(group == g, full[g * D_CHUNK:(g + 1) * D_CHUNK, :], 0.0)
        o_ref[0, rows, :] = (u * sg).astype(BF16)


def _sgu(z, lng, w, bexp, tb):
    b, s, _ = z.shape
    col = lambda j: pl.BlockSpec((1, tb, HALF_W), lambda bb, i: (bb, i, j))
    return pl.pallas_call(
        _sgu_kernel,
        grid=(b, s // tb),
        in_specs=[col(4), col(5),
                  pl.BlockSpec((1, HALF_W), lambda bb, i: (0, 0)),
                  pl.BlockSpec((D_GROUPS * D_CHUNK, D_CHUNK), lambda bb, i: (0, 0)),
                  pl.BlockSpec((D_CHUNK, HALF_W), lambda bb, i: (0, 0))],
        out_specs=col(0),
        out_shape=jax.ShapeDtypeStruct((b, s, HALF_W), BF16),
        name="sgu",
        compiler_params=_cparams("parallel", "parallel"),
    )(z, z, lng, w, bexp)


def _tiles(s):
    return dict(tm=min(512, s), tf=1024, tq=min(128, s), tk=min(512, s), tband=min(512, s),
                tb_mlstm=min(1024, s), lc=128, tb_sgu=min(512, s))


def _gqa_head_order():
    half = B_Q_HEADS // 2
    return [h for j in range(half) for h in (j, half + j)]


def _trunk(x, c, p):
    b, s, d = x.shape
    t = _tiles(s)
    mod = _adaln(c, p["w_ada"], p["b_ada"])
    ta, tq, tk = _rope_tables(s, p["qk_norm_g"])
    for layer in range(2):
        m = mod[:, layer * 6 * d:(layer + 1) * 6 * d].reshape(b, 6, 1, d)
        sh1, sc1, g1, sh2, sc2, g2 = (m[:, j] for j in range(6))
        ng = p["norm_g"][layer].reshape(4, 1, d)
        tail = (x, g1, ng[1], ng[2], sc2, sh2, p["w_ff1"][layer], p["w_ff2"][layer], g2, ng[3])
        if layer == 0:
            qa, ka, va, qb, kb, vb = _inproj_even(x, ng[0], sc1, sh1, p["w_in_ab"], ta, tq, tk,
                                                  t["tm"])
            oa = _dilated_attention(qa, ka, va, t["tband"])
            ob = _gqa_attention(qb, kb, vb, t["tq"], t["tk"])
            x = _layer_tail_even(oa, ob, p["w_out_ab"], *tail, tm=t["tm"], tf=t["tf"])
        else:
            z, gates = _inproj_odd(x, ng[0], sc1, sh1, p["w_in_cd"], p["gate_bias"], t["tm"])
            hf, hb = _mlstm(z, gates, t["tb_mlstm"], t["lc"])
            dout = _sgu(z, p["sg_norm_g"], p["w_spatial"], p["b_spatial"], t["tb_sgu"])
            x = _layer_tail_odd(hf, hb, z, dout, p["mh_norm_g"], p["w_out_cd"], *tail,
                                tm=t["tm"], tf=t["tf"])
    return x


def _prepare_params(w_in_ab, w_out_ab, qk_norm_g, w_in_cd, w_out_cd, gate_bias, mh_norm_g,
                    sg_norm_g, w_spatial, b_spatial, w_ada, b_ada, norm_g, w_ff1, w_ff2):
    depth, d, _ = w_ada.shape
    wcd = w_in_cd[0]
    gate_lo = 4 * HALF_W
    gate_hi = gate_lo + N_GATES

    def gate_tiles(g):
        i_fw, f_fw, i_bw, f_bw = (g[:, j * C_HEADS:(j + 1) * C_HEADS] for j in range(4))
        pad = jnp.zeros((g.shape[0], LANES - 2 * C_HEADS), g.dtype)
        return jnp.concatenate([i_fw, i_bw, pad, f_fw, f_bw, pad], axis=1)

    wcd = jnp.concatenate([wcd[:, :gate_lo], wcd[:, gate_hi:],
                           gate_tiles(wcd[:, gate_lo:gate_hi])], axis=1)
    gb = gate_tiles(gate_bias[0].reshape(1, N_GATES).astype(F32))
    bexp = jnp.repeat(b_spatial[0].astype(F32).T, HEAD_DIM, axis=1)

    qb_lo = 3 * HALF_W
    order = np.asarray(_gqa_head_order())
    head_cols = (order[:, None] * HEAD_DIM + np.arange(HEAD_DIM)[None, :]).reshape(-1)
    wab = w_in_ab[0]
    wab = jnp.concatenate([wab[:, :qb_lo], wab[:, qb_lo + head_cols], wab[:, qb_lo + HALF_W:]],
                          axis=1)
    wout_ab = w_out_ab[0]
    wout_ab = jnp.concatenate([wout_ab[:HALF_W], wout_ab[HALF_W + head_cols]], axis=0)
    return dict(
        w_ada=jnp.concatenate([w_ada[l] for l in range(depth)], axis=1).astype(BF16),
        b_ada=b_ada.reshape(1, -1).astype(F32),
        w_in_ab=wab.astype(BF16), w_out_ab=wout_ab.astype(BF16),
        qk_norm_g=qk_norm_g[0],
        w_in_cd=wcd.astype(BF16), w_out_cd=w_out_cd[0].astype(BF16), gate_bias=gb,
        mh_norm_g=mh_norm_g[0].reshape(1, HALF_W).astype(F32),
        sg_norm_g=sg_norm_g[0].reshape(1, HALF_W).astype(F32),
        w_spatial=w_spatial[0].reshape(D_GROUPS * D_CHUNK, D_CHUNK).astype(BF16),
        b_spatial=bexp,
        norm_g=norm_g.astype(F32), w_ff1=w_ff1.astype(BF16), w_ff2=w_ff2.astype(BF16))


def kernel(x_prompt, x_sample, c_prompt, c_sample, w_in_ab, w_out_ab, qk_norm_g, w_in_cd, w_out_cd, gate_bias, mh_norm_g, sg_norm_g, w_spatial, b_spatial, w_ada, b_ada, norm_g, w_ff1, w_ff2):
    p = _prepare_params(w_in_ab, w_out_ab, qk_norm_g, w_in_cd, w_out_cd, gate_bias, mh_norm_g,
                        sg_norm_g, w_spatial, b_spatial, w_ada, b_ada, norm_g, w_ff1, w_ff2)
    return (_trunk(x_prompt, c_prompt, p), _trunk(x_sample, c_sample, p))
```

```python
import functools

import numpy as np
import jax
import jax.numpy as jnp
from jax import lax
from jax.experimental import pallas as pl
from jax.experimental.pallas import tpu as pltpu

F32 = jnp.float32
BF16 = jnp.bfloat16

D_MODEL = 1024
HEAD_DIM = 64
GRID_W = 64
ROPE_THETA = 10000.0
EPS = 1e-6
A_HEADS = 8
A_PATTERNS = ((128, 1), (512, 4), (2048, 16))
B_Q_HEADS = 8
B_KV_HEADS = 2
C_HEADS = 8
D_GROUPS = 8
D_CHUNK = 128
D_FF = 4 * D_MODEL
HALF_W = 512
AB_IN = 3 * HALF_W + HALF_W + 2 * B_KV_HEADS * HEAD_DIM
CD_MAIN = 6 * HALF_W
N_GATES = 4 * C_HEADS
GATE_W = 256
QUAD = 4
ONES_LANE_LOWER = HEAD_DIM
ONES_LANE_UPPER = 0

LANES = 128
MXU_N = 256
VMEM_LIMIT = 48 * 1024 * 1024

NEG = -1e30


def _cparams(*sem):
    return pltpu.CompilerParams(dimension_semantics=sem, vmem_limit_bytes=VMEM_LIMIT)


def _rms_rows(x):
    return x * lax.rsqrt(jnp.mean(x * x, axis=-1, keepdims=True) + EPS)


def _split_dot(a, b):
    hi = a.astype(BF16)
    r1 = a - hi.astype(F32)
    mid = r1.astype(BF16)
    lo = (r1 - mid.astype(F32)).astype(BF16)
    dot = functools.partial(jnp.dot, preferred_element_type=F32)
    return dot(hi, b) + dot(mid, b) + dot(lo, b)


def _split_dot_rhs(b, a):
    hi = a.astype(BF16)
    r1 = a - hi.astype(F32)
    mid = r1.astype(BF16)
    lo = (r1 - mid.astype(F32)).astype(BF16)
    dot = functools.partial(jnp.dot, preferred_element_type=F32)
    return dot(b, hi) + dot(b, mid) + dot(b, lo)


def _head_mean_square(z, ones_blk, precise=True):
    z2 = z * z
    hi = z2.astype(BF16)
    total = jnp.dot(hi, ones_blk, preferred_element_type=F32)
    if precise:
        lo = (z2 - hi.astype(F32)).astype(BF16)
        total = total + jnp.dot(lo, ones_blk, preferred_element_type=F32)
    return total * (1.0 / HEAD_DIM)


def _head_block_ones(width=LANES):
    r = lax.broadcasted_iota(jnp.int32, (width, width), 0) // HEAD_DIM
    c = lax.broadcasted_iota(jnp.int32, (width, width), 1) // HEAD_DIM
    return (r == c).astype(BF16)


def _log_sigmoid(x):
    return jnp.minimum(x, 0.0) - jnp.log1p(jnp.exp(-jnp.abs(x)))


def _dot_nt(a, b):
    return lax.dot_general(a, b, (((1,), (1,)), ((), ())), preferred_element_type=F32)


def _dot_tn(a, b):
    return lax.dot_general(a, b, (((0,), (0,)), ((), ())), preferred_element_type=F32)


def _adaln_kernel(c_ref, w_ref, b_ref, o_ref):
    c = c_ref[...]
    ca = c * jax.nn.sigmoid(c)
    o_ref[...] = jnp.dot(ca.astype(BF16), w_ref[...], preferred_element_type=F32) + b_ref[...]


def _adaln(c, w, b):
    nb, d = c.shape
    n = w.shape[1]
    tn = 2048
    return pl.pallas_call(
        _adaln_kernel,
        grid=(n // tn,),
        in_specs=[pl.BlockSpec((nb, d), lambda j: (0, 0)),
                  pl.BlockSpec((d, tn), lambda j: (0, j)),
                  pl.BlockSpec((1, tn), lambda j: (0, j))],
        out_specs=pl.BlockSpec((nb, tn), lambda j: (0, j)),
        out_shape=jax.ShapeDtypeStruct((nb, n), F32),
        name="adaln",
        compiler_params=_cparams("parallel"),
    )(c, w, b)


def _swap_halves(z, half, upper):
    return jnp.where(upper, pltpu.roll(z, LANES - half, axis=1), pltpu.roll(z, half, axis=1))


def _inproj_even_kernel(x_ref, g_ref, sc_ref, sh_ref, w_ref, ta_ref, tq_ref, tk_ref,
                        qa_ref, ka_ref, va_ref, qb_ref, kb_ref, vb_ref, kna_ref, knb_ref):
    x = x_ref[0]
    h = _rms_rows(x) * (g_ref[...] * (1.0 + sc_ref[0])) + sh_ref[0]
    hb = h.astype(BF16)
    tm = x.shape[0]
    lane = lax.broadcasted_iota(jnp.int32, (tm, LANES), 1)
    up64 = (lane % 64) < 32
    up32 = (lane % 32) < 16
    ones_blk = _head_block_ones(MXU_N)
    q_scale = HEAD_DIM ** -0.5

    def rotary_1d(z):
        return z * ta_ref[:, :LANES] + _swap_halves(z, 32, up64) * ta_ref[:, LANES:]

    def rotary_axial(z, t_ref):
        return z * t_ref[:, :LANES] + _swap_halves(z, 16, up32) * t_ref[:, LANES:]

    def per_chunk(f, zz):
        return jnp.concatenate([f(zz[:, :LANES]), f(zz[:, LANES:])], axis=1)

    lower = lane < HEAD_DIM

    def store_pair(ref, first, second, z, fill, masked):
        if masked:
            lo = jnp.where(lower, z, jnp.where(lane == ONES_LANE_LOWER, fill, 0.0))
            up = jnp.where(lower, jnp.where(lane == ONES_LANE_UPPER, fill, 0.0), z)
        else:
            lo = jnp.where(lane == ONES_LANE_LOWER, fill, z)
            up = jnp.where(lane == ONES_LANE_UPPER, fill, z)
        ref[0, first] = lo.astype(BF16)
        ref[0, second] = up.astype(BF16)

    def store_queries(ref, heads, y):
        sq = HEAD_DIM * _head_mean_square(y, ones_blk, precise=False) + 1e-30
        norm = sq * lax.rsqrt(sq)
        for s, (first, second) in enumerate(heads):
            cols = slice(s * LANES, (s + 1) * LANES)
            store_pair(ref, first, second, y[:, cols],
                       pltpu.roll(norm[:, cols], HEAD_DIM, axis=1), True)

    def store_keys(ref, kn_ref, heads, y):
        ms = jnp.max(_head_mean_square(y, ones_blk, precise=False), axis=0, keepdims=True)
        for s, (first, second) in enumerate(heads):
            cols = slice(s * LANES, (s + 1) * LANES)
            store_pair(ref, first, second, y[:, cols], 1.0, False)
            kn_ref[0, first // 2] = jnp.broadcast_to(ms[:, cols], (8, LANES))

    projections = [jnp.dot(hb, w_ref[:, j * MXU_N:(j + 1) * MXU_N], preferred_element_type=F32)
                   for j in range(AB_IN // MXU_N)]
    half_b = B_Q_HEADS // 2
    for j, zz in enumerate(projections):
        pairs = [(4 * (j % 2), 4 * (j % 2) + 1), (4 * (j % 2) + 2, 4 * (j % 2) + 3)]
        if j < 2:
            store_queries(qa_ref, pairs, per_chunk(rotary_1d, zz) * q_scale)
        elif j < 4:
            store_keys(ka_ref, kna_ref, pairs, per_chunk(rotary_1d, zz))
        elif j < 6:
            for s, (first, second) in enumerate(pairs):
                store_pair(va_ref, first, second, zz[:, s * LANES:(s + 1) * LANES], 1.0, False)
        elif j < 8:
            y = per_chunk(lambda z: rotary_axial(z, tq_ref), zz)
            y = y * lax.rsqrt(_head_mean_square(zz, ones_blk, precise=False) + EPS)
            store_queries(qb_ref, [(2 * (j - 6), 2 * (j - 6) + half_b),
                                   (2 * (j - 6) + 1, 2 * (j - 6) + 1 + half_b)], y)
        else:
            ms = _head_mean_square(zz, ones_blk, precise=False)[:, :LANES]
            kb = rotary_axial(zz[:, :LANES], tk_ref) * lax.rsqrt(ms + EPS)
            store_keys(kb_ref, knb_ref, [(0, 1)], jnp.concatenate([kb, kb], axis=1))
            store_pair(vb_ref, 0, 1, zz[:, LANES:], 1.0, False)


def _inproj_even(x, g, sc, sh, w, ta, tq, tk, tm):
    b, s, d = x.shape
    n = w.shape[1]
    tab = pl.BlockSpec((tm, 2 * LANES), lambda i, bb: (i, 0))
    mod = pl.BlockSpec((1, 1, d), lambda i, bb: (bb, 0, 0))
    heads = (A_HEADS, A_HEADS, A_HEADS, B_Q_HEADS, B_KV_HEADS, B_KV_HEADS)
    return pl.pallas_call(
        _inproj_even_kernel,
        grid=(s // tm, b),
        in_specs=[pl.BlockSpec((1, tm, d), lambda i, bb: (bb, i, 0)),
                  pl.BlockSpec((1, d), lambda i, bb: (0, 0)),
                  mod, mod,
                  pl.BlockSpec((d, n), lambda i, bb: (0, 0)),
                  tab, tab, tab],
        out_specs=([pl.BlockSpec((1, nh, tm, LANES), lambda i, bb: (bb, 0, i, 0)) for nh in heads]
                   + [pl.BlockSpec((1, nc, 8, LANES), lambda i, bb: (bb, 0, i, 0))
                      for nc in (A_HEADS // 2, B_KV_HEADS // 2)]),
        out_shape=([jax.ShapeDtypeStruct((b, nh, s, LANES), BF16) for nh in heads]
                   + [jax.ShapeDtypeStruct((b, nc, 8 * (s // tm), LANES), F32)
                      for nc in (A_HEADS // 2, B_KV_HEADS // 2)]),
        name="inproj_even",
        compiler_params=_cparams("parallel", "parallel"),
    )(x, g, sc, sh, w, ta, tq, tk)


def _rope_tables(s, qk_g):
    pos = jnp.arange(s)

    def angles(p, dim):
        inv_freq = ROPE_THETA ** (-jnp.arange(0, dim, 2, dtype=F32) / dim)
        return p.astype(F32)[:, None] * inv_freq[None, :]

    a1 = angles(pos, HEAD_DIM)
    cos1, sin1 = jnp.cos(a1), jnp.sin(a1)
    direct = jnp.concatenate([cos1, cos1], axis=-1)
    swapped = jnp.concatenate([-sin1, sin1], axis=-1)
    ta = jnp.concatenate([direct, direct, swapped, swapped], axis=-1)

    ar = angles(pos // GRID_W, HEAD_DIM // 2)
    ac = angles(pos % GRID_W, HEAD_DIM // 2)
    cosb = jnp.concatenate([jnp.cos(ar), jnp.cos(ar), jnp.cos(ac), jnp.cos(ac)], axis=-1)
    sinb = jnp.concatenate([-jnp.sin(ar), jnp.sin(ar), -jnp.sin(ac), jnp.sin(ac)], axis=-1)

    def gained(gain, scale):
        gain = gain.astype(F32)
        gswap = gain.reshape(2, 2, HEAD_DIM // 4)[:, ::-1].reshape(HEAD_DIM)
        direct = cosb * gain * scale
        swapped = sinb * gswap * scale
        return jnp.concatenate([direct, direct, swapped, swapped], axis=-1)

    return ta, gained(qk_g[0], HEAD_DIM ** -0.5), gained(qk_g[1], 1.0)


LOG2E = 1.4426950408889634
LN2 = 0.6931471805599453
BOUND_SLACK = 1.02
MIN_DENOM = 2.0 ** -60


def _ones_lane(upper):
    return ONES_LANE_UPPER if upper else ONES_LANE_LOWER


def _query_coefficients(kn, upper, extra=0.0):
    ms = jnp.max(kn, axis=0, keepdims=True)
    kmax = jnp.sqrt(HEAD_DIM * (ms[:, :1] + ms[:, HEAD_DIM:HEAD_DIM + 1]) + 1.0)
    lane = lax.broadcasted_iota(jnp.int32, (1, LANES), 1)
    ones = lane == jnp.where(upper, ONES_LANE_UPPER, ONES_LANE_LOWER)
    return (jnp.where(ones, -LOG2E * BOUND_SLACK * kmax, LOG2E),
            jnp.where(ones, -LOG2E * extra, 0.0))


def _bounded_queries(q, coefficients):
    scale, offset = coefficients
    return (q.astype(F32) * scale + offset).astype(BF16)


def _online_step(q, k, v, bias, m_ref, acc_ref):
    s = _dot_nt(q, k)
    if bias is not None:
        s = s + bias
    m_prev = m_ref[...]
    m_new = jnp.maximum(m_prev, jnp.max(s, axis=-1, keepdims=True))
    p = jnp.exp(s - m_new).astype(BF16)
    acc_ref[...] = jnp.exp(m_prev - m_new) * acc_ref[...] + jnp.dot(p, v, preferred_element_type=F32)
    m_ref[...] = m_new


def _online_init(m_ref, acc_ref):
    m_ref[...] = jnp.full(m_ref.shape, -jnp.inf, F32)
    acc_ref[...] = jnp.zeros(acc_ref.shape, F32)


def _denominators(acc, upper):
    lane = _ones_lane(upper)
    return acc[:, lane:lane + 1]


def _normalised_pair(acc_lower, acc_upper):
    lane = lax.broadcasted_iota(jnp.int32, acc_lower.shape, 1)
    return jnp.where(lane < HEAD_DIM, acc_lower / _denominators(acc_lower, False),
                     acc_upper / _denominators(acc_upper, True))


def _gqa_kernel(q_ref, k_ref, v_ref, kn_ref, o_ref, qa_ref, acc_ref, m_ref, *, tk, rsub):
    nh, tq, dp = q_ref.shape[1:]
    gsz = nh // 2
    grows = gsz * tq
    nkv = k_ref.shape[2] // tk

    def group_q(g):
        return q_ref[0, g * gsz:(g + 1) * gsz].reshape(grows, dp)

    for g in range(2):
        qa_ref[g * grows:(g + 1) * grows, :] = _bounded_queries(
            group_q(g), _query_coefficients(kn_ref[0, 0], g == 1))
    acc_ref[...] = jnp.zeros(acc_ref.shape, F32)

    def bounded(j, carry):
        keys = pl.ds(pl.multiple_of(j * tk, tk), tk)
        subtiles = [(g, pl.ds(g * grows + r * rsub, rsub))
                    for g in range(2) for r in range(grows // rsub)]
        scores = [_dot_nt(qa_ref[rr, :], k_ref[0, g, keys, :]) for g, rr in subtiles]
        weights = [jnp.exp2(s).astype(BF16) for s in scores]
        for (g, rr), p in zip(subtiles, weights):
            acc_ref[rr, :] += jnp.dot(p, v_ref[0, g, keys, :], preferred_element_type=F32)
        return carry

    lax.fori_loop(0, nkv, bounded, 0, unroll=2 if nkv % 2 == 0 else 1)

    smallest =jnp.minimum(jnp.min(_denominators(acc_ref[:grows, :], False)),
                           jnp.min(_denominators(acc_ref[grows:, :], True)))

    @pl.when(smallest < MIN_DENOM)
    def _():
        _online_init(m_ref, acc_ref)

        def online(j, carry):
            keys = pl.ds(pl.multiple_of(j * tk, tk), tk)
            for g in range(2):
                rr = pl.ds(g * grows, grows)
                _online_step(group_q(g), k_ref[0, g, keys, :], v_ref[0, g, keys, :], None,
                             m_ref.at[rr, :], acc_ref.at[rr, :])
            return carry

        lax.fori_loop(0, nkv, online, 0)

    for j in range(gsz):
        lo = acc_ref[j * tq:(j + 1) * tq, :]
        up = acc_ref[grows + j * tq:grows + (j + 1) * tq, :]
        o_ref[0, :, j * LANES:(j + 1) * LANES] = _normalised_pair(lo, up).astype(o_ref.dtype)


def _gqa_attention(q, k, v, kn, tq, tk):
    b, hq, s, dp = q.shape
    assert k.shape[1] == 2 and hq % 2 == 0
    rows = hq * tq
    kv = pl.BlockSpec((1, 2, s, dp), lambda bb, i: (bb, 0, 0, 0))
    return pl.pallas_call(
        functools.partial(_gqa_kernel, tk=tk, rsub=min(256, rows // 2)),
        grid=(b, s // tq),
        in_specs=[pl.BlockSpec((1, hq, tq, dp), lambda bb, i: (bb, 0, i, 0)), kv, kv,
                  pl.BlockSpec((1,) + kn.shape[1:], lambda bb, i: (bb, 0, 0, 0))],
        out_specs=pl.BlockSpec((1, tq, hq * HEAD_DIM), lambda bb, i: (bb, i, 0)),
        out_shape=jax.ShapeDtypeStruct((b, s, hq * HEAD_DIM), BF16),
        scratch_shapes=[pltpu.VMEM((rows, dp), BF16), pltpu.VMEM((rows, dp), F32),
                        pltpu.VMEM((rows, 1), F32)],
        name="gqa_attention",
        compiler_params=_cparams("parallel", "parallel"),
    )(q, k, v, kn)


def _band_reach(t):
    return -(-max(w // 2 for w, _ in A_PATTERNS) // t)


def _band_bias(t, nproc):
    offs = range(-(nproc - 1), nproc)
    a = np.arange(t)[:, None]
    bcol = np.arange(t)[None, :]
    out = np.zeros((len(offs), t, t), np.float64)
    for n, d in enumerate(offs):
        j = d * t + bcol - a
        mult = np.zeros((t, t), np.float64)
        for window, dil in A_PATTERNS:
            mult += ((j % dil) == 0) & (np.abs(j) <= window // 2)
        out[n] = np.where(mult > 0, LOG2E * np.log(np.maximum(mult, 1.0)), NEG)
    return jnp.asarray(out, F32)


BAND_REACH = A_PATTERNS[0][0] // (2 * A_PATTERNS[0][1])
SUB = 2 * BAND_REACH


def _band_masks(win):
    t = np.arange(SUB)[:, None]
    c = np.arange(win)[None, :]
    out = [np.where(np.abs(c - idx * BAND_REACH - t) <= BAND_REACH, 0.0, NEG) for idx in range(3)]
    return jnp.asarray(np.stack(out), F32)


def _band_window(q0, length, win):
    if isinstance(q0, int):
        start = min(max(q0 - BAND_REACH, 0), length - win)
    else:
        start = pl.multiple_of(jnp.clip(q0 - BAND_REACH, 0, length - win), BAND_REACH)
    return start, (q0 - start) // BAND_REACH


def _band_tiles(jobs, q_of, k_of, v_of, mask_ref, coefficients, acc_of):
    scores = [_dot_nt(_bounded_queries(q_of(j), coefficients(j)), k_of(j)) + mask_ref[idx]
              for j, idx in jobs]
    weights = [jnp.exp2(s).astype(BF16) for s in scores]
    for (j, _), p in zip(jobs, weights):
        acc_of(j, jnp.dot(p, v_of(j), preferred_element_type=F32))


def _dilated_pass_kernel(*refs, d, win, group, has_prev):
    q_ref, k_ref, v_ref, kn_ref = refs[:4]
    prev_ref = refs[4] if has_prev else None
    mask_ref, o_ref, acc_ref = refs[4 + has_prev:]
    length = q_ref.shape[2]
    upper = pl.program_id(1) % 2 == 1
    coeff = _query_coefficients(kn_ref[0, 0], upper)
    if has_prev:
        acc_ref[...] = prev_ref[0, 0].astype(F32)
    else:
        acc_ref[...] = jnp.zeros(acc_ref.shape, F32)

    tiles = [(r, q0) for r in range(d) for q0 in range(0, length, SUB)]

    def lanes(j):
        return slice(tiles[j][0] * LANES, (tiles[j][0] + 1) * LANES)

    def rows(j):
        return slice(tiles[j][1], tiles[j][1] + SUB)

    def keys(j):
        start, _ = _band_window(tiles[j][1], length, win)
        return slice(start, start + win)

    def accumulate(j, update):
        acc_ref[rows(j), lanes(j)] += update

    for g in range(0, len(tiles), group):
        jobs = [(j, _band_window(tiles[j][1], length, win)[1])
                for j in range(g, min(g + group, len(tiles)))]
        _band_tiles(jobs,
                    lambda j: q_ref[0, 0, rows(j), lanes(j)],
                    lambda j: k_ref[0, 0, keys(j), lanes(j)],
                    lambda j: v_ref[0, 0, keys(j), lanes(j)],
                    mask_ref, lambda j: coeff, accumulate)
    o_ref[0, 0] = acc_ref[...].astype(o_ref.dtype)


def _dilated_pass(q, k, v, kn, prev, d):
    b, h, s, dp = q.shape
    length = s // d
    win = min(2 * SUB, length)
    view = lambda a: a.reshape(b, h, length, d * dp)
    spec = pl.BlockSpec((1, 1, length, d * dp), lambda bb, hh: (bb, hh, 0, 0))
    masks = _band_masks(win)
    operands = [view(q), view(k), view(v), kn] + ([view(prev)] if prev is not None else [])
    out = pl.pallas_call(
        functools.partial(_dilated_pass_kernel, d=d, win=win, group=8, has_prev=prev is not None),
        grid=(b, h),
        in_specs=([spec, spec, spec,
                   pl.BlockSpec((1, 1) + kn.shape[2:], lambda bb, hh: (bb, hh // 2, 0, 0))]
                  + ([spec] if prev is not None else [])
                  + [pl.BlockSpec(masks.shape, lambda bb, hh: (0, 0, 0))]),
        out_specs=spec,
        out_shape=jax.ShapeDtypeStruct((b, h, length, d * dp), BF16),
        scratch_shapes=[pltpu.VMEM((length, d * dp), F32)],
        name=f"dilated_pass_{d}",
        compiler_params=_cparams("parallel", "parallel"),
    )(*operands, masks)
    return out.reshape(b, h, s, dp)


def _dilated_final_kernel(q_ref, k_ref, v_ref, kn_ref, prev_ref, mask_ref, bias_ref, o_ref,
                          acc_ref, m_ref, *, nproc):
    t, dp = q_ref.shape[2:]
    s = k_ref.shape[2]
    i = pl.program_id(2)
    win = mask_ref.shape[2]
    coeffs = [_query_coefficients(kn_ref[0, 0], h == 1) for h in range(2)]
    acc_ref[...] = prev_ref[0].astype(F32)

    tiles = [(h, u * SUB) for h in range(2) for u in range(t // SUB)]
    windows = [_band_window(i * t + r0, s, win) for _, r0 in tiles]

    def accumulate(j, update):
        acc_ref[tiles[j][0], tiles[j][1]:tiles[j][1] + SUB, :] += update

    _band_tiles([(j, windows[j][1]) for j in range(len(tiles))],
                lambda j: q_ref[0, tiles[j][0], tiles[j][1]:tiles[j][1] + SUB, :],
                lambda j: k_ref[0, tiles[j][0], pl.ds(windows[j][0], win), :],
                lambda j: v_ref[0, tiles[j][0], pl.ds(windows[j][0], win), :],
                mask_ref, lambda j: coeffs[tiles[j][0]], accumulate)

    smallest = jnp.minimum(jnp.min(_denominators(acc_ref[0], False)),
                           jnp.min(_denominators(acc_ref[1], True)))

    @pl.when(smallest < MIN_DENOM)
    def _():
        first = jnp.clip(i - _band_reach(t), 0, s // t - nproc)
        for h in range(2):
            acc = acc_ref.at[h]
            _online_init(m_ref, acc)

            def online(n, carry):
                kb = first + n
                keys = pl.ds(pl.multiple_of(kb * t, t), t)
                _online_step(q_ref[0, h], k_ref[0, h, keys, :], v_ref[0, h, keys, :],
                             LN2 * bias_ref[kb - i + nproc - 1], m_ref, acc)
                return carry

            lax.fori_loop(0, nproc, online, 0)

    o_ref[0] = _normalised_pair(acc_ref[0], acc_ref[1]).astype(o_ref.dtype)


def _dilated_attention(q, k, v, kn, t):
    b, h, s, dp = q.shape
    assert all(w // (2 * dil) == BAND_REACH for w, dil in A_PATTERNS) and A_PATTERNS[0][1] == 1
    prev = None
    for _, dil in reversed(A_PATTERNS[1:]):
        prev = _dilated_pass(q, k, v, kn, prev, dil)
    nproc = min(2 * _band_reach(t) + 1, s // t)
    bias = _band_bias(t, nproc)
    masks = _band_masks(2 * SUB)
    kv = pl.BlockSpec((1, 2, s, dp), lambda bb, hh, i: (bb, hh, 0, 0))
    blk = pl.BlockSpec((1, 2, t, dp), lambda bb, hh, i: (bb, hh, i, 0))
    return pl.pallas_call(
        functools.partial(_dilated_final_kernel, nproc=nproc),
        grid=(b, h // 2, s // t),
        in_specs=[blk, kv, kv,
                  pl.BlockSpec((1, 1) + kn.shape[2:], lambda bb, hh, i: (bb, hh, 0, 0)),
                  blk,
                  pl.BlockSpec(masks.shape, lambda bb, hh, i: (0, 0, 0)),
                  pl.BlockSpec(bias.shape, lambda bb, hh, i: (0, 0, 0),
                               pipeline_mode=pl.Buffered(1))],
        out_specs=pl.BlockSpec((1, t, LANES), lambda bb, hh, i: (bb, i, hh)),
        out_shape=jax.ShapeDtypeStruct((b, s, h * HEAD_DIM), BF16),
        scratch_shapes=[pltpu.VMEM((2, t, dp), F32), pltpu.VMEM((t, 1), F32)],
        name="dilated_final",
        compiler_params=_cparams("parallel", "parallel", "parallel"),
    )(q, k, v, kn, prev, masks, bias)


def _residual_update(x, y, gate, norm_g):
    return x + gate * (_rms_rows(y) * norm_g)


def _mixer_output_even(oa_ref, ob_ref):
    return oa_ref[0], ob_ref[0]


def _mixer_output_odd(hf_ref, hb_ref, og_ref, d_ref, mhg_ref):
    ones_blk = _head_block_ones()
    parts = []
    for c in range(HALF_W // LANES):
        cols = slice(c * LANES, (c + 1) * LANES)
        hs = hf_ref[0, :, cols] + hb_ref[0, :, cols]
        hn = hs * lax.rsqrt(_head_mean_square(hs, ones_blk) + EPS) * mhg_ref[:, cols]
        parts.append((hn * jax.nn.sigmoid(og_ref[0, :, cols].astype(F32))).astype(BF16))
    return jnp.concatenate(parts, axis=-1), d_ref[0]


def _layer_tail_kernel(*refs, odd, tf):
    n_mix = 5 if odd else 2
    mix = (_mixer_output_odd if odd else _mixer_output_even)(*refs[:n_mix])
    (wo_ref, x_ref, g1_ref, ng1_ref, ng2_ref, sc_ref, sh_ref, w1_ref, w2_ref, g2_ref, ng3_ref,
     o_ref, a_ref) = refs[n_mix:]
    y = jnp.dot(mix[0], wo_ref[:HALF_W, :], preferred_element_type=F32)
    y = y + jnp.dot(mix[1], wo_ref[HALF_W:, :], preferred_element_type=F32)
    x1 = _residual_update(x_ref[0], y, g1_ref[0], ng1_ref[...])
    h = (_rms_rows(x1) * (ng2_ref[...] * (1.0 + sc_ref[0])) + sh_ref[0]).astype(BF16)
    for kc in range(w1_ref.shape[1] // tf):
        cols = slice(kc * tf, (kc + 1) * tf)
        a = jnp.maximum(jnp.dot(h, w1_ref[:, cols], preferred_element_type=F32), 0.0)
        a_ref[:, cols] = (a * a).astype(BF16)
    y2 = jnp.dot(a_ref[...], w2_ref[...], preferred_element_type=F32)
    o_ref[0] = _residual_update(x1, y2, g2_ref[0], ng3_ref[...])


def _layer_tail(mix_inputs, mix_specs, odd, wo, x, g1, ng1, ng2, sc, sh, w1, w2, g2, ng3, tm, tf):
    b, s, d = x.shape
    f = w1.shape[1]
    full = pl.BlockSpec((1, tm, d), lambda bb, i: (bb, i, 0))
    mod = pl.BlockSpec((1, 1, d), lambda bb, i: (bb, 0, 0))
    vec = pl.BlockSpec((1, d), lambda bb, i: (0, 0))

    def resident(shape):
        return pl.BlockSpec(shape, lambda bb, i: (0, 0), pipeline_mode=pl.Buffered(1))

    return pl.pallas_call(
        functools.partial(_layer_tail_kernel, odd=odd, tf=tf),
        grid=(b, s // tm),
        in_specs=mix_specs + [resident((d, d)), full, mod, vec, vec, mod, mod,
                              resident((d, f)), resident((f, d)), mod, vec],
        out_specs=full,
        out_shape=jax.ShapeDtypeStruct((b, s, d), F32),
        scratch_shapes=[pltpu.VMEM((tm, f), BF16)],
        name="layer_tail_odd" if odd else "layer_tail_even",
        compiler_params=_cparams("parallel", "parallel"),
    )(*mix_inputs, wo, x, g1, ng1, ng2, sc, sh, w1, w2, g2, ng3)


def _layer_tail_even(oa, ob, *rest, tm, tf):
    half = pl.BlockSpec((1, tm, HALF_W), lambda bb, i: (bb, i, 0))
    return _layer_tail([oa, ob], [half, half], False, *rest, tm, tf)


def _layer_tail_odd(hf, hb, z, dout, mhg, *rest, tm, tf):
    half = pl.BlockSpec((1, tm, HALF_W), lambda bb, i: (bb, i, 0))
    og = pl.BlockSpec((1, tm, HALF_W), lambda bb, i: (bb, i, 3))
    return _layer_tail([hf, hb, z, dout, mhg],
                       [half, half, og, half, pl.BlockSpec((1, HALF_W), lambda bb, i: (0, 0))],
                       True, *rest, tm, tf)


def _inproj_odd_kernel(x_ref, g_ref, sc_ref, sh_ref, w_ref, gb_ref, z_ref, gates_ref):
    x = x_ref[0]
    h = _rms_rows(x) * (g_ref[...] * (1.0 + sc_ref[0])) + sh_ref[0]
    hb = h.astype(BF16)
    k_scale = HEAD_DIM ** -0.5
    for j in range(CD_MAIN // MXU_N):
        cols = slice(j * MXU_N, (j + 1) * MXU_N)
        zz = jnp.dot(hb, w_ref[:, cols], preferred_element_type=F32)
        if HALF_W <= j * MXU_N < 2 * HALF_W:
            zz = zz * k_scale
        z_ref[0, :, cols] = zz.astype(BF16)
    gates_ref[0] = jnp.dot(hb, w_ref[:, CD_MAIN:], preferred_element_type=F32) + gb_ref[...]


def _inproj_odd(x, g, sc, sh, w, gb, tm):
    b, s, d = x.shape
    n = w.shape[1]
    mod = pl.BlockSpec((1, 1, d), lambda i, bb: (bb, 0, 0))
    return pl.pallas_call(
        _inproj_odd_kernel,
        grid=(s // tm, b),
        in_specs=[pl.BlockSpec((1, tm, d), lambda i, bb: (bb, i, 0)),
                  pl.BlockSpec((1, d), lambda i, bb: (0, 0)),
                  mod, mod,
                  pl.BlockSpec((d, n), lambda i, bb: (0, 0)),
                  pl.BlockSpec((1, GATE_W), lambda i, bb: (0, 0))],
        out_specs=[pl.BlockSpec((1, tm, CD_MAIN), lambda i, bb: (bb, i, 0)),
                   pl.BlockSpec((1, tm, GATE_W), lambda i, bb: (bb, i, 0))],
        out_shape=[jax.ShapeDtypeStruct((b, s, CD_MAIN), BF16),
                   jax.ShapeDtypeStruct((b, s, GATE_W), F32)],
        name="inproj_odd",
        compiler_params=_cparams("parallel", "parallel"),
    )(x, g, sc, sh, w, gb)


def _gate_lane(rev, head):
    return (C_HEADS if rev else 0) + head


def _mlstm_constants(rev, lc):
    wq = QUAD * HEAD_DIM
    t = np.arange(lc)
    sees = (t[None, :] >= t[:, None]) if rev else (t[None, :] <= t[:, None])
    nq = C_HEADS // QUAD
    sel_s = np.zeros((nq, LANES, QUAD * lc), np.float32)
    sel_v = np.zeros((nq, LANES, wq), np.float32)
    for qd in range(nq):
        for hh in range(QUAD):
            lane = _gate_lane(rev, QUAD * qd + hh)
            sel_s[qd, lane, hh * lc:(hh + 1) * lc] = 1.0
            sel_v[qd, lane, hh * HEAD_DIM:(hh + 1) * HEAD_DIM] = 1.0
    head_of_row = np.arange(QUAD * lc) // lc
    head_of_col = np.arange(wq) // HEAD_DIM
    return dict(
        tri=jnp.asarray(sees, BF16),
        sel_s=jnp.asarray(sel_s, BF16), sel_v=jnp.asarray(sel_v, BF16),
        ones_bd=jnp.asarray(np.transpose(sel_s, (0, 2, 1)), BF16),
        row_head=jnp.asarray(head_of_row[:, None] == head_of_col[None, :], BF16),
        diag=jnp.asarray(head_of_col[:, None] == head_of_col[None, :], F32),
        causal=jnp.asarray(np.tile(sees, (1, QUAD)), F32))


def _scan_max(x, rev):
    n = x.shape[0]
    row = lax.broadcasted_iota(jnp.int32, x.shape, 0)
    sh = 1
    while sh < n:
        if rev:
            x = jnp.maximum(x, jnp.where(row < n - sh, pltpu.roll(x, n - sh, axis=0), -jnp.inf))
        else:
            x = jnp.maximum(x, jnp.where(row >= sh, pltpu.roll(x, sh, axis=0), -jnp.inf))
        sh *= 2
    return x


def _mlstm_kernel(*refs, lc):
    nd = len(_MLSTM_DIR_CONSTS)
    blocks = (refs[0:4], refs[4:8])
    rowhead_ref, diag_ref = refs[8:10]
    consts = (refs[10:10 + nd], refs[10 + nd:10 + 2 * nd])
    outs = refs[10 + 2 * nd:12 + 2 * nd]
    c_sc, n_sc, m_sc = refs[12 + 2 * nd:]

    @pl.when(pl.program_id(1) == 0)
    def _():
        c_sc[...] = jnp.zeros(c_sc.shape, F32)
        n_sc[...] = jnp.zeros(n_sc.shape, F32)
        m_sc[...] = jnp.zeros(m_sc.shape, F32)

    nch = outs[0].shape[1] // lc

    wq = QUAD * HEAD_DIM
    edges = (lc - 1, 0)
    jobs = [(rev, qd) for rev in (0, 1) for qd in range(C_HEADS // QUAD)]
    dot = functools.partial(jnp.dot, preferred_element_type=F32)

    def chunk(ci, carry):
        rows = (pl.ds(pl.multiple_of(ci * lc, lc), lc),
                pl.ds(pl.multiple_of((nch - 1 - ci) * lc, lc), lc))
        gate = []
        for rev in (0, 1):
            g_ref = blocks[rev][3]
            tri_ref = consts[rev][0]
            gi = g_ref[0, rows[rev], :LANES]
            bcum = _split_dot_rhs(tri_ref[...], _log_sigmoid(g_ref[0, rows[rev], LANES:]))
            a = gi - bcum
            m_prev = m_sc[rev]
            mt_b = jnp.maximum(m_prev, _scan_max(a, bool(rev))).astype(BF16)
            mt = mt_b.astype(F32)
            w_b = jnp.exp(m_prev - mt).astype(BF16)
            b_all = bcum[edges[rev]:edges[rev] + 1, :]
            log_w = b_all + a
            m_new = jnp.maximum(b_all + m_prev, jnp.max(log_w, axis=0, keepdims=True))
            gate.append(dict(
                mt_b=mt_b, w_b=w_b, w=w_b.astype(F32),
                floor=jnp.exp(-(bcum + mt)),
                wk_b=jnp.exp(log_w - m_new).astype(BF16),
                decay=jnp.broadcast_to(jnp.exp(b_all + m_prev - m_new), (8, LANES)),
                a_t=a.T))
            m_sc[rev] = m_new

        row_head = rowhead_ref[...]

        def operand(rev, qd, j):
            return blocks[rev][j][0, rows[rev], qd * wq:(qd + 1) * wq]

        def sel_v(rev, qd):
            return consts[rev][2][qd]

        scores, log_e = [], []
        for rev, qd in jobs:
            kbd = jnp.concatenate([operand(rev, qd, 1)] * QUAD, axis=0) * row_head
            scores.append(_dot_nt(operand(rev, qd, 0), kbd))
            a_rows = jnp.concatenate(
                [jnp.broadcast_to(gate[rev]["a_t"][ln:ln + 1, :], (lc, lc))
                 for ln in (_gate_lane(bool(rev), QUAD * qd + hh) for hh in range(QUAD))], axis=1)
            log_e.append(jnp.where(consts[rev][4][...] > 0.0,
                                   a_rows - dot(gate[rev]["mt_b"], consts[rev][1][qd]), NEG))
        probs = [(s * jnp.exp(e)).astype(BF16) for s, e in zip(scores, log_e)]

        pvs, qcs = [], []
        for (rev, qd), p in zip(jobs, probs):
            v = operand(rev, qd, 2)
            vaug = jnp.concatenate(
                [jnp.concatenate([v] * QUAD, axis=0) * row_head, consts[rev][3][qd]], axis=1)
            pvs.append(dot(p, vaug))
            caug = jnp.concatenate(
                [c_sc[rev, qd].astype(BF16),
                 (sel_v(rev, qd).astype(F32) * n_sc[rev, qd]).astype(BF16)], axis=0)
            qcs.append(_dot_nt(operand(rev, qd, 0), caug))

        for (rev, qd), pv, qc in zip(jobs, pvs, qcs):
            g = gate[rev]
            den = pv[:, wq:] + g["w"] * qc[:, wq:]
            scale_b = (1.0 / jnp.maximum(jnp.abs(den), g["floor"])).astype(BF16)
            outs[rev][0, rows[rev], qd * wq:(qd + 1) * wq] = (
                dot(scale_b, sel_v(rev, qd)) * (pv[:, :wq] + dot(g["w_b"], sel_v(rev, qd)) * qc[:, :wq]))

        for rev, qd in jobs:
            g = gate[rev]
            k = operand(rev, qd, 1)
            wk_v = dot(g["wk_b"], sel_v(rev, qd))
            dec_v = _split_dot(g["decay"], sel_v(rev, qd))[:1]
            vw = (wk_v * operand(rev, qd, 2).astype(F32)).astype(BF16)
            c_sc[rev, qd] = dec_v * c_sc[rev, qd] + diag_ref[...] * _dot_tn(vw, k)
            n_sc[rev, qd] = (dec_v * n_sc[rev, qd]
                             + jnp.sum(wk_v * k.astype(F32), axis=0, keepdims=True))
        return carry

    lax.fori_loop(0, nch, chunk, 0, unroll=2 if nch % 2 == 0 else 1)


_MLSTM_DIR_CONSTS = ("tri", "sel_s", "sel_v", "ones_bd", "causal")


def _mlstm(z, gates, tb, lc):
    b, s, _ = z.shape
    nblk = s // tb
    position = (lambda i: i, lambda i: nblk - 1 - i)

    def col(rev, j):
        return pl.BlockSpec((1, tb, HALF_W), lambda bb, i: (bb, position[rev](i), j))

    def block_specs(rev):
        return [col(rev, 0), col(rev, 1), col(rev, 2),
                pl.BlockSpec((1, tb, GATE_W), lambda bb, i: (bb, position[rev](i), 0))]

    def const_spec(a):
        return pl.BlockSpec(a.shape, lambda bb, i: (0,) * a.ndim)

    consts = (_mlstm_constants(False, lc), _mlstm_constants(True, lc))
    shared = [consts[0]["row_head"], consts[0]["diag"]]
    per_dir = [consts[rev][n] for rev in (0, 1) for n in _MLSTM_DIR_CONSTS]
    nq = C_HEADS // QUAD
    wq = QUAD * HEAD_DIM
    return pl.pallas_call(
        functools.partial(_mlstm_kernel, lc=lc),
        grid=(b, nblk),
        in_specs=block_specs(0) + block_specs(1) + [const_spec(a) for a in shared + per_dir],
        out_specs=[col(0, 0), col(1, 0)],
        out_shape=[jax.ShapeDtypeStruct((b, s, HALF_W), F32)] * 2,
        scratch_shapes=[pltpu.VMEM((2, nq, wq, wq), F32), pltpu.VMEM((2, nq, 1, wq), F32),
                        pltpu.VMEM((2, 1, LANES), F32)],
        name="mlstm",
        compiler_params=_cparams("parallel", "arbitrary"),
    )(z, z, z, gates, z, z, z, gates, *shared, *per_dir)


def _sgu_kernel(u_ref, v_ref, lng_ref, w_ref, b_ref, o_ref):
    tb = u_ref.shape[1]
    group = lax.broadcasted_iota(jnp.int32, (D_CHUNK, HALF_W), 1) // HEAD_DIM
    for ch in range(tb // D_CHUNK):
        rows = slice(ch * D_CHUNK, (ch + 1) * D_CHUNK)
        u = jax.nn.gelu(u_ref[0, rows, :].astype(F32))
        v = jax.nn.gelu(v_ref[0, rows, :].astype(F32))
        dv = v - jnp.mean(v, axis=-1, keepdims=True)
        vn = dv * lax.rsqrt(jnp.mean(dv * dv, axis=-1, keepdims=True) + EPS) * lng_ref[...]
        full = jnp.dot(w_ref[...], vn.astype(BF16), preferred_element_type=F32)
        sg = b_ref[...]
        for g in range(D_GROUPS):
            sg = sg + jnp.where(group == g, full[g * D_CHUNK:(g + 1) * D_CHUNK, :], 0.0)
        o_ref[0, rows, :] = (u * sg).astype(BF16)


def _sgu(z, lng, w, bexp, tb):
    b, s, _ = z.shape
    col = lambda j: pl.BlockSpec((1, tb, HALF_W), lambda bb, i: (bb, i, j))
    return pl.pallas_call(
        _sgu_kernel,
        grid=(b, s // tb),
        in_specs=[col(4), col(5),
                  pl.BlockSpec((1, HALF_W), lambda bb, i: (0, 0)),
                  pl.BlockSpec((D_GROUPS * D_CHUNK, D_CHUNK), lambda bb, i: (0, 0)),
                  pl.BlockSpec((D_CHUNK, HALF_W), lambda bb, i: (0, 0))],
        out_specs=col(0),
        out_shape=jax.ShapeDtypeStruct((b, s, HALF_W), BF16),
        name="sgu",
        compiler_params=_cparams("parallel", "parallel"),
    )(z, z, lng, w, bexp)


def _tiles(s):
    return dict(tm=min(512, s), tf=1024, tq=min(128, s), tk=min(512, s), tband=min(512, s),
                tb_mlstm=min(1024, s), lc=128, tb_sgu=min(512, s))


def _gqa_head_order():
    half = B_Q_HEADS // 2
    return [h for j in range(half) for h in (j, half + j)]


def _trunk(x, c, p):
    b, s, d = x.shape
    t = _tiles(s)
    mod = _adaln(c, p["w_ada"], p["b_ada"])
    ta, tq, tk = _rope_tables(s, p["qk_norm_g"])
    for layer in range(2):
        m = mod[:, layer * 6 * d:(layer + 1) * 6 * d].reshape(b, 6, 1, d)
        sh1, sc1, g1, sh2, sc2, g2 = (m[:, j] for j in range(6))
        ng = p["norm_g"][layer].reshape(4, 1, d)
        tail = (x, g1, ng[1], ng[2], sc2, sh2, p["w_ff1"][layer], p["w_ff2"][layer], g2, ng[3])
        if layer == 0:
            qa, ka, va, qb, kb, vb, kna, knb = _inproj_even(x, ng[0], sc1, sh1, p["w_in_ab"],
                                                            ta, tq, tk, t["tm"])
            oa = _dilated_attention(qa, ka, va, kna, t["tband"])
            ob = _gqa_attention(qb, kb, vb, knb, t["tq"], t["tk"])
            x = _layer_tail_even(oa, ob, p["w_out_ab"], *tail, tm=t["tm"], tf=t["tf"])
        else:
            z, gates = _inproj_odd(x, ng[0], sc1, sh1, p["w_in_cd"], p["gate_bias"], t["tm"])
            hf, hb = _mlstm(z, gates, t["tb_mlstm"], t["lc"])
            dout = _sgu(z, p["sg_norm_g"], p["w_spatial"], p["b_spatial"], t["tb_sgu"])
            x = _layer_tail_odd(hf, hb, z, dout, p["mh_norm_g"], p["w_out_cd"], *tail,
                                tm=t["tm"], tf=t["tf"])
    return x


def _prepare_params(w_in_ab, w_out_ab, qk_norm_g, w_in_cd, w_out_cd, gate_bias, mh_norm_g,
                    sg_norm_g, w_spatial, b_spatial, w_ada, b_ada, norm_g, w_ff1, w_ff2):
    depth, d, _ = w_ada.shape
    wcd = w_in_cd[0]
    gate_lo = 4 * HALF_W
    gate_hi = gate_lo + N_GATES

    def gate_tiles(g):
        i_fw, f_fw, i_bw, f_bw = (g[:, j * C_HEADS:(j + 1) * C_HEADS] for j in range(4))
        pad = jnp.zeros((g.shape[0], LANES - 2 * C_HEADS), g.dtype)
        return jnp.concatenate([i_fw, i_bw, pad, f_fw, f_bw, pad], axis=1)

    wcd = jnp.concatenate([wcd[:, :gate_lo], wcd[:, gate_hi:],
                           gate_tiles(wcd[:, gate_lo:gate_hi])], axis=1)
    gb = gate_tiles(gate_bias[0].reshape(1, N_GATES).astype(F32))
    bexp = jnp.repeat(b_spatial[0].astype(F32).T, HEAD_DIM, axis=1)

    qb_lo = 3 * HALF_W
    order = np.asarray(_gqa_head_order())
    head_cols = (order[:, None] * HEAD_DIM + np.arange(HEAD_DIM)[None, :]).reshape(-1)
    wab = w_in_ab[0]
    wab = jnp.concatenate([wab[:, :qb_lo], wab[:, qb_lo + head_cols], wab[:, qb_lo + HALF_W:]],
                          axis=1)
    wout_ab = w_out_ab[0]
    wout_ab = jnp.concatenate([wout_ab[:HALF_W], wout_ab[HALF_W + head_cols]], axis=0)
    return dict(
        w_ada=jnp.concatenate([w_ada[l] for l in range(depth)], axis=1).astype(BF16),
        b_ada=b_ada.reshape(1, -1).astype(F32),
        w_in_ab=wab.astype(BF16), w_out_ab=wout_ab.astype(BF16),
        qk_norm_g=qk_norm_g[0],
        w_in_cd=wcd.astype(BF16), w_out_cd=w_out_cd[0].astype(BF16), gate_bias=gb,
        mh_norm_g=mh_norm_g[0].reshape(1, HALF_W).astype(F32),
        sg_norm_g=sg_norm_g[0].reshape(1, HALF_W).astype(F32),
        w_spatial=w_spatial[0].reshape(D_GROUPS * D_CHUNK, D_CHUNK).astype(BF16),
        b_spatial=bexp,
        norm_g=norm_g.astype(F32), w_ff1=w_ff1.astype(BF16), w_ff2=w_ff2.astype(BF16))


def kernel(x_prompt, x_sample, c_prompt, c_sample, w_in_ab, w_out_ab, qk_norm_g, w_in_cd, w_out_cd, gate_bias, mh_norm_g, sg_norm_g, w_spatial, b_spatial, w_ada, b_ada, norm_g, w_ff1, w_ff2):
    p = _prepare_params(w_in_ab, w_out_ab, qk_norm_g, w_in_cd, w_out_cd, gate_bias, mh_norm_g,
                        sg_norm_g, w_spatial, b_spatial, w_ada, b_ada, norm_g, w_ff1, w_ff2)
    return (_trunk(x_prompt, c_prompt, p), _trunk(x_sample, c_sample, p))
```

```python
import functools

import numpy as np
import jax
import jax.numpy as jnp
from jax import lax
from jax.experimental import pallas as pl
from jax.experimental.pallas import tpu as pltpu

F32 = jnp.float32
BF16 = jnp.bfloat16

D_MODEL = 1024
HEAD_DIM = 64
GRID_W = 64
ROPE_THETA = 10000.0
EPS = 1e-6
A_HEADS = 8
A_PATTERNS = ((128, 1), (512, 4), (2048, 16))
B_Q_HEADS = 8
B_KV_HEADS = 2
C_HEADS = 8
D_GROUPS = 8
D_CHUNK = 128
D_FF = 4 * D_MODEL
HALF_W = 512
AB_IN = 3 * HALF_W + HALF_W + 2 * B_KV_HEADS * HEAD_DIM
CD_MAIN = 6 * HALF_W
N_GATES = 4 * C_HEADS
GATE_W = 256
QUAD = 4
ONES_LANE_LOWER = HEAD_DIM
ONES_LANE_UPPER = 0

LANES = 128
MXU_N = 256
VMEM_LIMIT = 48 * 1024 * 1024

NEG = -1e30


def _cparams(*sem):
    return pltpu.CompilerParams(dimension_semantics=sem, vmem_limit_bytes=VMEM_LIMIT)


def _rms_rows(x):
    return x * lax.rsqrt(jnp.mean(x * x, axis=-1, keepdims=True) + EPS)


def _split_dot(a, b):
    hi = a.astype(BF16)
    r1 = a - hi.astype(F32)
    mid = r1.astype(BF16)
    lo = (r1 - mid.astype(F32)).astype(BF16)
    dot = functools.partial(jnp.dot, preferred_element_type=F32)
    return dot(hi, b) + dot(mid, b) + dot(lo, b)


def _split_dot_rhs(b, a):
    hi = a.astype(BF16)
    r1 = a - hi.astype(F32)
    mid = r1.astype(BF16)
    lo = (r1 - mid.astype(F32)).astype(BF16)
    dot = functools.partial(jnp.dot, preferred_element_type=F32)
    return dot(b, hi) + dot(b, mid) + dot(b, lo)


def _head_mean_square(z, ones_blk, precise=True):
    z2 = z * z
    hi = z2.astype(BF16)
    total = jnp.dot(hi, ones_blk, preferred_element_type=F32)
    if precise:
        lo = (z2 - hi.astype(F32)).astype(BF16)
        total = total + jnp.dot(lo, ones_blk, preferred_element_type=F32)
    return total * (1.0 / HEAD_DIM)


def _head_block_ones(width=LANES):
    r = lax.broadcasted_iota(jnp.int32, (width, width), 0) // HEAD_DIM
    c = lax.broadcasted_iota(jnp.int32, (width, width), 1) // HEAD_DIM
    return (r == c).astype(BF16)


def _log_sigmoid(x):
    return jnp.minimum(x, 0.0) - jnp.log1p(jnp.exp(-jnp.abs(x)))


def _dot_nt(a, b):
    return lax.dot_general(a, b, (((1,), (1,)), ((), ())), preferred_element_type=F32)


def _dot_tn(a, b):
    return lax.dot_general(a, b, (((0,), (0,)), ((), ())), preferred_element_type=F32)


def _adaln_kernel(c_ref, w_ref, b_ref, o_ref):
    c = c_ref[...]
    ca = c * jax.nn.sigmoid(c)
    o_ref[...] = jnp.dot(ca.astype(BF16), w_ref[...], preferred_element_type=F32) + b_ref[...]


def _adaln(c, w, b):
    nb, d = c.shape
    n = w.shape[1]
    tn = 2048
    return pl.pallas_call(
        _adaln_kernel,
        grid=(n // tn,),
        in_specs=[pl.BlockSpec((nb, d), lambda j: (0, 0)),
                  pl.BlockSpec((d, tn), lambda j: (0, j)),
                  pl.BlockSpec((1, tn), lambda j: (0, j))],
        out_specs=pl.BlockSpec((nb, tn), lambda j: (0, j)),
        out_shape=jax.ShapeDtypeStruct((nb, n), F32),
        name="adaln",
        compiler_params=_cparams("parallel"),
    )(c, w, b)


def _swap_halves(z, half, upper):
    return jnp.where(upper, pltpu.roll(z, LANES - half, axis=1), pltpu.roll(z, half, axis=1))


def _inproj_even_kernel(x_ref, g_ref, sc_ref, sh_ref, w_ref, ta_ref, tq_ref, tk_ref,
                        qa_ref, ka_ref, va_ref, qb_ref, kb_ref, vb_ref, kna_ref, knb_ref):
    x = x_ref[0]
    h = _rms_rows(x) * (g_ref[...] * (1.0 + sc_ref[0])) + sh_ref[0]
    hb = h.astype(BF16)
    tm = x.shape[0]
    lane = lax.broadcasted_iota(jnp.int32, (tm, LANES), 1)
    up64 = (lane % 64) < 32
    up32 = (lane % 32) < 16
    ones_blk = _head_block_ones(MXU_N)
    q_scale = HEAD_DIM ** -0.5

    def rotary_1d(z):
        return z * ta_ref[:, :LANES] + _swap_halves(z, 32, up64) * ta_ref[:, LANES:]

    def rotary_axial(z, t_ref):
        return z * t_ref[:, :LANES] + _swap_halves(z, 16, up32) * t_ref[:, LANES:]

    def per_chunk(f, zz):
        return jnp.concatenate([f(zz[:, :LANES]), f(zz[:, LANES:])], axis=1)

    lower = lane < HEAD_DIM

    def store_pair(ref, first, second, z, fill, masked):
        if masked:
            lo = jnp.where(lower, z, jnp.where(lane == ONES_LANE_LOWER, fill, 0.0))
            up = jnp.where(lower, jnp.where(lane == ONES_LANE_UPPER, fill, 0.0), z)
        else:
            lo = jnp.where(lane == ONES_LANE_LOWER, fill, z)
            up = jnp.where(lane == ONES_LANE_UPPER, fill, z)
        ref[0, first] = lo.astype(BF16)
        ref[0, second] = up.astype(BF16)

    def store_queries(ref, heads, y):
        sq = HEAD_DIM * _head_mean_square(y, ones_blk, precise=False) + 1e-30
        norm = sq * lax.rsqrt(sq)
        for s, (first, second) in enumerate(heads):
            cols = slice(s * LANES, (s + 1) * LANES)
            store_pair(ref, first, second, y[:, cols],
                       pltpu.roll(norm[:, cols], HEAD_DIM, axis=1), True)

    def store_keys(ref, kn_ref, heads, y):
        ms = jnp.max(_head_mean_square(y, ones_blk, precise=False), axis=0, keepdims=True)
        for s, (first, second) in enumerate(heads):
            cols = slice(s * LANES, (s + 1) * LANES)
            store_pair(ref, first, second, y[:, cols], 1.0, False)
            kn_ref[0, first // 2] = jnp.broadcast_to(ms[:, cols], (8, LANES))

    projections = [jnp.dot(hb, w_ref[:, j * MXU_N:(j + 1) * MXU_N], preferred_element_type=F32)
                   for j in range(AB_IN // MXU_N)]
    half_b = B_Q_HEADS // 2
    for j, zz in enumerate(projections):
        pairs = [(4 * (j % 2), 4 * (j % 2) + 1), (4 * (j % 2) + 2, 4 * (j % 2) + 3)]
        if j < 2:
            store_queries(qa_ref, pairs, per_chunk(rotary_1d, zz) * q_scale)
        elif j < 4:
            store_keys(ka_ref, kna_ref, pairs, per_chunk(rotary_1d, zz))
        elif j < 6:
            for s, (first, second) in enumerate(pairs):
                store_pair(va_ref, first, second, zz[:, s * LANES:(s + 1) * LANES], 1.0, False)
        elif j < 8:
            y = per_chunk(lambda z: rotary_axial(z, tq_ref), zz)
            y = y * lax.rsqrt(_head_mean_square(zz, ones_blk, precise=False) + EPS)
            store_queries(qb_ref, [(2 * (j - 6), 2 * (j - 6) + half_b),
                                   (2 * (j - 6) + 1, 2 * (j - 6) + 1 + half_b)], y)
        else:
            ms = _head_mean_square(zz, ones_blk, precise=False)[:, :LANES]
            kb = rotary_axial(zz[:, :LANES], tk_ref) * lax.rsqrt(ms + EPS)
            store_keys(kb_ref, knb_ref, [(0, 1)], jnp.concatenate([kb, kb], axis=1))
            store_pair(vb_ref, 0, 1, zz[:, LANES:], 1.0, False)


def _inproj_even(x, g, sc, sh, w, ta, tq, tk, tm):
    b, s, d = x.shape
    n = w.shape[1]
    tab = pl.BlockSpec((tm, 2 * LANES), lambda i, bb: (i, 0))
    mod = pl.BlockSpec((1, 1, d), lambda i, bb: (bb, 0, 0))
    heads = (A_HEADS, A_HEADS, A_HEADS, B_Q_HEADS, B_KV_HEADS, B_KV_HEADS)
    return pl.pallas_call(
        _inproj_even_kernel,
        grid=(s // tm, b),
        in_specs=[pl.BlockSpec((1, tm, d), lambda i, bb: (bb, i, 0)),
                  pl.BlockSpec((1, d), lambda i, bb: (0, 0)),
                  mod, mod,
                  pl.BlockSpec((d, n), lambda i, bb: (0, 0)),
                  tab, tab, tab],
        out_specs=([pl.BlockSpec((1, nh, tm, LANES), lambda i, bb: (bb, 0, i, 0)) for nh in heads]
                   + [pl.BlockSpec((1, nc, 8, LANES), lambda i, bb: (bb, 0, i, 0))
                      for nc in (A_HEADS // 2, B_KV_HEADS // 2)]),
        out_shape=([jax.ShapeDtypeStruct((b, nh, s, LANES), BF16) for nh in heads]
                   + [jax.ShapeDtypeStruct((b, nc, 8 * (s // tm), LANES), F32)
                      for nc in (A_HEADS // 2, B_KV_HEADS // 2)]),
        name="inproj_even",
        compiler_params=_cparams("parallel", "parallel"),
    )(x, g, sc, sh, w, ta, tq, tk)


def _rope_tables(s, qk_g):
    pos = jnp.arange(s)

    def angles(p, dim):
        inv_freq = ROPE_THETA ** (-jnp.arange(0, dim, 2, dtype=F32) / dim)
        return p.astype(F32)[:, None] * inv_freq[None, :]

    a1 = angles(pos, HEAD_DIM)
    cos1, sin1 = jnp.cos(a1), jnp.sin(a1)
    direct = jnp.concatenate([cos1, cos1], axis=-1)
    swapped = jnp.concatenate([-sin1, sin1], axis=-1)
    ta = jnp.concatenate([direct, direct, swapped, swapped], axis=-1)

    ar = angles(pos // GRID_W, HEAD_DIM // 2)
    ac = angles(pos % GRID_W, HEAD_DIM // 2)
    cosb = jnp.concatenate([jnp.cos(ar), jnp.cos(ar), jnp.cos(ac), jnp.cos(ac)], axis=-1)
    sinb = jnp.concatenate([-jnp.sin(ar), jnp.sin(ar), -jnp.sin(ac), jnp.sin(ac)], axis=-1)

    def gained(gain, scale):
        gain = gain.astype(F32)
        gswap = gain.reshape(2, 2, HEAD_DIM // 4)[:, ::-1].reshape(HEAD_DIM)
        direct = cosb * gain * scale
        swapped = sinb * gswap * scale
        return jnp.concatenate([direct, direct, swapped, swapped], axis=-1)

    return ta, gained(qk_g[0], HEAD_DIM ** -0.5), gained(qk_g[1], 1.0)


LOG2E = 1.4426950408889634
LN2 = 0.6931471805599453
BOUND_SLACK = 1.02
MIN_DENOM = 2.0 ** -60


def _ones_lane(upper):
    return ONES_LANE_UPPER if upper else ONES_LANE_LOWER


def _query_coefficients(kn, upper, extra=0.0):
    ms = jnp.max(kn, axis=0, keepdims=True)
    kmax = jnp.sqrt(HEAD_DIM * (ms[:, :1] + ms[:, HEAD_DIM:HEAD_DIM + 1]) + 1.0)
    lane = lax.broadcasted_iota(jnp.int32, (1, LANES), 1)
    ones = lane == jnp.where(upper, ONES_LANE_UPPER, ONES_LANE_LOWER)
    return (jnp.where(ones, -LOG2E * BOUND_SLACK * kmax, LOG2E),
            jnp.where(ones, -LOG2E * extra, 0.0))


def _bounded_queries(q, coefficients):
    scale, offset = coefficients
    return (q.astype(F32) * scale + offset).astype(BF16)


def _online_step(q, k, v, bias, m_ref, acc_ref):
    s = _dot_nt(q, k)
    if bias is not None:
        s = s + bias
    m_prev = m_ref[...]
    m_new = jnp.maximum(m_prev, jnp.max(s, axis=-1, keepdims=True))
    p = jnp.exp(s - m_new).astype(BF16)
    acc_ref[...] = jnp.exp(m_prev - m_new) * acc_ref[...] + jnp.dot(p, v, preferred_element_type=F32)
    m_ref[...] = m_new


def _online_init(m_ref, acc_ref):
    m_ref[...] = jnp.full(m_ref.shape, -jnp.inf, F32)
    acc_ref[...] = jnp.zeros(acc_ref.shape, F32)


def _denominators(acc, upper):
    lane = _ones_lane(upper)
    return acc[:, lane:lane + 1]


def _normalised_pair(acc_lower, acc_upper):
    lane = lax.broadcasted_iota(jnp.int32, acc_lower.shape, 1)
    return jnp.where(lane < HEAD_DIM, acc_lower / _denominators(acc_lower, False),
                     acc_upper / _denominators(acc_upper, True))


def _gqa_kernel(q_ref, k_ref, v_ref, kn_ref, o_ref, qa_ref, acc_ref, m_ref, *, tk, rsub):
    nh, tq, dp = q_ref.shape[1:]
    gsz = nh // 2
    grows = gsz * tq
    nkv = k_ref.shape[2] // tk

    def group_q(g):
        return q_ref[0, g * gsz:(g + 1) * gsz].reshape(grows, dp)

    for g in range(2):
        qa_ref[g * grows:(g + 1) * grows, :] = _bounded_queries(
            group_q(g), _query_coefficients(kn_ref[0, 0], g == 1))
    acc_ref[...] = jnp.zeros(acc_ref.shape, F32)

    def bounded(j, carry):
        keys = pl.ds(pl.multiple_of(j * tk, tk), tk)
        subtiles = [(g, pl.ds(g * grows + r * rsub, rsub))
                    for g in range(2) for r in range(grows // rsub)]
        scores = [_dot_nt(qa_ref[rr, :], k_ref[0, g, keys, :]) for g, rr in subtiles]
        weights = [jnp.exp2(s).astype(BF16) for s in scores]
        for (g, rr), p in zip(subtiles, weights):
            acc_ref[rr, :] += jnp.dot(p, v_ref[0, g, keys, :], preferred_element_type=F32)
        return carry

    lax.fori_loop(0, nkv, bounded, 0, unroll=2 if nkv % 2 == 0 else 1)

    smallest =jnp.minimum(jnp.min(_denominators(acc_ref[:grows, :], False)),
                           jnp.min(_denominators(acc_ref[grows:, :], True)))

    @pl.when(smallest < MIN_DENOM)
    def _():
        _online_init(m_ref, acc_ref)

        def online(j, carry):
            keys = pl.ds(pl.multiple_of(j * tk, tk), tk)
            for g in range(2):
                rr = pl.ds(g * grows, grows)
                _online_step(group_q(g), k_ref[0, g, keys, :], v_ref[0, g, keys, :], None,
                             m_ref.at[rr, :], acc_ref.at[rr, :])
            return carry

        lax.fori_loop(0, nkv, online, 0)

    for j in range(gsz):
        lo = acc_ref[j * tq:(j + 1) * tq, :]
        up = acc_ref[grows + j * tq:grows + (j + 1) * tq, :]
        o_ref[0, :, j * LANES:(j + 1) * LANES] = _normalised_pair(lo, up).astype(o_ref.dtype)


def _gqa_attention(q, k, v, kn, tq, tk):
    b, hq, s, dp = q.shape
    assert k.shape[1] == 2 and hq % 2 == 0
    rows = hq * tq
    kv = pl.BlockSpec((1, 2, s, dp), lambda bb, i: (bb, 0, 0, 0))
    return pl.pallas_call(
        functools.partial(_gqa_kernel, tk=tk, rsub=min(256, rows // 2)),
        grid=(b, s // tq),
        in_specs=[pl.BlockSpec((1, hq, tq, dp), lambda bb, i: (bb, 0, i, 0)), kv, kv,
                  pl.BlockSpec((1,) + kn.shape[1:], lambda bb, i: (bb, 0, 0, 0))],
        out_specs=pl.BlockSpec((1, tq, hq * HEAD_DIM), lambda bb, i: (bb, i, 0)),
        out_shape=jax.ShapeDtypeStruct((b, s, hq * HEAD_DIM), BF16),
        scratch_shapes=[pltpu.VMEM((rows, dp), BF16), pltpu.VMEM((rows, dp), F32),
                        pltpu.VMEM((rows, 1), F32)],
        name="gqa_attention",
        compiler_params=_cparams("parallel", "parallel"),
    )(q, k, v, kn)


def _band_reach(t):
    return -(-max(w // 2 for w, _ in A_PATTERNS) // t)


def _band_bias(t, nproc):
    offs = range(-(nproc - 1), nproc)
    a = np.arange(t)[:, None]
    bcol = np.arange(t)[None, :]
    out = np.zeros((len(offs), t, t), np.float64)
    for n, d in enumerate(offs):
        j = d * t + bcol - a
        mult = np.zeros((t, t), np.float64)
        for window, dil in A_PATTERNS:
            mult += ((j % dil) == 0) & (np.abs(j) <= window // 2)
        out[n] = np.where(mult > 0, LOG2E * np.log(np.maximum(mult, 1.0)), NEG)
    return jnp.asarray(out, F32)


def _band_kernel(q_ref, k_ref, v_ref, kn_ref, bias_ref, o_ref, qa_ref, acc_ref, m_ref, *,
                 nproc, rsub):
    t, dp = q_ref.shape[2:]
    nblk = k_ref.shape[2] // t
    i = pl.program_id(2)
    first = jnp.clip(i - _band_reach(t), 0, nblk - nproc)
    max_bias = float(np.log(len(A_PATTERNS)))

    for h in range(2):
        qa_ref[h] = _bounded_queries(q_ref[0, h],
                                     _query_coefficients(kn_ref[0, 0], h == 1, max_bias))
    acc_ref[...] = jnp.zeros(acc_ref.shape, F32)
    subtiles = [(h, pl.ds(r * rsub, rsub)) for h in range(2) for r in range(t // rsub)]

    def key_block(n):
        kb = first + n
        return pl.ds(pl.multiple_of(kb * t, t), t), bias_ref.at[kb - i + nproc - 1]

    def bounded(n, carry):
        keys, bias = key_block(n)
        scores = [_dot_nt(qa_ref[h, rr, :], k_ref[0, h, keys, :]) + bias[rr, :]
                  for h, rr in subtiles]
        weights = [jnp.exp2(s).astype(BF16) for s in scores]
        for (h, rr), p in zip(subtiles, weights):
            acc_ref[h, rr, :] += jnp.dot(p, v_ref[0, h, keys, :], preferred_element_type=F32)
        return carry

    lax.fori_loop(0, nproc, bounded, 0, unroll=True)

    smallest = jnp.minimum(jnp.min(_denominators(acc_ref[0], False)),
                           jnp.min(_denominators(acc_ref[1], True)))

    @pl.when(smallest < MIN_DENOM)
    def _():
        for h in range(2):
            acc = acc_ref.at[h]
            _online_init(m_ref, acc)

            def online(n, carry):
                keys, bias = key_block(n)
                _online_step(q_ref[0, h], k_ref[0, h, keys, :], v_ref[0, h, keys, :],
                             LN2 * bias[...], m_ref, acc)
                return carry

            lax.fori_loop(0, nproc, online, 0)

    o_ref[0] = _normalised_pair(acc_ref[0], acc_ref[1]).astype(o_ref.dtype)


def _dilated_attention(q, k, v, kn, t):
    b, h, s, dp = q.shape
    nproc = min(2 * _band_reach(t) + 1, s // t)
    bias = _band_bias(t, nproc)
    kv = pl.BlockSpec((1, 2, s, dp), lambda bb, hh, i: (bb, hh, 0, 0))
    return pl.pallas_call(
        functools.partial(_band_kernel, nproc=nproc, rsub=min(256, t)),
        grid=(b, h // 2, s // t),
        in_specs=[pl.BlockSpec((1, 2, t, dp), lambda bb, hh, i: (bb, hh, i, 0)), kv, kv,
                  pl.BlockSpec((1, 1) + kn.shape[2:], lambda bb, hh, i: (bb, hh, 0, 0)),
                  pl.BlockSpec(bias.shape, lambda bb, hh, i: (0, 0, 0),
                               pipeline_mode=pl.Buffered(1))],
        out_specs=pl.BlockSpec((1, t, LANES), lambda bb, hh, i: (bb, i, hh)),
        out_shape=jax.ShapeDtypeStruct((b, s, h * HEAD_DIM), BF16),
        scratch_shapes=[pltpu.VMEM((2, t, dp), BF16), pltpu.VMEM((2, t, dp), F32),
                        pltpu.VMEM((t, 1), F32)],
        name="dilated_attention",
        compiler_params=_cparams("parallel", "parallel", "parallel"),
    )(q, k, v, kn, bias)


def _residual_update(x, y, gate, norm_g):
    return x + gate * (_rms_rows(y) * norm_g)


def _mixer_output_even(oa_ref, ob_ref):
    return oa_ref[0], ob_ref[0]


def _mixer_output_odd(hf_ref, hb_ref, og_ref, d_ref, mhg_ref):
    ones_blk = _head_block_ones()
    parts = []
    for c in range(HALF_W // LANES):
        cols = slice(c * LANES, (c + 1) * LANES)
        hs = hf_ref[0, :, cols] + hb_ref[0, :, cols]
        hn = hs * lax.rsqrt(_head_mean_square(hs, ones_blk) + EPS) * mhg_ref[:, cols]
        parts.append((hn * jax.nn.sigmoid(og_ref[0, :, cols].astype(F32))).astype(BF16))
    return jnp.concatenate(parts, axis=-1), d_ref[0]


def _layer_tail_kernel(*refs, odd, tf):
    n_mix = 5 if odd else 2
    mix = (_mixer_output_odd if odd else _mixer_output_even)(*refs[:n_mix])
    (wo_ref, x_ref, g1_ref, ng1_ref, ng2_ref, sc_ref, sh_ref, w1_ref, w2_ref, g2_ref, ng3_ref,
     o_ref, a_ref) = refs[n_mix:]
    y = jnp.dot(mix[0], wo_ref[:HALF_W, :], preferred_element_type=F32)
    y = y + jnp.dot(mix[1], wo_ref[HALF_W:, :], preferred_element_type=F32)
    x1 = _residual_update(x_ref[0], y, g1_ref[0], ng1_ref[...])
    h = (_rms_rows(x1) * (ng2_ref[...] * (1.0 + sc_ref[0])) + sh_ref[0]).astype(BF16)
    for kc in range(w1_ref.shape[1] // tf):
        cols = slice(kc * tf, (kc + 1) * tf)
        a = jnp.maximum(jnp.dot(h, w1_ref[:, cols], preferred_element_type=F32), 0.0)
        a_ref[:, cols] = (a * a).astype(BF16)
    y2 = jnp.dot(a_ref[...], w2_ref[...], preferred_element_type=F32)
    o_ref[0] = _residual_update(x1, y2, g2_ref[0], ng3_ref[...])


def _layer_tail(mix_inputs, mix_specs, odd, wo, x, g1, ng1, ng2, sc, sh, w1, w2, g2, ng3, tm, tf):
    b, s, d = x.shape
    f = w1.shape[1]
    full = pl.BlockSpec((1, tm, d), lambda bb, i: (bb, i, 0))
    mod = pl.BlockSpec((1, 1, d), lambda bb, i: (bb, 0, 0))
    vec = pl.BlockSpec((1, d), lambda bb, i: (0, 0))

    def resident(shape):
        return pl.BlockSpec(shape, lambda bb, i: (0, 0), pipeline_mode=pl.Buffered(1))

    return pl.pallas_call(
        functools.partial(_layer_tail_kernel, odd=odd, tf=tf),
        grid=(b, s // tm),
        in_specs=mix_specs + [resident((d, d)), full, mod, vec, vec, mod, mod,
                              resident((d, f)), resident((f, d)), mod, vec],
        out_specs=full,
        out_shape=jax.ShapeDtypeStruct((b, s, d), F32),
        scratch_shapes=[pltpu.VMEM((tm, f), BF16)],
        name="layer_tail_odd" if odd else "layer_tail_even",
        compiler_params=_cparams("parallel", "parallel"),
    )(*mix_inputs, wo, x, g1, ng1, ng2, sc, sh, w1, w2, g2, ng3)


def _layer_tail_even(oa, ob, *rest, tm, tf):
    half = pl.BlockSpec((1, tm, HALF_W), lambda bb, i: (bb, i, 0))
    return _layer_tail([oa, ob], [half, half], False, *rest, tm, tf)


def _layer_tail_odd(hf, hb, z, dout, mhg, *rest, tm, tf):
    half = pl.BlockSpec((1, tm, HALF_W), lambda bb, i: (bb, i, 0))
    og = pl.BlockSpec((1, tm, HALF_W), lambda bb, i: (bb, i, 3))
    return _layer_tail([hf, hb, z, dout, mhg],
                       [half, half, og, half, pl.BlockSpec((1, HALF_W), lambda bb, i: (0, 0))],
                       True, *rest, tm, tf)


def _inproj_odd_kernel(x_ref, g_ref, sc_ref, sh_ref, w_ref, gb_ref, z_ref, gates_ref):
    x = x_ref[0]
    h = _rms_rows(x) * (g_ref[...] * (1.0 + sc_ref[0])) + sh_ref[0]
    hb = h.astype(BF16)
    k_scale = HEAD_DIM ** -0.5
    for j in range(CD_MAIN // MXU_N):
        cols = slice(j * MXU_N, (j + 1) * MXU_N)
        zz = jnp.dot(hb, w_ref[:, cols], preferred_element_type=F32)
        if HALF_W <= j * MXU_N < 2 * HALF_W:
            zz = zz * k_scale
        z_ref[0, :, cols] = zz.astype(BF16)
    gates_ref[0] = jnp.dot(hb, w_ref[:, CD_MAIN:], preferred_element_type=F32) + gb_ref[...]


def _inproj_odd(x, g, sc, sh, w, gb, tm):
    b, s, d = x.shape
    n = w.shape[1]
    mod = pl.BlockSpec((1, 1, d), lambda i, bb: (bb, 0, 0))
    return pl.pallas_call(
        _inproj_odd_kernel,
        grid=(s // tm, b),
        in_specs=[pl.BlockSpec((1, tm, d), lambda i, bb: (bb, i, 0)),
                  pl.BlockSpec((1, d), lambda i, bb: (0, 0)),
                  mod, mod,
                  pl.BlockSpec((d, n), lambda i, bb: (0, 0)),
                  pl.BlockSpec((1, GATE_W), lambda i, bb: (0, 0))],
        out_specs=[pl.BlockSpec((1, tm, CD_MAIN), lambda i, bb: (bb, i, 0)),
                   pl.BlockSpec((1, tm, GATE_W), lambda i, bb: (bb, i, 0))],
        out_shape=[jax.ShapeDtypeStruct((b, s, CD_MAIN), BF16),
                   jax.ShapeDtypeStruct((b, s, GATE_W), F32)],
        name="inproj_odd",
        compiler_params=_cparams("parallel", "parallel"),
    )(x, g, sc, sh, w, gb)


def _gate_lane(rev, head):
    return (C_HEADS if rev else 0) + head


def _mlstm_constants(rev, lc):
    wq = QUAD * HEAD_DIM
    t = np.arange(lc)
    sees = (t[None, :] >= t[:, None]) if rev else (t[None, :] <= t[:, None])
    nq = C_HEADS // QUAD
    sel_s = np.zeros((nq, LANES, QUAD * lc), np.float32)
    sel_v = np.zeros((nq, LANES, wq), np.float32)
    for qd in range(nq):
        for hh in range(QUAD):
            lane = _gate_lane(rev, QUAD * qd + hh)
            sel_s[qd, lane, hh * lc:(hh + 1) * lc] = 1.0
            sel_v[qd, lane, hh * HEAD_DIM:(hh + 1) * HEAD_DIM] = 1.0
    head_of_row = np.arange(QUAD * lc) // lc
    head_of_col = np.arange(wq) // HEAD_DIM
    return dict(
        tri=jnp.asarray(sees, BF16),
        sel_s=jnp.asarray(sel_s, BF16), sel_v=jnp.asarray(sel_v, BF16),
        ones_bd=jnp.asarray(np.transpose(sel_s, (0, 2, 1)), BF16),
        row_head=jnp.asarray(head_of_row[:, None] == head_of_col[None, :], BF16),
        diag=jnp.asarray(head_of_col[:, None] == head_of_col[None, :], F32),
        causal=jnp.asarray(np.tile(sees, (1, QUAD)), F32))


def _scan_max(x, rev):
    n = x.shape[0]
    row = lax.broadcasted_iota(jnp.int32, x.shape, 0)
    sh = 1
    while sh < n:
        if rev:
            x = jnp.maximum(x, jnp.where(row < n - sh, pltpu.roll(x, n - sh, axis=0), -jnp.inf))
        else:
            x = jnp.maximum(x, jnp.where(row >= sh, pltpu.roll(x, sh, axis=0), -jnp.inf))
        sh *= 2
    return x


def _mlstm_kernel(*refs, lc):
    nd = len(_MLSTM_DIR_CONSTS)
    blocks = (refs[0:4], refs[4:8])
    rowhead_ref, diag_ref = refs[8:10]
    consts = (refs[10:10 + nd], refs[10 + nd:10 + 2 * nd])
    outs = refs[10 + 2 * nd:12 + 2 * nd]
    c_sc, n_sc, m_sc = refs[12 + 2 * nd:]

    @pl.when(pl.program_id(1) == 0)
    def _():
        c_sc[...] = jnp.zeros(c_sc.shape, F32)
        n_sc[...] = jnp.zeros(n_sc.shape, F32)
        m_sc[...] = jnp.zeros(m_sc.shape, F32)

    nch = outs[0].shape[1] // lc

    wq = QUAD * HEAD_DIM
    edges = (lc - 1, 0)
    jobs = [(rev, qd) for rev in (0, 1) for qd in range(C_HEADS // QUAD)]
    dot = functools.partial(jnp.dot, preferred_element_type=F32)

    def chunk(ci, carry):
        rows = (pl.ds(pl.multiple_of(ci * lc, lc), lc),
                pl.ds(pl.multiple_of((nch - 1 - ci) * lc, lc), lc))
        gate = []
        for rev in (0, 1):
            g_ref = blocks[rev][3]
            tri_ref = consts[rev][0]
            gi = g_ref[0, rows[rev], :LANES]
            bcum = _split_dot_rhs(tri_ref[...], _log_sigmoid(g_ref[0, rows[rev], LANES:]))
            a = gi - bcum
            m_prev = m_sc[rev]
            mt_b = jnp.maximum(m_prev, _scan_max(a, bool(rev))).astype(BF16)
            mt = mt_b.astype(F32)
            w_b = jnp.exp(m_prev - mt).astype(BF16)
            b_all = bcum[edges[rev]:edges[rev] + 1, :]
            log_w = b_all + a
            m_new = jnp.maximum(b_all + m_prev, jnp.max(log_w, axis=0, keepdims=True))
            gate.append(dict(
                mt_b=mt_b, w_b=w_b, w=w_b.astype(F32),
                floor=jnp.exp(-(bcum + mt)),
                wk_b=jnp.exp(log_w - m_new).astype(BF16),
                decay=jnp.broadcast_to(jnp.exp(b_all + m_prev - m_new), (8, LANES)),
                a_t=a.T))
            m_sc[rev] = m_new

        row_head = rowhead_ref[...]

        def operand(rev, qd, j):
            return blocks[rev][j][0, rows[rev], qd * wq:(qd + 1) * wq]

        def sel_v(rev, qd):
            return consts[rev][2][qd]

        scores, log_e = [], []
        for rev, qd in jobs:
            kbd = jnp.concatenate([operand(rev, qd, 1)] * QUAD, axis=0) * row_head
            scores.append(_dot_nt(operand(rev, qd, 0), kbd))
            a_rows = jnp.concatenate(
                [jnp.broadcast_to(gate[rev]["a_t"][ln:ln + 1, :], (lc, lc))
                 for ln in (_gate_lane(bool(rev), QUAD * qd + hh) for hh in range(QUAD))], axis=1)
            log_e.append(jnp.where(consts[rev][4][...] > 0.0,
                                   a_rows - dot(gate[rev]["mt_b"], consts[rev][1][qd]), NEG))
        probs = [(s * jnp.exp(e)).astype(BF16) for s, e in zip(scores, log_e)]

        pvs, qcs = [], []
        for (rev, qd), p in zip(jobs, probs):
            v = operand(rev, qd, 2)
            vaug = jnp.concatenate(
                [jnp.concatenate([v] * QUAD, axis=0) * row_head, consts[rev][3][qd]], axis=1)
            pvs.append(dot(p, vaug))
            caug = jnp.concatenate(
                [c_sc[rev, qd].astype(BF16),
                 (sel_v(rev, qd).astype(F32) * n_sc[rev, qd]).astype(BF16)], axis=0)
            qcs.append(_dot_nt(operand(rev, qd, 0), caug))

        for (rev, qd), pv, qc in zip(jobs, pvs, qcs):
            g = gate[rev]
            den = pv[:, wq:] + g["w"] * qc[:, wq:]
            scale_b = (1.0 / jnp.maximum(jnp.abs(den), g["floor"])).astype(BF16)
            outs[rev][0, rows[rev], qd * wq:(qd + 1) * wq] = (
                dot(scale_b, sel_v(rev, qd)) * (pv[:, :wq] + dot(g["w_b"], sel_v(rev, qd)) * qc[:, :wq]))

        for rev, qd in jobs:
            g = gate[rev]
            k = operand(rev, qd, 1)
            wk_v = dot(g["wk_b"], sel_v(rev, qd))
            dec_v = _split_dot(g["decay"], sel_v(rev, qd))[:1]
            vw = (wk_v * operand(rev, qd, 2).astype(F32)).astype(BF16)
            c_sc[rev, qd] = dec_v * c_sc[rev, qd] + diag_ref[...] * _dot_tn(vw, k)
            n_sc[rev, qd] = (dec_v * n_sc[rev, qd]
                             + jnp.sum(wk_v * k.astype(F32), axis=0, keepdims=True))
        return carry

    lax.fori_loop(0, nch, chunk, 0, unroll=2 if nch % 2 == 0 else 1)


_MLSTM_DIR_CONSTS = ("tri", "sel_s", "sel_v", "ones_bd", "causal")


def _mlstm(z, gates, tb, lc):
    b, s, _ = z.shape
    nblk = s // tb
    position = (lambda i: i, lambda i: nblk - 1 - i)

    def col(rev, j):
        return pl.BlockSpec((1, tb, HALF_W), lambda bb, i: (bb, position[rev](i), j))

    def block_specs(rev):
        return [col(rev, 0), col(rev, 1), col(rev, 2),
                pl.BlockSpec((1, tb, GATE_W), lambda bb, i: (bb, position[rev](i), 0))]

    def const_spec(a):
        return pl.BlockSpec(a.shape, lambda bb, i: (0,) * a.ndim)

    consts = (_mlstm_constants(False, lc), _mlstm_constants(True, lc))
    shared = [consts[0]["row_head"], consts[0]["diag"]]
    per_dir = [consts[rev][n] for rev in (0, 1) for n in _MLSTM_DIR_CONSTS]
    nq = C_HEADS // QUAD
    wq = QUAD * HEAD_DIM
    return pl.pallas_call(
        functools.partial(_mlstm_kernel, lc=lc),
        grid=(b, nblk),
        in_specs=block_specs(0) + block_specs(1) + [const_spec(a) for a in shared + per_dir],
        out_specs=[col(0, 0), col(1, 0)],
        out_shape=[jax.ShapeDtypeStruct((b, s, HALF_W), F32)] * 2,
        scratch_shapes=[pltpu.VMEM((2, nq, wq, wq), F32), pltpu.VMEM((2, nq, 1, wq), F32),
                        pltpu.VMEM((2, 1, LANES), F32)],
        name="mlstm",
        compiler_params=_cparams("parallel", "arbitrary"),
    )(z, z, z, gates, z, z, z, gates, *shared, *per_dir)


def _sgu_kernel(u_ref, v_ref, lng_ref, w_ref, b_ref, o_ref):
    tb = u_ref.shape[1]
    group = lax.broadcasted_iota(jnp.int32, (D_CHUNK, HALF_W), 1) // HEAD_DIM
    for ch in range(tb // D_CHUNK):
        rows = slice(ch * D_CHUNK, (ch + 1) * D_CHUNK)
        u = jax.nn.gelu(u_ref[0, rows, :].astype(F32))
        v = jax.nn.gelu(v_ref[0, rows, :].astype(F32))
        dv = v - jnp.mean(v, axis=-1, keepdims=True)
        vn = dv * lax.rsqrt(jnp.mean(dv * dv, axis=-1, keepdims=True) + EPS) * lng_ref[...]
        full = jnp.dot(w_ref[...], vn.astype(BF16), preferred_element_type=F32)
        sg = b_ref[...]
        for g in range(D_GROUPS):
            sg = sg + jnp.where(group == g, full[g * D_CHUNK:(g + 1) * D_CHUNK, :], 0.0)
        o_ref[0, rows, :] = (u * sg).astype(BF16)


def _sgu(z, lng, w, bexp, tb):
    b, s, _ = z.shape
    col = lambda j: pl.BlockSpec((1, tb, HALF_W), lambda bb, i: (bb, i, j))
    return pl.pallas_call(
        _sgu_kernel,
        grid=(b, s // tb),
        in_specs=[col(4), col(5),
                  pl.BlockSpec((1, HALF_W), lambda bb, i: (0, 0)),
                  pl.BlockSpec((D_GROUPS * D_CHUNK, D_CHUNK), lambda bb, i: (0, 0)),
                  pl.BlockSpec((D_CHUNK, HALF_W), lambda bb, i: (0, 0))],
        out_specs=col(0),
        out_shape=jax.ShapeDtypeStruct((b, s, HALF_W), BF16),
        name="sgu",
        compiler_params=_cparams("parallel", "parallel"),
    )(z, z, lng, w, bexp)


def _tiles(s):
    return dict(tm=min(512, s), tf=1024, tq=min(128, s), tk=min(512, s), tband=min(512, s),
                tb_mlstm=min(1024, s), lc=128, tb_sgu=min(512, s))


def _gqa_head_order():
    half = B_Q_HEADS // 2
    return [h for j in range(half) for h in (j, half + j)]


def _trunk(x, c, p):
    b, s, d = x.shape
    t = _tiles(s)
    mod = _adaln(c, p["w_ada"], p["b_ada"])
    ta, tq, tk = _rope_tables(s, p["qk_norm_g"])
    for layer in range(2):
        m = mod[:, layer * 6 * d:(layer + 1) * 6 * d].reshape(b, 6, 1, d)
        sh1, sc1, g1, sh2, sc2, g2 = (m[:, j] for j in range(6))
        ng = p["norm_g"][layer].reshape(4, 1, d)
        tail = (x, g1, ng[1], ng[2], sc2, sh2, p["w_ff1"][layer], p["w_ff2"][layer], g2, ng[3])
        if layer == 0:
            qa, ka, va, qb, kb, vb, kna, knb = _inproj_even(x, ng[0], sc1, sh1, p["w_in_ab"],
                                                            ta, tq, tk, t["tm"])
            oa = _dilated_attention(qa, ka, va, kna, t["tband"])
            ob = _gqa_attention(qb, kb, vb, knb, t["tq"], t["tk"])
            x = _layer_tail_even(oa, ob, p["w_out_ab"], *tail, tm=t["tm"], tf=t["tf"])
        else:
            z, gates = _inproj_odd(x, ng[0], sc1, sh1, p["w_in_cd"], p["gate_bias"], t["tm"])
            hf, hb = _mlstm(z, gates, t["tb_mlstm"], t["lc"])
            dout = _sgu(z, p["sg_norm_g"], p["w_spatial"], p["b_spatial"], t["tb_sgu"])
            x = _layer_tail_odd(hf, hb, z, dout, p["mh_norm_g"], p["w_out_cd"], *tail,
                                tm=t["tm"], tf=t["tf"])
    return x


def _prepare_params(w_in_ab, w_out_ab, qk_norm_g, w_in_cd, w_out_cd, gate_bias, mh_norm_g,
                    sg_norm_g, w_spatial, b_spatial, w_ada, b_ada, norm_g, w_ff1, w_ff2):
    depth, d, _ = w_ada.shape
    wcd = w_in_cd[0]
    gate_lo = 4 * HALF_W
    gate_hi = gate_lo + N_GATES

    def gate_tiles(g):
        i_fw, f_fw, i_bw, f_bw = (g[:, j * C_HEADS:(j + 1) * C_HEADS] for j in range(4))
        pad = jnp.zeros((g.shape[0], LANES - 2 * C_HEADS), g.dtype)
        return jnp.concatenate([i_fw, i_bw, pad, f_fw, f_bw, pad], axis=1)

    wcd = jnp.concatenate([wcd[:, :gate_lo], wcd[:, gate_hi:],
                           gate_tiles(wcd[:, gate_lo:gate_hi])], axis=1)
    gb = gate_tiles(gate_bias[0].reshape(1, N_GATES).astype(F32))
    bexp = jnp.repeat(b_spatial[0].astype(F32).T, HEAD_DIM, axis=1)

    qb_lo = 3 * HALF_W
    order = np.asarray(_gqa_head_order())
    head_cols = (order[:, None] * HEAD_DIM + np.arange(HEAD_DIM)[None, :]).reshape(-1)
    wab = w_in_ab[0]
    wab = jnp.concatenate([wab[:, :qb_lo], wab[:, qb_lo + head_cols], wab[:, qb_lo + HALF_W:]],
                          axis=1)
    wout_ab = w_out_ab[0]
    wout_ab = jnp.concatenate([wout_ab[:HALF_W], wout_ab[HALF_W + head_cols]], axis=0)
    return dict(
        w_ada=jnp.concatenate([w_ada[l] for l in range(depth)], axis=1).astype(BF16),
        b_ada=b_ada.reshape(1, -1).astype(F32),
        w_in_ab=wab.astype(BF16), w_out_ab=wout_ab.astype(BF16),
        qk_norm_g=qk_norm_g[0],
        w_in_cd=wcd.astype(BF16), w_out_cd=w_out_cd[0].astype(BF16), gate_bias=gb,
        mh_norm_g=mh_norm_g[0].reshape(1, HALF_W).astype(F32),
        sg_norm_g=sg_norm_g[0].reshape(1, HALF_W).astype(F32),
        w_spatial=w_spatial[0].reshape(D_GROUPS * D_CHUNK, D_CHUNK).astype(BF16),
        b_spatial=bexp,
        norm_g=norm_g.astype(F32), w_ff1=w_ff1.astype(BF16), w_ff2=w_ff2.astype(BF16))


def kernel(x_prompt, x_sample, c_prompt, c_sample, w_in_ab, w_out_ab, qk_norm_g, w_in_cd, w_out_cd, gate_bias, mh_norm_g, sg_norm_g, w_spatial, b_spatial, w_ada, b_ada, norm_g, w_ff1, w_ff2):
    p = _prepare_params(w_in_ab, w_out_ab, qk_norm_g, w_in_cd, w_out_cd, gate_bias, mh_norm_g,
                        sg_norm_g, w_spatial, b_spatial, w_ada, b_ada, norm_g, w_ff1, w_ff2)
    return (_trunk(x_prompt, c_prompt, p), _trunk(x_sample, c_sample, p))
```

```python
import functools

import numpy as np
import jax
import jax.numpy as jnp
from jax import lax
from jax.experimental import pallas as pl
from jax.experimental.pallas import tpu as pltpu

F32 = jnp.float32
BF16 = jnp.bfloat16

D_MODEL = 1024
HEAD_DIM = 64
GRID_W = 64
ROPE_THETA = 10000.0
EPS = 1e-6
A_HEADS = 8
A_PATTERNS = ((128, 1), (512, 4), (2048, 16))
B_Q_HEADS = 8
B_KV_HEADS = 2
C_HEADS = 8
D_GROUPS = 8
D_CHUNK = 128
HALF_W = D_MODEL // 2
AB_IN = 3 * HALF_W + HALF_W + 2 * B_KV_HEADS * HEAD_DIM
CD_MAIN = 6 * HALF_W
N_GATES = 4 * C_HEADS
GATE_W = 256
QUAD = 4
ONES_LANE_LOWER = HEAD_DIM
ONES_LANE_UPPER = 0

LANES = 128
MXU_N = 256
V7X_VMEM_BYTES = 64 * 1024 * 1024
VMEM_LIMIT = 3 * V7X_VMEM_BYTES // 4

NEG = -1e30


def _cparams(*sem):
    return pltpu.CompilerParams(dimension_semantics=sem, vmem_limit_bytes=VMEM_LIMIT)


def _rms_rows(x):
    return x * lax.rsqrt(jnp.mean(x * x, axis=-1, keepdims=True) + EPS)


def _split_dot(a, b):
    hi = a.astype(BF16)
    r1 = a - hi.astype(F32)
    mid = r1.astype(BF16)
    lo = (r1 - mid.astype(F32)).astype(BF16)
    dot = functools.partial(jnp.dot, preferred_element_type=F32)
    return dot(hi, b) + dot(mid, b) + dot(lo, b)


def _split_dot_rhs(b, a):
    hi = a.astype(BF16)
    r1 = a - hi.astype(F32)
    mid = r1.astype(BF16)
    lo = (r1 - mid.astype(F32)).astype(BF16)
    dot = functools.partial(jnp.dot, preferred_element_type=F32)
    return dot(b, hi) + dot(b, mid) + dot(b, lo)


def _head_mean_square(z, ones_blk, precise=True):
    z2 = z * z
    hi = z2.astype(BF16)
    total = jnp.dot(hi, ones_blk, preferred_element_type=F32)
    if precise:
        lo = (z2 - hi.astype(F32)).astype(BF16)
        total = total + jnp.dot(lo, ones_blk, preferred_element_type=F32)
    return total * (1.0 / HEAD_DIM)


def _head_block_ones(width=LANES):
    r = lax.broadcasted_iota(jnp.int32, (width, width), 0) // HEAD_DIM
    c = lax.broadcasted_iota(jnp.int32, (width, width), 1) // HEAD_DIM
    return (r == c).astype(BF16)


def _log_sigmoid(x):
    return jnp.minimum(x, 0.0) - jnp.log1p(jnp.exp(-jnp.abs(x)))


def _dot_nt(a, b):
    return lax.dot_general(a, b, (((1,), (1,)), ((), ())), preferred_element_type=F32)


def _dot_tn(a, b):
    return lax.dot_general(a, b, (((0,), (0,)), ((), ())), preferred_element_type=F32)


def _adaln_kernel(c_ref, w_ref, b_ref, o_ref):
    c = c_ref[...]
    ca = c * jax.nn.sigmoid(c)
    o_ref[...] = jnp.dot(ca.astype(BF16), w_ref[...], preferred_element_type=F32) + b_ref[...]


def _adaln(c, w, b):
    nb, d = c.shape
    n = w.shape[1]
    tn = 2048
    return pl.pallas_call(
        _adaln_kernel,
        grid=(n // tn,),
        in_specs=[pl.BlockSpec((nb, d), lambda j: (0, 0)),
                  pl.BlockSpec((d, tn), lambda j: (0, j)),
                  pl.BlockSpec((1, tn), lambda j: (0, j))],
        out_specs=pl.BlockSpec((nb, tn), lambda j: (0, j)),
        out_shape=jax.ShapeDtypeStruct((nb, n), F32),
        name="adaln",
        compiler_params=_cparams("parallel"),
    )(c, w, b)


def _swap_halves(z, half, upper):
    return jnp.where(upper, pltpu.roll(z, LANES - half, axis=1), pltpu.roll(z, half, axis=1))


def _inproj_even_kernel(x_ref, g_ref, sc_ref, sh_ref, w_ref, ta_ref, tq_ref, tk_ref,
                        qa_ref, ka_ref, va_ref, qb_ref, kb_ref, vb_ref, kna_ref, knb_ref):
    x = x_ref[0]
    h = _rms_rows(x) * (g_ref[...] * (1.0 + sc_ref[0])) + sh_ref[0]
    hb = h.astype(BF16)
    tm = x.shape[0]
    lane = lax.broadcasted_iota(jnp.int32, (tm, LANES), 1)
    up64 = (lane % 64) < 32
    up32 = (lane % 32) < 16
    ones_blk = _head_block_ones(MXU_N)
    q_scale = HEAD_DIM ** -0.5

    def rotary_1d(z):
        return z * ta_ref[:, :LANES] + _swap_halves(z, 32, up64) * ta_ref[:, LANES:]

    def rotary_axial(z, t_ref):
        return z * t_ref[:, :LANES] + _swap_halves(z, 16, up32) * t_ref[:, LANES:]

    def per_chunk(f, zz):
        return jnp.concatenate([f(zz[:, :LANES]), f(zz[:, LANES:])], axis=1)

    lower = lane < HEAD_DIM

    def store_pair(ref, first, second, z, fill, masked):
        if masked:
            lo = jnp.where(lower, z, jnp.where(lane == ONES_LANE_LOWER, fill, 0.0))
            up = jnp.where(lower, jnp.where(lane == ONES_LANE_UPPER, fill, 0.0), z)
        else:
            lo = jnp.where(lane == ONES_LANE_LOWER, fill, z)
            up = jnp.where(lane == ONES_LANE_UPPER, fill, z)
        ref[0, first] = lo.astype(BF16)
        ref[0, second] = up.astype(BF16)

    def store_queries(ref, heads, y):
        sq = HEAD_DIM * _head_mean_square(y, ones_blk, precise=False) + 1e-30
        norm = sq * lax.rsqrt(sq)
        for s, (first, second) in enumerate(heads):
            cols = slice(s * LANES, (s + 1) * LANES)
            store_pair(ref, first, second, y[:, cols],
                       pltpu.roll(norm[:, cols], HEAD_DIM, axis=1), True)

    def store_keys(ref, kn_ref, heads, y):
        ms = jnp.max(_head_mean_square(y, ones_blk, precise=False), axis=0, keepdims=True)
        for s, (first, second) in enumerate(heads):
            cols = slice(s * LANES, (s + 1) * LANES)
            store_pair(ref, first, second, y[:, cols], 1.0, False)
            kn_ref[0, first // 2] = jnp.broadcast_to(ms[:, cols], (8, LANES))

    projections = [jnp.dot(hb, w_ref[:, j * MXU_N:(j + 1) * MXU_N], preferred_element_type=F32)
                   for j in range(AB_IN // MXU_N)]
    half_b = B_Q_HEADS // 2
    for j, zz in enumerate(projections):
        pairs = [(4 * (j % 2), 4 * (j % 2) + 1), (4 * (j % 2) + 2, 4 * (j % 2) + 3)]
        if j < 2:
            store_queries(qa_ref, pairs, per_chunk(rotary_1d, zz) * q_scale)
        elif j < 4:
            store_keys(ka_ref, kna_ref, pairs, per_chunk(rotary_1d, zz))
        elif j < 6:
            for s, (first, second) in enumerate(pairs):
                store_pair(va_ref, first, second, zz[:, s * LANES:(s + 1) * LANES], 1.0, False)
        elif j < 8:
            y = per_chunk(lambda z: rotary_axial(z, tq_ref), zz)
            y = y * lax.rsqrt(_head_mean_square(zz, ones_blk, precise=False) + EPS)
            store_queries(qb_ref, [(2 * (j - 6), 2 * (j - 6) + half_b),
                                   (2 * (j - 6) + 1, 2 * (j - 6) + 1 + half_b)], y)
        else:
            ms = _head_mean_square(zz, ones_blk, precise=False)[:, :LANES]
            kb = rotary_axial(zz[:, :LANES], tk_ref) * lax.rsqrt(ms + EPS)
            store_keys(kb_ref, knb_ref, [(0, 1)], jnp.concatenate([kb, kb], axis=1))
            store_pair(vb_ref, 0, 1, zz[:, LANES:], 1.0, False)


def _inproj_even(x, g, sc, sh, w, ta, tq, tk, tm):
    b, s, d = x.shape
    n = w.shape[1]
    tab = pl.BlockSpec((tm, 2 * LANES), lambda i, bb: (i, 0))
    mod = pl.BlockSpec((1, 1, d), lambda i, bb: (bb, 0, 0))
    heads = (A_HEADS, A_HEADS, A_HEADS, B_Q_HEADS, B_KV_HEADS, B_KV_HEADS)
    return pl.pallas_call(
        _inproj_even_kernel,
        grid=(s // tm, b),
        in_specs=[pl.BlockSpec((1, tm, d), lambda i, bb: (bb, i, 0)),
                  pl.BlockSpec((1, d), lambda i, bb: (0, 0)),
                  mod, mod,
                  pl.BlockSpec((d, n), lambda i, bb: (0, 0)),
                  tab, tab, tab],
        out_specs=([pl.BlockSpec((1, nh, tm, LANES), lambda i, bb: (bb, 0, i, 0)) for nh in heads]
                   + [pl.BlockSpec((1, nc, 8, LANES), lambda i, bb: (bb, 0, i, 0))
                      for nc in (A_HEADS // 2, B_KV_HEADS // 2)]),
        out_shape=([jax.ShapeDtypeStruct((b, nh, s, LANES), BF16) for nh in heads]
                   + [jax.ShapeDtypeStruct((b, nc, 8 * (s // tm), LANES), F32)
                      for nc in (A_HEADS // 2, B_KV_HEADS // 2)]),
        name="inproj_even",
        compiler_params=_cparams("parallel", "parallel"),
    )(x, g, sc, sh, w, ta, tq, tk)


def _rope_tables(s, qk_g):
    pos = jnp.arange(s)

    def angles(p, dim):
        inv_freq = ROPE_THETA ** (-jnp.arange(0, dim, 2, dtype=F32) / dim)
        return p.astype(F32)[:, None] * inv_freq[None, :]

    a1 = angles(pos, HEAD_DIM)
    cos1, sin1 = jnp.cos(a1), jnp.sin(a1)
    direct = jnp.concatenate([cos1, cos1], axis=-1)
    swapped = jnp.concatenate([-sin1, sin1], axis=-1)
    ta = jnp.concatenate([direct, direct, swapped, swapped], axis=-1)

    ar = angles(pos // GRID_W, HEAD_DIM // 2)
    ac = angles(pos % GRID_W, HEAD_DIM // 2)
    cosb = jnp.concatenate([jnp.cos(ar), jnp.cos(ar), jnp.cos(ac), jnp.cos(ac)], axis=-1)
    sinb = jnp.concatenate([-jnp.sin(ar), jnp.sin(ar), -jnp.sin(ac), jnp.sin(ac)], axis=-1)

    def gained(gain, scale):
        gain = gain.astype(F32)
        gswap = gain.reshape(2, 2, HEAD_DIM // 4)[:, ::-1].reshape(HEAD_DIM)
        direct = cosb * gain * scale
        swapped = sinb * gswap * scale
        return jnp.concatenate([direct, direct, swapped, swapped], axis=-1)

    return ta, gained(qk_g[0], HEAD_DIM ** -0.5), gained(qk_g[1], 1.0)


LOG2E = 1.4426950408889634
LN2 = 0.6931471805599453
BOUND_SLACK = 1.02
MIN_DENOM = 2.0 ** -60


def _ones_lane(upper):
    return ONES_LANE_UPPER if upper else ONES_LANE_LOWER


def _query_coefficients(kn, upper, extra=0.0):
    ms = jnp.max(kn, axis=0, keepdims=True)
    kmax = jnp.sqrt(HEAD_DIM * (ms[:, :1] + ms[:, HEAD_DIM:HEAD_DIM + 1]) + 1.0)
    lane = lax.broadcasted_iota(jnp.int32, (1, LANES), 1)
    ones = lane == _ones_lane(upper)
    return (jnp.where(ones, -LOG2E * BOUND_SLACK * kmax, LOG2E),
            jnp.where(ones, -LOG2E * extra, 0.0))


def _bounded_queries(q, coefficients):
    scale, offset = coefficients
    return (q.astype(F32) * scale + offset).astype(BF16)


def _online_step(q, k, v, bias, m_ref, acc_ref):
    s = _dot_nt(q, k)
    if bias is not None:
        s = s + bias
    m_prev = m_ref[...]
    m_new = jnp.maximum(m_prev, jnp.max(s, axis=-1, keepdims=True))
    p = jnp.exp(s - m_new).astype(BF16)
    acc_ref[...] = jnp.exp(m_prev - m_new) * acc_ref[...] + jnp.dot(p, v, preferred_element_type=F32)
    m_ref[...] = m_new


def _online_init(m_ref, acc_ref):
    m_ref[...] = jnp.full(m_ref.shape, -jnp.inf, F32)
    acc_ref[...] = jnp.zeros(acc_ref.shape, F32)


def _denominators(acc, upper):
    lane = _ones_lane(upper)
    return acc[:, lane:lane + 1]


def _normalised_pair(acc_lower, acc_upper):
    lane = lax.broadcasted_iota(jnp.int32, acc_lower.shape, 1)
    return jnp.where(lane < HEAD_DIM, acc_lower / _denominators(acc_lower, False),
                     acc_upper / _denominators(acc_upper, True))


def _gqa_kernel(q_ref, k_ref, v_ref, kn_ref, o_ref, qa_ref, acc_ref, m_ref, *, tk, rsub):
    nh, tq, dp = q_ref.shape[1:]
    gsz = nh // 2
    grows = gsz * tq
    nkv = k_ref.shape[2] // tk

    def group_q(g):
        return q_ref[0, g * gsz:(g + 1) * gsz].reshape(grows, dp)

    for g in range(2):
        qa_ref[g * grows:(g + 1) * grows, :] = _bounded_queries(
            group_q(g), _query_coefficients(kn_ref[0, 0], g == 1))
    acc_ref[...] = jnp.zeros(acc_ref.shape, F32)

    def bounded(j, carry):
        keys = pl.ds(pl.multiple_of(j * tk, tk), tk)
        subtiles = [(g, pl.ds(g * grows + r * rsub, rsub))
                    for g in range(2) for r in range(grows // rsub)]
        scores = [_dot_nt(qa_ref[rr, :], k_ref[0, g, keys, :]) for g, rr in subtiles]
        weights = [jnp.exp2(s).astype(BF16) for s in scores]
        for (g, rr), p in zip(subtiles, weights):
            acc_ref[rr, :] += jnp.dot(p, v_ref[0, g, keys, :], preferred_element_type=F32)
        return carry

    lax.fori_loop(0, nkv, bounded, 0, unroll=2 if nkv % 2 == 0 else 1)

    smallest =jnp.minimum(jnp.min(_denominators(acc_ref[:grows, :], False)),
                           jnp.min(_denominators(acc_ref[grows:, :], True)))

    @pl.when(smallest < MIN_DENOM)
    def _():
        _online_init(m_ref, acc_ref)

        def online(j, carry):
            keys = pl.ds(pl.multiple_of(j * tk, tk), tk)
            for g in range(2):
                rr = pl.ds(g * grows, grows)
                _online_step(group_q(g), k_ref[0, g, keys, :], v_ref[0, g, keys, :], None,
                             m_ref.at[rr, :], acc_ref.at[rr, :])
            return carry

        lax.fori_loop(0, nkv, online, 0)

    for j in range(gsz):
        lo = acc_ref[j * tq:(j + 1) * tq, :]
        up = acc_ref[grows + j * tq:grows + (j + 1) * tq, :]
        o_ref[0, :, j * LANES:(j + 1) * LANES] = _normalised_pair(lo, up).astype(o_ref.dtype)


def _gqa_attention(q, k, v, kn, tq, tk):
    b, hq, s, dp = q.shape
    assert k.shape[1] == 2 and hq % 2 == 0
    rows = hq * tq
    kv = pl.BlockSpec((1, 2, s, dp), lambda bb, i: (bb, 0, 0, 0))
    return pl.pallas_call(
        functools.partial(_gqa_kernel, tk=tk, rsub=min(256, rows // 2)),
        grid=(b, s // tq),
        in_specs=[pl.BlockSpec((1, hq, tq, dp), lambda bb, i: (bb, 0, i, 0)), kv, kv,
                  pl.BlockSpec((1,) + kn.shape[1:], lambda bb, i: (bb, 0, 0, 0))],
        out_specs=pl.BlockSpec((1, tq, hq * HEAD_DIM), lambda bb, i: (bb, i, 0)),
        out_shape=jax.ShapeDtypeStruct((b, s, hq * HEAD_DIM), BF16),
        scratch_shapes=[pltpu.VMEM((rows, dp), BF16), pltpu.VMEM((rows, dp), F32),
                        pltpu.VMEM((rows, 1), F32)],
        name="gqa_attention",
        compiler_params=_cparams("parallel", "parallel"),
    )(q, k, v, kn)


def _band_reach(t):
    return -(-max(w // 2 for w, _ in A_PATTERNS) // t)


def _band_bias(t, nproc):
    offs = range(-(nproc - 1), nproc)
    a = np.arange(t)[:, None]
    bcol = np.arange(t)[None, :]
    out = np.zeros((len(offs), t, t), np.float64)
    for n, d in enumerate(offs):
        j = d * t + bcol - a
        mult = np.zeros((t, t), np.float64)
        for window, dil in A_PATTERNS:
            mult += ((j % dil) == 0) & (np.abs(j) <= window // 2)
        out[n] = np.where(mult > 0, LOG2E * np.log(np.maximum(mult, 1.0)), NEG)
    return jnp.asarray(out, F32)


def _band_kernel(q_ref, k_ref, v_ref, kn_ref, bias_ref, o_ref, qa_ref, acc_ref, m_ref, *,
                 nproc, rsub):
    t, dp = q_ref.shape[2:]
    nblk = k_ref.shape[2] // t
    i = pl.program_id(2)
    first = jnp.clip(i - _band_reach(t), 0, nblk - nproc)
    max_bias = float(np.log(len(A_PATTERNS)))

    for h in range(2):
        qa_ref[h] = _bounded_queries(q_ref[0, h],
                                     _query_coefficients(kn_ref[0, 0], h == 1, max_bias))
    acc_ref[...] = jnp.zeros(acc_ref.shape, F32)
    subtiles = [(h, pl.ds(r * rsub, rsub)) for h in range(2) for r in range(t // rsub)]

    def key_block(n):
        kb = first + n
        return pl.ds(pl.multiple_of(kb * t, t), t), bias_ref.at[kb - i + nproc - 1]

    def bounded(n, carry):
        keys, bias = key_block(n)
        scores = [_dot_nt(qa_ref[h, rr, :], k_ref[0, h, keys, :]) + bias[rr, :]
                  for h, rr in subtiles]
        weights = [jnp.exp2(s).astype(BF16) for s in scores]
        for (h, rr), p in zip(subtiles, weights):
            acc_ref[h, rr, :] += jnp.dot(p, v_ref[0, h, keys, :], preferred_element_type=F32)
        return carry

    lax.fori_loop(0, nproc, bounded, 0, unroll=True)

    smallest = jnp.minimum(jnp.min(_denominators(acc_ref[0], False)),
                           jnp.min(_denominators(acc_ref[1], True)))

    @pl.when(smallest < MIN_DENOM)
    def _():
        for h in range(2):
            acc = acc_ref.at[h]
            _online_init(m_ref, acc)

            def online(n, carry):
                keys, bias = key_block(n)
                _online_step(q_ref[0, h], k_ref[0, h, keys, :], v_ref[0, h, keys, :],
                             LN2 * bias[...], m_ref, acc)
                return carry

            lax.fori_loop(0, nproc, online, 0)

    o_ref[0] = _normalised_pair(acc_ref[0], acc_ref[1]).astype(o_ref.dtype)


def _dilated_attention(q, k, v, kn, t):
    b, h, s, dp = q.shape
    nproc = min(2 * _band_reach(t) + 1, s // t)
    bias = _band_bias(t, nproc)
    kv = pl.BlockSpec((1, 2, s, dp), lambda bb, hh, i: (bb, hh, 0, 0))
    return pl.pallas_call(
        functools.partial(_band_kernel, nproc=nproc, rsub=min(256, t)),
        grid=(b, h // 2, s // t),
        in_specs=[pl.BlockSpec((1, 2, t, dp), lambda bb, hh, i: (bb, hh, i, 0)), kv, kv,
                  pl.BlockSpec((1, 1) + kn.shape[2:], lambda bb, hh, i: (bb, hh, 0, 0)),
                  pl.BlockSpec(bias.shape, lambda bb, hh, i: (0, 0, 0),
                               pipeline_mode=pl.Buffered(1))],
        out_specs=pl.BlockSpec((1, t, LANES), lambda bb, hh, i: (bb, i, hh)),
        out_shape=jax.ShapeDtypeStruct((b, s, h * HEAD_DIM), BF16),
        scratch_shapes=[pltpu.VMEM((2, t, dp), BF16), pltpu.VMEM((2, t, dp), F32),
                        pltpu.VMEM((t, 1), F32)],
        name="dilated_attention",
        compiler_params=_cparams("parallel", "parallel", "parallel"),
    )(q, k, v, kn, bias)


def _residual_update(x, y, gate, norm_g):
    return x + gate * (_rms_rows(y) * norm_g)


def _mixer_output_even(oa_ref, ob_ref):
    return oa_ref[0], ob_ref[0]


def _mixer_output_odd(hf_ref, hb_ref, og_ref, d_ref, mhg_ref):
    ones_blk = _head_block_ones()
    parts = []
    for c in range(HALF_W // LANES):
        cols = slice(c * LANES, (c + 1) * LANES)
        hs = hf_ref[0, :, cols] + hb_ref[0, :, cols]
        hn = hs * lax.rsqrt(_head_mean_square(hs, ones_blk) + EPS) * mhg_ref[:, cols]
        parts.append((hn * jax.nn.sigmoid(og_ref[0, :, cols].astype(F32))).astype(BF16))
    return jnp.concatenate(parts, axis=-1), d_ref[0]


def _layer_tail_kernel(*refs, odd, tf):
    n_mix = 5 if odd else 2
    mix = (_mixer_output_odd if odd else _mixer_output_even)(*refs[:n_mix])
    (wo_ref, x_ref, g1_ref, ng1_ref, ng2_ref, sc_ref, sh_ref, w1_ref, w2_ref, g2_ref, ng3_ref,
     o_ref, a_ref) = refs[n_mix:]
    y = jnp.dot(mix[0], wo_ref[:HALF_W, :], preferred_element_type=F32)
    y = y + jnp.dot(mix[1], wo_ref[HALF_W:, :], preferred_element_type=F32)
    x1 = _residual_update(x_ref[0], y, g1_ref[0], ng1_ref[...])
    h = (_rms_rows(x1) * (ng2_ref[...] * (1.0 + sc_ref[0])) + sh_ref[0]).astype(BF16)
    for kc in range(w1_ref.shape[1] // tf):
        cols = slice(kc * tf, (kc + 1) * tf)
        a = jnp.maximum(jnp.dot(h, w1_ref[:, cols], preferred_element_type=F32), 0.0)
        a_ref[:, cols] = (a * a).astype(BF16)
    y2 = jnp.dot(a_ref[...], w2_ref[...], preferred_element_type=F32)
    o_ref[0] = _residual_update(x1, y2, g2_ref[0], ng3_ref[...])


def _layer_tail(mix_inputs, mix_specs, odd, wo, x, g1, ng1, ng2, sc, sh, w1, w2, g2, ng3, tm, tf):
    b, s, d = x.shape
    f = w1.shape[1]
    full = pl.BlockSpec((1, tm, d), lambda bb, i: (bb, i, 0))
    mod = pl.BlockSpec((1, 1, d), lambda bb, i: (bb, 0, 0))
    vec = pl.BlockSpec((1, d), lambda bb, i: (0, 0))

    def resident(shape):
        return pl.BlockSpec(shape, lambda bb, i: (0, 0), pipeline_mode=pl.Buffered(1))

    return pl.pallas_call(
        functools.partial(_layer_tail_kernel, odd=odd, tf=tf),
        grid=(b, s // tm),
        in_specs=mix_specs + [resident((d, d)), full, mod, vec, vec, mod, mod,
                              resident((d, f)), resident((f, d)), mod, vec],
        out_specs=full,
        out_shape=jax.ShapeDtypeStruct((b, s, d), F32),
        scratch_shapes=[pltpu.VMEM((tm, f), BF16)],
        name="layer_tail_odd" if odd else "layer_tail_even",
        compiler_params=_cparams("parallel", "parallel"),
    )(*mix_inputs, wo, x, g1, ng1, ng2, sc, sh, w1, w2, g2, ng3)


def _layer_tail_even(oa, ob, *rest, tm, tf):
    half = pl.BlockSpec((1, tm, HALF_W), lambda bb, i: (bb, i, 0))
    return _layer_tail([oa, ob], [half, half], False, *rest, tm, tf)


def _layer_tail_odd(hf, hb, z, dout, mhg, *rest, tm, tf):
    half = pl.BlockSpec((1, tm, HALF_W), lambda bb, i: (bb, i, 0))
    og = pl.BlockSpec((1, tm, HALF_W), lambda bb, i: (bb, i, 3))
    return _layer_tail([hf, hb, z, dout, mhg],
                       [half, half, og, half, pl.BlockSpec((1, HALF_W), lambda bb, i: (0, 0))],
                       True, *rest, tm, tf)


def _inproj_odd_kernel(x_ref, g_ref, sc_ref, sh_ref, w_ref, gb_ref, z_ref, gates_ref):
    x = x_ref[0]
    h = _rms_rows(x) * (g_ref[...] * (1.0 + sc_ref[0])) + sh_ref[0]
    hb = h.astype(BF16)
    k_scale = HEAD_DIM ** -0.5
    projections = [jnp.dot(hb, w_ref[:, j * MXU_N:(j + 1) * MXU_N], preferred_element_type=F32)
                   for j in range(CD_MAIN // MXU_N)]
    gates_ref[0] = jnp.dot(hb, w_ref[:, CD_MAIN:], preferred_element_type=F32) + gb_ref[...]
    for j, zz in enumerate(projections):
        if HALF_W <= j * MXU_N < 2 * HALF_W:
            zz = zz * k_scale
        z_ref[0, :, j * MXU_N:(j + 1) * MXU_N] = zz.astype(BF16)


def _inproj_odd(x, g, sc, sh, w, gb, tm):
    b, s, d = x.shape
    n = w.shape[1]
    mod = pl.BlockSpec((1, 1, d), lambda i, bb: (bb, 0, 0))
    return pl.pallas_call(
        _inproj_odd_kernel,
        grid=(s // tm, b),
        in_specs=[pl.BlockSpec((1, tm, d), lambda i, bb: (bb, i, 0)),
                  pl.BlockSpec((1, d), lambda i, bb: (0, 0)),
                  mod, mod,
                  pl.BlockSpec((d, n), lambda i, bb: (0, 0)),
                  pl.BlockSpec((1, GATE_W), lambda i, bb: (0, 0))],
        out_specs=[pl.BlockSpec((1, tm, CD_MAIN), lambda i, bb: (bb, i, 0)),
                   pl.BlockSpec((1, tm, GATE_W), lambda i, bb: (bb, i, 0))],
        out_shape=[jax.ShapeDtypeStruct((b, s, CD_MAIN), BF16),
                   jax.ShapeDtypeStruct((b, s, GATE_W), F32)],
        name="inproj_odd",
        compiler_params=_cparams("parallel", "parallel"),
    )(x, g, sc, sh, w, gb)


def _gate_lane(rev, head):
    return (C_HEADS if rev else 0) + head


def _mlstm_constants(rev, lc):
    wq = QUAD * HEAD_DIM
    t = np.arange(lc)
    sees = (t[None, :] >= t[:, None]) if rev else (t[None, :] <= t[:, None])
    nq = C_HEADS // QUAD
    sel_s = np.zeros((nq, LANES, QUAD * lc), np.float32)
    sel_v = np.zeros((nq, LANES, wq), np.float32)
    for qd in range(nq):
        for hh in range(QUAD):
            lane = _gate_lane(rev, QUAD * qd + hh)
            sel_s[qd, lane, hh * lc:(hh + 1) * lc] = 1.0
            sel_v[qd, lane, hh * HEAD_DIM:(hh + 1) * HEAD_DIM] = 1.0
    head_of_row = np.arange(QUAD * lc) // lc
    head_of_col = np.arange(wq) // HEAD_DIM
    return dict(
        tri=jnp.asarray(sees, BF16),
        sel_s=jnp.asarray(sel_s, BF16), sel_v=jnp.asarray(sel_v, BF16),
        ones_bd=jnp.asarray(np.transpose(sel_s, (0, 2, 1)), BF16),
        row_head=jnp.asarray(head_of_row[:, None] == head_of_col[None, :], BF16),
        diag=jnp.asarray(head_of_col[:, None] == head_of_col[None, :], F32),
        causal=jnp.asarray(np.tile(sees, (1, QUAD)), F32))


def _scan_max(x, rev):
    n = x.shape[0]
    row = lax.broadcasted_iota(jnp.int32, x.shape, 0)
    sh = 1
    while sh < n:
        if rev:
            x = jnp.maximum(x, jnp.where(row < n - sh, pltpu.roll(x, n - sh, axis=0), -jnp.inf))
        else:
            x = jnp.maximum(x, jnp.where(row >= sh, pltpu.roll(x, sh, axis=0), -jnp.inf))
        sh *= 2
    return x


def _mlstm_kernel(*refs, lc):
    nd = len(_MLSTM_DIR_CONSTS)
    blocks = (refs[0:4], refs[4:8])
    rowhead_ref, diag_ref = refs[8:10]
    consts = (refs[10:10 + nd], refs[10 + nd:10 + 2 * nd])
    outs = refs[10 + 2 * nd:12 + 2 * nd]
    c_sc, n_sc, m_sc = refs[12 + 2 * nd:]

    @pl.when(pl.program_id(1) == 0)
    def _():
        c_sc[...] = jnp.zeros(c_sc.shape, F32)
        n_sc[...] = jnp.zeros(n_sc.shape, F32)
        m_sc[...] = jnp.zeros(m_sc.shape, F32)

    nch = outs[0].shape[1] // lc

    wq = QUAD * HEAD_DIM
    edges = (lc - 1, 0)
    jobs = [(rev, qd) for rev in (0, 1) for qd in range(C_HEADS // QUAD)]
    dot = functools.partial(jnp.dot, preferred_element_type=F32)

    def chunk(ci, carry):
        rows = (pl.ds(pl.multiple_of(ci * lc, lc), lc),
                pl.ds(pl.multiple_of((nch - 1 - ci) * lc, lc), lc))
        gate = []
        for rev in (0, 1):
            g_ref = blocks[rev][3]
            tri_ref = consts[rev][0]
            gi = g_ref[0, rows[rev], :LANES]
            bcum = _split_dot_rhs(tri_ref[...], _log_sigmoid(g_ref[0, rows[rev], LANES:]))
            a = gi - bcum
            m_prev = m_sc[rev]
            mt_b = jnp.maximum(m_prev, _scan_max(a, bool(rev))).astype(BF16)
            mt = mt_b.astype(F32)
            w_b = jnp.exp(m_prev - mt).astype(BF16)
            b_all = bcum[edges[rev]:edges[rev] + 1, :]
            log_w = b_all + a
            m_new = jnp.maximum(b_all + m_prev, jnp.max(log_w, axis=0, keepdims=True))
            gate.append(dict(
                mt_b=mt_b, w_b=w_b, w=w_b.astype(F32),
                floor=jnp.exp(-(bcum + mt)),
                wk_b=jnp.exp(log_w - m_new).astype(BF16),
                decay=jnp.broadcast_to(jnp.exp(b_all + m_prev - m_new), (8, LANES)),
                a_t=a.T))
            m_sc[rev] = m_new

        row_head = rowhead_ref[...]

        def operand(rev, qd, j):
            return blocks[rev][j][0, rows[rev], qd * wq:(qd + 1) * wq]

        def sel_v(rev, qd):
            return consts[rev][2][qd]

        scores, log_e = [], []
        for rev, qd in jobs:
            kbd = jnp.concatenate([operand(rev, qd, 1)] * QUAD, axis=0) * row_head
            scores.append(_dot_nt(operand(rev, qd, 0), kbd))
            a_rows = jnp.concatenate(
                [jnp.broadcast_to(gate[rev]["a_t"][ln:ln + 1, :], (lc, lc))
                 for ln in (_gate_lane(bool(rev), QUAD * qd + hh) for hh in range(QUAD))], axis=1)
            log_e.append(jnp.where(consts[rev][4][...] > 0.0,
                                   a_rows - dot(gate[rev]["mt_b"], consts[rev][1][qd]), NEG))
        probs = [(s * jnp.exp(e)).astype(BF16) for s, e in zip(scores, log_e)]

        pvs, qcs = [], []
        for (rev, qd), p in zip(jobs, probs):
            v = operand(rev, qd, 2)
            vaug = jnp.concatenate(
                [jnp.concatenate([v] * QUAD, axis=0) * row_head, consts[rev][3][qd]], axis=1)
            pvs.append(dot(p, vaug))
            caug = jnp.concatenate(
                [c_sc[rev, qd].astype(BF16),
                 (sel_v(rev, qd).astype(F32) * n_sc[rev, qd]).astype(BF16)], axis=0)
            qcs.append(_dot_nt(operand(rev, qd, 0), caug))

        for (rev, qd), pv, qc in zip(jobs, pvs, qcs):
            g = gate[rev]
            den = pv[:, wq:] + g["w"] * qc[:, wq:]
            scale_b = (1.0 / jnp.maximum(jnp.abs(den), g["floor"])).astype(BF16)
            outs[rev][0, rows[rev], qd * wq:(qd + 1) * wq] = (
                dot(scale_b, sel_v(rev, qd)) * (pv[:, :wq] + dot(g["w_b"], sel_v(rev, qd)) * qc[:, :wq]))

        for rev, qd in jobs:
            g = gate[rev]
            k = operand(rev, qd, 1)
            wk_v = dot(g["wk_b"], sel_v(rev, qd))
            dec_v = _split_dot(g["decay"], sel_v(rev, qd))[:1]
            vw = (wk_v * operand(rev, qd, 2).astype(F32)).astype(BF16)
            c_sc[rev, qd] = dec_v * c_sc[rev, qd] + diag_ref[...] * _dot_tn(vw, k)
            n_sc[rev, qd] = (dec_v * n_sc[rev, qd]
                             + jnp.sum(wk_v * k.astype(F32), axis=0, keepdims=True))
        return carry

    lax.fori_loop(0, nch, chunk, 0, unroll=2 if nch % 2 == 0 else 1)


_MLSTM_DIR_CONSTS = ("tri", "sel_s", "sel_v", "ones_bd", "causal")


def _mlstm(z, gates, tb, lc):
    b, s, _ = z.shape
    nblk = s // tb
    position = (lambda i: i, lambda i: nblk - 1 - i)

    def col(rev, j):
        return pl.BlockSpec((1, tb, HALF_W), lambda bb, i: (bb, position[rev](i), j))

    def block_specs(rev):
        return [col(rev, 0), col(rev, 1), col(rev, 2),
                pl.BlockSpec((1, tb, GATE_W), lambda bb, i: (bb, position[rev](i), 0))]

    def const_spec(a):
        return pl.BlockSpec(a.shape, lambda bb, i: (0,) * a.ndim)

    consts = (_mlstm_constants(False, lc), _mlstm_constants(True, lc))
    shared = [consts[0]["row_head"], consts[0]["diag"]]
    per_dir = [consts[rev][n] for rev in (0, 1) for n in _MLSTM_DIR_CONSTS]
    nq = C_HEADS // QUAD
    wq = QUAD * HEAD_DIM
    return pl.pallas_call(
        functools.partial(_mlstm_kernel, lc=lc),
        grid=(b, nblk),
        in_specs=block_specs(0) + block_specs(1) + [const_spec(a) for a in shared + per_dir],
        out_specs=[col(0, 0), col(1, 0)],
        out_shape=[jax.ShapeDtypeStruct((b, s, HALF_W), F32)] * 2,
        scratch_shapes=[pltpu.VMEM((2, nq, wq, wq), F32), pltpu.VMEM((2, nq, 1, wq), F32),
                        pltpu.VMEM((2, 1, LANES), F32)],
        name="mlstm",
        compiler_params=_cparams("parallel", "arbitrary"),
    )(z, z, z, gates, z, z, z, gates, *shared, *per_dir)


def _sgu_kernel(u_ref, v_ref, lng_ref, w_ref, b_ref, o_ref):
    tb = u_ref.shape[1]
    group = lax.broadcasted_iota(jnp.int32, (D_CHUNK, HALF_W), 1) // HEAD_DIM
    for ch in range(tb // D_CHUNK):
        rows = slice(ch * D_CHUNK, (ch + 1) * D_CHUNK)
        u = jax.nn.gelu(u_ref[0, rows, :].astype(F32))
        v = jax.nn.gelu(v_ref[0, rows, :].astype(F32))
        dv = v - jnp.mean(v, axis=-1, keepdims=True)
        vn = dv * lax.rsqrt(jnp.mean(dv * dv, axis=-1, keepdims=True) + EPS) * lng_ref[...]
        full = jnp.dot(w_ref[...], vn.astype(BF16), preferred_element_type=F32)
        sg = b_ref[...]
        for g in range(D_GROUPS):
            sg = sg + jnp.where(group == g, full[g * D_CHUNK:(g + 1) * D_CHUNK, :], 0.0)
        o_ref[0, rows, :] = (u * sg).astype(BF16)


def _sgu(z, lng, w, bexp, tb):
    b, s, _ = z.shape
    col = lambda j: pl.BlockSpec((1, tb, HALF_W), lambda bb, i: (bb, i, j))
    return pl.pallas_call(
        _sgu_kernel,
        grid=(b, s // tb),
        in_specs=[col(4), col(5),
                  pl.BlockSpec((1, HALF_W), lambda bb, i: (0, 0)),
                  pl.BlockSpec((D_GROUPS * D_CHUNK, D_CHUNK), lambda bb, i: (0, 0)),
                  pl.BlockSpec((D_CHUNK, HALF_W), lambda bb, i: (0, 0))],
        out_specs=col(0),
        out_shape=jax.ShapeDtypeStruct((b, s, HALF_W), BF16),
        name="sgu",
        compiler_params=_cparams("parallel", "parallel"),
    )(z, z, lng, w, bexp)


def _tiles(s):
    return dict(tm=min(512, s), tf=1024, tq=min(256, s), tk=min(1024, s), tband=min(512, s),
                tb_mlstm=min(1024, s), lc=128, tb_sgu=min(512, s))


def _gqa_head_order():
    half = B_Q_HEADS // 2
    return [h for j in range(half) for h in (j, half + j)]


def _trunk(x, c, p):
    b, s, d = x.shape
    t = _tiles(s)
    mod = _adaln(c, p["w_ada"], p["b_ada"])
    ta, tq, tk = _rope_tables(s, p["qk_norm_g"])
    for layer in range(2):
        m = mod[:, layer * 6 * d:(layer + 1) * 6 * d].reshape(b, 6, 1, d)
        sh1, sc1, g1, sh2, sc2, g2 = (m[:, j] for j in range(6))
        ng = p["norm_g"][layer].reshape(4, 1, d)
        tail = (x, g1, ng[1], ng[2], sc2, sh2, p["w_ff1"][layer], p["w_ff2"][layer], g2, ng[3])
        if layer == 0:
            qa, ka, va, qb, kb, vb, kna, knb = _inproj_even(x, ng[0], sc1, sh1, p["w_in_ab"],
                                                            ta, tq, tk, t["tm"])
            oa = _dilated_attention(qa, ka, va, kna, t["tband"])
            ob = _gqa_attention(qb, kb, vb, knb, t["tq"], t["tk"])
            x = _layer_tail_even(oa, ob, p["w_out_ab"], *tail, tm=t["tm"], tf=t["tf"])
        else:
            z, gates = _inproj_odd(x, ng[0], sc1, sh1, p["w_in_cd"], p["gate_bias"], t["tm"])
            hf, hb = _mlstm(z, gates, t["tb_mlstm"], t["lc"])
            dout = _sgu(z, p["sg_norm_g"], p["w_spatial"], p["b_spatial"], t["tb_sgu"])
            x = _layer_tail_odd(hf, hb, z, dout, p["mh_norm_g"], p["w_out_cd"], *tail,
                                tm=t["tm"], tf=t["tf"])
    return x


def _prepare_params(w_in_ab, w_out_ab, qk_norm_g, w_in_cd, w_out_cd, gate_bias, mh_norm_g,
                    sg_norm_g, w_spatial, b_spatial, w_ada, b_ada, norm_g, w_ff1, w_ff2):
    depth, d, _ = w_ada.shape
    wcd = w_in_cd[0]
    gate_lo = 4 * HALF_W
    gate_hi = gate_lo + N_GATES

    def gate_tiles(g):
        i_fw, f_fw, i_bw, f_bw = (g[:, j * C_HEADS:(j + 1) * C_HEADS] for j in range(4))
        pad = jnp.zeros((g.shape[0], LANES - 2 * C_HEADS), g.dtype)
        return jnp.concatenate([i_fw, i_bw, pad, f_fw, f_bw, pad], axis=1)

    wcd = jnp.concatenate([wcd[:, :gate_lo], wcd[:, gate_hi:],
                           gate_tiles(wcd[:, gate_lo:gate_hi])], axis=1)
    gb = gate_tiles(gate_bias[0].reshape(1, N_GATES).astype(F32))
    bexp = jnp.repeat(b_spatial[0].astype(F32).T, HEAD_DIM, axis=1)

    qb_lo = 3 * HALF_W
    order = np.asarray(_gqa_head_order())
    head_cols = (order[:, None] * HEAD_DIM + np.arange(HEAD_DIM)[None, :]).reshape(-1)
    wab = w_in_ab[0]
    wab = jnp.concatenate([wab[:, :qb_lo], wab[:, qb_lo + head_cols], wab[:, qb_lo + HALF_W:]],
                          axis=1)
    wout_ab = w_out_ab[0]
    wout_ab = jnp.concatenate([wout_ab[:HALF_W], wout_ab[HALF_W + head_cols]], axis=0)
    return dict(
        w_ada=jnp.concatenate([w_ada[l] for l in range(depth)], axis=1).astype(BF16),
        b_ada=b_ada.reshape(1, -1).astype(F32),
        w_in_ab=wab.astype(BF16), w_out_ab=wout_ab.astype(BF16),
        qk_norm_g=qk_norm_g[0],
        w_in_cd=wcd.astype(BF16), w_out_cd=w_out_cd[0].astype(BF16), gate_bias=gb,
        mh_norm_g=mh_norm_g[0].reshape(1, HALF_W).astype(F32),
        sg_norm_g=sg_norm_g[0].reshape(1, HALF_W).astype(F32),
        w_spatial=w_spatial[0].reshape(D_GROUPS * D_CHUNK, D_CHUNK).astype(BF16),
        b_spatial=bexp,
        norm_g=norm_g.astype(F32), w_ff1=w_ff1.astype(BF16), w_ff2=w_ff2.astype(BF16))


def kernel(x_prompt, x_sample, c_prompt, c_sample, w_in_ab, w_out_ab, qk_norm_g, w_in_cd, w_out_cd, gate_bias, mh_norm_g, sg_norm_g, w_spatial, b_spatial, w_ada, b_ada, norm_g, w_ff1, w_ff2):
    p = _prepare_params(w_in_ab, w_out_ab, qk_norm_g, w_in_cd, w_out_cd, gate_bias, mh_norm_g,
                        sg_norm_g, w_spatial, b_spatial, w_ada, b_ada, norm_g, w_ff1, w_ff2)
    return (_trunk(x_prompt, c_prompt, p), _trunk(x_sample, c_sample, p))
```

```python
import functools

import numpy as np
import jax
import jax.numpy as jnp
from jax import lax
from jax.experimental import pallas as pl
from jax.experimental.pallas import tpu as pltpu

F32 = jnp.float32
BF16 = jnp.bfloat16

D_MODEL = 1024
HEAD_DIM = 64
GRID_W = 64
ROPE_THETA = 10000.0
EPS = 1e-6
A_HEADS = 8
A_PATTERNS = ((128, 1), (512, 4), (2048, 16))
B_Q_HEADS = 8
B_KV_HEADS = 2
C_HEADS = 8
D_GROUPS = 8
D_CHUNK = 128
HALF_W = D_MODEL // 2
AB_IN = 3 * HALF_W + HALF_W + 2 * B_KV_HEADS * HEAD_DIM
CD_MAIN = 6 * HALF_W
N_GATES = 4 * C_HEADS
GATE_W = 256
QUAD = 4
ONES_LANE_LOWER = HEAD_DIM
ONES_LANE_UPPER = 0

LANES = 128
MXU_N = 256
V7X_VMEM_BYTES = 64 * 1024 * 1024
VMEM_LIMIT = 3 * V7X_VMEM_BYTES // 4

NEG = -1e30


def _cparams(*sem):
    return pltpu.CompilerParams(dimension_semantics=sem, vmem_limit_bytes=VMEM_LIMIT)


def _rms_rows(x):
    return x * lax.rsqrt(jnp.mean(x * x, axis=-1, keepdims=True) + EPS)


def _split_dot(a, b):
    hi = a.astype(BF16)
    r1 = a - hi.astype(F32)
    mid = r1.astype(BF16)
    lo = (r1 - mid.astype(F32)).astype(BF16)
    dot = functools.partial(jnp.dot, preferred_element_type=F32)
    return dot(hi, b) + dot(mid, b) + dot(lo, b)


def _split_dot_rhs(b, a):
    hi = a.astype(BF16)
    r1 = a - hi.astype(F32)
    mid = r1.astype(BF16)
    lo = (r1 - mid.astype(F32)).astype(BF16)
    dot = functools.partial(jnp.dot, preferred_element_type=F32)
    return dot(b, hi) + dot(b, mid) + dot(b, lo)


def _head_mean_square(z, ones_blk, precise=True):
    z2 = z * z
    hi = z2.astype(BF16)
    total = jnp.dot(hi, ones_blk, preferred_element_type=F32)
    if precise:
        lo = (z2 - hi.astype(F32)).astype(BF16)
        total = total + jnp.dot(lo, ones_blk, preferred_element_type=F32)
    return total * (1.0 / HEAD_DIM)


def _head_block_ones(width=LANES):
    r = lax.broadcasted_iota(jnp.int32, (width, width), 0) // HEAD_DIM
    c = lax.broadcasted_iota(jnp.int32, (width, width), 1) // HEAD_DIM
    return (r == c).astype(BF16)


def _log_sigmoid(x):
    return jnp.minimum(x, 0.0) - jnp.log1p(jnp.exp(-jnp.abs(x)))


def _dot_nt(a, b):
    return lax.dot_general(a, b, (((1,), (1,)), ((), ())), preferred_element_type=F32)


def _dot_tn(a, b):
    return lax.dot_general(a, b, (((0,), (0,)), ((), ())), preferred_element_type=F32)


def _adaln_kernel(c_ref, w_ref, b_ref, o_ref):
    c = c_ref[...]
    ca = c * jax.nn.sigmoid(c)
    o_ref[...] = jnp.dot(ca.astype(BF16), w_ref[...], preferred_element_type=F32) + b_ref[...]


def _adaln(c, w, b):
    nb, d = c.shape
    n = w.shape[1]
    tn = 2048
    return pl.pallas_call(
        _adaln_kernel,
        grid=(n // tn,),
        in_specs=[pl.BlockSpec((nb, d), lambda j: (0, 0)),
                  pl.BlockSpec((d, tn), lambda j: (0, j)),
                  pl.BlockSpec((1, tn), lambda j: (0, j))],
        out_specs=pl.BlockSpec((nb, tn), lambda j: (0, j)),
        out_shape=jax.ShapeDtypeStruct((nb, n), F32),
        name="adaln",
        compiler_params=_cparams("parallel"),
    )(c, w, b)


def _swap_halves(z, half, upper):
    return jnp.where(upper, pltpu.roll(z, LANES - half, axis=1), pltpu.roll(z, half, axis=1))


def _inproj_even_kernel(x_ref, g_ref, sc_ref, sh_ref, w_ref, ta_ref, tq_ref, tk_ref,
                        qa_ref, ka_ref, va_ref, qb_ref, kb_ref, vb_ref, kna_ref, knb_ref):
    x = x_ref[0]
    h = _rms_rows(x) * (g_ref[...] * (1.0 + sc_ref[0])) + sh_ref[0]
    hb = h.astype(BF16)
    tm = x.shape[0]
    lane = lax.broadcasted_iota(jnp.int32, (tm, LANES), 1)
    up64 = (lane % 64) < 32
    up32 = (lane % 32) < 16
    ones_blk = _head_block_ones(MXU_N)
    q_scale = HEAD_DIM ** -0.5

    def rotary_1d(z):
        return z * ta_ref[:, :LANES] + _swap_halves(z, 32, up64) * ta_ref[:, LANES:]

    def rotary_axial(z, t_ref):
        return z * t_ref[:, :LANES] + _swap_halves(z, 16, up32) * t_ref[:, LANES:]

    def per_chunk(f, zz):
        return jnp.concatenate([f(zz[:, :LANES]), f(zz[:, LANES:])], axis=1)

    lower = lane < HEAD_DIM

    def store_pair(ref, first, second, z, fill, masked):
        if masked:
            lo = jnp.where(lower, z, jnp.where(lane == ONES_LANE_LOWER, fill, 0.0))
            up = jnp.where(lower, jnp.where(lane == ONES_LANE_UPPER, fill, 0.0), z)
        else:
            lo = jnp.where(lane == ONES_LANE_LOWER, fill, z)
            up = jnp.where(lane == ONES_LANE_UPPER, fill, z)
        ref[0, first] = lo.astype(BF16)
        ref[0, second] = up.astype(BF16)

    def store_queries(ref, heads, y):
        sq = HEAD_DIM * _head_mean_square(y, ones_blk, precise=False) + 1e-30
        norm = sq * lax.rsqrt(sq)
        for s, (first, second) in enumerate(heads):
            cols = slice(s * LANES, (s + 1) * LANES)
            store_pair(ref, first, second, y[:, cols],
                       pltpu.roll(norm[:, cols], HEAD_DIM, axis=1), True)

    def store_keys(ref, kn_ref, heads, y):
        ms = jnp.max(_head_mean_square(y, ones_blk, precise=False), axis=0, keepdims=True)
        for s, (first, second) in enumerate(heads):
            cols = slice(s * LANES, (s + 1) * LANES)
            store_pair(ref, first, second, y[:, cols], 1.0, False)
            kn_ref[0, first // 2] = jnp.broadcast_to(ms[:, cols], (8, LANES))

    projections = [jnp.dot(hb, w_ref[:, j * MXU_N:(j + 1) * MXU_N], preferred_element_type=F32)
                   for j in range(AB_IN // MXU_N)]
    half_b = B_Q_HEADS // 2
    for j, zz in enumerate(projections):
        pairs = [(4 * (j % 2), 4 * (j % 2) + 1), (4 * (j % 2) + 2, 4 * (j % 2) + 3)]
        if j < 2:
            store_queries(qa_ref, pairs, per_chunk(rotary_1d, zz) * q_scale)
        elif j < 4:
            store_keys(ka_ref, kna_ref, pairs, per_chunk(rotary_1d, zz))
        elif j < 6:
            for s, (first, second) in enumerate(pairs):
                store_pair(va_ref, first, second, zz[:, s * LANES:(s + 1) * LANES], 1.0, False)
        elif j < 8:
            y = per_chunk(lambda z: rotary_axial(z, tq_ref), zz)
            y = y * lax.rsqrt(_head_mean_square(zz, ones_blk, precise=False) + EPS)
            store_queries(qb_ref, [(2 * (j - 6), 2 * (j - 6) + half_b),
                                   (2 * (j - 6) + 1, 2 * (j - 6) + 1 + half_b)], y)
        else:
            ms = _head_mean_square(zz, ones_blk, precise=False)[:, :LANES]
            kb = rotary_axial(zz[:, :LANES], tk_ref) * lax.rsqrt(ms + EPS)
            store_keys(kb_ref, knb_ref, [(0, 1)], jnp.concatenate([kb, kb], axis=1))
            store_pair(vb_ref, 0, 1, zz[:, LANES:], 1.0, False)


def _inproj_even(x, g, sc, sh, w, ta, tq, tk, tm):
    b, s, d = x.shape
    n = w.shape[1]
    tab = pl.BlockSpec((tm, 2 * LANES), lambda i, bb: (i, 0))
    mod = pl.BlockSpec((1, 1, d), lambda i, bb: (bb, 0, 0))
    heads = (A_HEADS, A_HEADS, A_HEADS, B_Q_HEADS, B_KV_HEADS, B_KV_HEADS)
    return pl.pallas_call(
        _inproj_even_kernel,
        grid=(s // tm, b),
        in_specs=[pl.BlockSpec((1, tm, d), lambda i, bb: (bb, i, 0)),
                  pl.BlockSpec((1, d), lambda i, bb: (0, 0)),
                  mod, mod,
                  pl.BlockSpec((d, n), lambda i, bb: (0, 0)),
                  tab, tab, tab],
        out_specs=([pl.BlockSpec((1, nh, tm, LANES), lambda i, bb: (bb, 0, i, 0)) for nh in heads]
                   + [pl.BlockSpec((1, nc, 8, LANES), lambda i, bb: (bb, 0, i, 0))
                      for nc in (A_HEADS // 2, B_KV_HEADS // 2)]),
        out_shape=([jax.ShapeDtypeStruct((b, nh, s, LANES), BF16) for nh in heads]
                   + [jax.ShapeDtypeStruct((b, nc, 8 * (s // tm), LANES), F32)
                      for nc in (A_HEADS // 2, B_KV_HEADS // 2)]),
        name="inproj_even",
        compiler_params=_cparams("parallel", "parallel"),
    )(x, g, sc, sh, w, ta, tq, tk)


def _rope_tables(s, qk_g):
    pos = jnp.arange(s)

    def angles(p, dim):
        inv_freq = ROPE_THETA ** (-jnp.arange(0, dim, 2, dtype=F32) / dim)
        return p.astype(F32)[:, None] * inv_freq[None, :]

    a1 = angles(pos, HEAD_DIM)
    cos1, sin1 = jnp.cos(a1), jnp.sin(a1)
    direct = jnp.concatenate([cos1, cos1], axis=-1)
    swapped = jnp.concatenate([-sin1, sin1], axis=-1)
    ta = jnp.concatenate([direct, direct, swapped, swapped], axis=-1)

    ar = angles(pos // GRID_W, HEAD_DIM // 2)
    ac = angles(pos % GRID_W, HEAD_DIM // 2)
    cosb = jnp.concatenate([jnp.cos(ar), jnp.cos(ar), jnp.cos(ac), jnp.cos(ac)], axis=-1)
    sinb = jnp.concatenate([-jnp.sin(ar), jnp.sin(ar), -jnp.sin(ac), jnp.sin(ac)], axis=-1)

    def gained(gain, scale):
        gain = gain.astype(F32)
        gswap = gain.reshape(2, 2, HEAD_DIM // 4)[:, ::-1].reshape(HEAD_DIM)
        direct = cosb * gain * scale
        swapped = sinb * gswap * scale
        return jnp.concatenate([direct, direct, swapped, swapped], axis=-1)

    return ta, gained(qk_g[0], HEAD_DIM ** -0.5), gained(qk_g[1], 1.0)


LOG2E = 1.4426950408889634
LN2 = 0.6931471805599453
BOUND_SLACK = 1.02
MIN_DENOM = 2.0 ** -60


def _ones_lane(upper):
    return ONES_LANE_UPPER if upper else ONES_LANE_LOWER


def _query_coefficients(kn, upper, extra=0.0):
    ms = jnp.max(kn, axis=0, keepdims=True)
    kmax = jnp.sqrt(HEAD_DIM * (ms[:, :1] + ms[:, HEAD_DIM:HEAD_DIM + 1]) + 1.0)
    lane = lax.broadcasted_iota(jnp.int32, (1, LANES), 1)
    ones = lane == _ones_lane(upper)
    return (jnp.where(ones, -LOG2E * BOUND_SLACK * kmax, LOG2E),
            jnp.where(ones, -LOG2E * extra, 0.0))


def _bounded_queries(q, coefficients):
    scale, offset = coefficients
    return (q.astype(F32) * scale + offset).astype(BF16)


def _online_step(q, k, v, bias, m_ref, acc_ref):
    s = _dot_nt(q, k)
    if bias is not None:
        s = s + bias
    m_prev = m_ref[...]
    m_new = jnp.maximum(m_prev, jnp.max(s, axis=-1, keepdims=True))
    p = jnp.exp(s - m_new).astype(BF16)
    acc_ref[...] = jnp.exp(m_prev - m_new) * acc_ref[...] + jnp.dot(p, v, preferred_element_type=F32)
    m_ref[...] = m_new


def _online_init(m_ref, acc_ref):
    m_ref[...] = jnp.full(m_ref.shape, -jnp.inf, F32)
    acc_ref[...] = jnp.zeros(acc_ref.shape, F32)


def _denominators(acc, upper):
    lane = _ones_lane(upper)
    return acc[:, lane:lane + 1]


def _normalised_pair(acc_lower, acc_upper):
    lane = lax.broadcasted_iota(jnp.int32, acc_lower.shape, 1)
    return jnp.where(lane < HEAD_DIM, acc_lower / _denominators(acc_lower, False),
                     acc_upper / _denominators(acc_upper, True))


def _gqa_kernel(q_ref, k_ref, v_ref, kn_ref, o_ref, qa_ref, acc_ref, m_ref, *, tk, rsub):
    nh, tq, dp = q_ref.shape[1:]
    gsz = nh // 2
    grows = gsz * tq
    nkv = k_ref.shape[2] // tk

    def group_q(g):
        return q_ref[0, g * gsz:(g + 1) * gsz].reshape(grows, dp)

    for g in range(2):
        qa_ref[g * grows:(g + 1) * grows, :] = _bounded_queries(
            group_q(g), _query_coefficients(kn_ref[0, 0], g == 1))
    acc_ref[...] = jnp.zeros(acc_ref.shape, F32)

    def bounded(j, carry):
        keys = pl.ds(pl.multiple_of(j * tk, tk), tk)
        subtiles = [(g, pl.ds(g * grows + r * rsub, rsub))
                    for g in range(2) for r in range(grows // rsub)]
        scores = [_dot_nt(qa_ref[rr, :], k_ref[0, g, keys, :]) for g, rr in subtiles]
        weights = [jnp.exp2(s).astype(BF16) for s in scores]
        for (g, rr), p in zip(subtiles, weights):
            acc_ref[rr, :] += jnp.dot(p, v_ref[0, g, keys, :], preferred_element_type=F32)
        return carry

    lax.fori_loop(0, nkv, bounded, 0, unroll=2 if nkv % 2 == 0 else 1)

    smallest =jnp.minimum(jnp.min(_denominators(acc_ref[:grows, :], False)),
                           jnp.min(_denominators(acc_ref[grows:, :], True)))

    @pl.when(smallest < MIN_DENOM)
    def _():
        _online_init(m_ref, acc_ref)

        def online(j, carry):
            keys = pl.ds(pl.multiple_of(j * tk, tk), tk)
            for g in range(2):
                rr = pl.ds(g * grows, grows)
                _online_step(group_q(g), k_ref[0, g, keys, :], v_ref[0, g, keys, :], None,
                             m_ref.at[rr, :], acc_ref.at[rr, :])
            return carry

        lax.fori_loop(0, nkv, online, 0)

    for j in range(gsz):
        lo = acc_ref[j * tq:(j + 1) * tq, :]
        up = acc_ref[grows + j * tq:grows + (j + 1) * tq, :]
        o_ref[0, :, j * LANES:(j + 1) * LANES] = _normalised_pair(lo, up).astype(o_ref.dtype)


def _gqa_attention(q, k, v, kn, tq, tk):
    b, hq, s, dp = q.shape
    assert k.shape[1] == 2 and hq % 2 == 0
    rows = hq * tq
    kv = pl.BlockSpec((1, 2, s, dp), lambda bb, i: (bb, 0, 0, 0))
    return pl.pallas_call(
        functools.partial(_gqa_kernel, tk=tk, rsub=min(256, rows // 2)),
        grid=(b, s // tq),
        in_specs=[pl.BlockSpec((1, hq, tq, dp), lambda bb, i: (bb, 0, i, 0)), kv, kv,
                  pl.BlockSpec((1,) + kn.shape[1:], lambda bb, i: (bb, 0, 0, 0))],
        out_specs=pl.BlockSpec((1, tq, hq * HEAD_DIM), lambda bb, i: (bb, i, 0)),
        out_shape=jax.ShapeDtypeStruct((b, s, hq * HEAD_DIM), BF16),
        scratch_shapes=[pltpu.VMEM((rows, dp), BF16), pltpu.VMEM((rows, dp), F32),
                        pltpu.VMEM((rows, 1), F32)],
        name="gqa_attention",
        compiler_params=_cparams("parallel", "parallel"),
    )(q, k, v, kn)


def _band_reach(t):
    return -(-max(w // 2 for w, _ in A_PATTERNS) // t)


def _band_bias(t, nproc):
    offs = range(-(nproc - 1), nproc)
    a = np.arange(t)[:, None]
    bcol = np.arange(t)[None, :]
    out = np.zeros((len(offs), t, t), np.float64)
    for n, d in enumerate(offs):
        j = d * t + bcol - a
        mult = np.zeros((t, t), np.float64)
        for window, dil in A_PATTERNS:
            mult += ((j % dil) == 0) & (np.abs(j) <= window // 2)
        out[n] = np.where(mult > 0, LOG2E * np.log(np.maximum(mult, 1.0)), NEG)
    return jnp.asarray(out, F32)


def _band_kernel(q_ref, k_ref, v_ref, kn_ref, bias_ref, o_ref, qa_ref, acc_ref, m_ref, *,
                 t, nproc, rsub):
    nsplit = q_ref.shape[2] // t
    nblk = k_ref.shape[2] // t
    max_bias = float(np.log(len(A_PATTERNS)))
    subtiles = [pl.ds(r * rsub, rsub) for r in range(t // rsub)]

    def block_index(u):
        return pl.program_id(2) * nsplit + u

    def key_block(u, n):
        i = block_index(u)
        kb = jnp.clip(i - _band_reach(t), 0, nblk - nproc) + n
        return pl.ds(pl.multiple_of(kb * t, t), t), bias_ref.at[kb - i + nproc - 1]

    def store(u):
        o_ref[0, u * t:(u + 1) * t, :] = _normalised_pair(
            acc_ref[u, 0], acc_ref[u, 1]).astype(o_ref.dtype)

    smallest = []
    for u in range(nsplit):
        rows = slice(u * t, (u + 1) * t)
        for h in range(2):
            qa_ref[u, h] = _bounded_queries(
                q_ref[0, h, rows, :], _query_coefficients(kn_ref[0, 0], h == 1, max_bias))
            for n in range(nproc):
                keys, bias = key_block(u, n)
                scores = [_dot_nt(qa_ref[u, h, rr, :], k_ref[0, h, keys, :]) + bias[rr, :]
                          for rr in subtiles]
                weights = [jnp.exp2(s).astype(BF16) for s in scores]
                for rr, p in zip(subtiles, weights):
                    update = jnp.dot(p, v_ref[0, h, keys, :], preferred_element_type=F32)
                    if n == 0:
                        acc_ref[u, h, rr, :] = update
                    else:
                        acc_ref[u, h, rr, :] += update
        smallest.append(jnp.minimum(jnp.min(_denominators(acc_ref[u, 0], False)),
                                    jnp.min(_denominators(acc_ref[u, 1], True))))
        store(u)

    for u in range(nsplit):
        @pl.when(smallest[u] < MIN_DENOM)
        def _():
            for h in range(2):
                acc = acc_ref.at[u, h]
                _online_init(m_ref, acc)

                def online(n, carry):
                    keys, bias = key_block(u, n)
                    _online_step(q_ref[0, h, u * t:(u + 1) * t, :], k_ref[0, h, keys, :],
                                 v_ref[0, h, keys, :], LN2 * bias[...], m_ref, acc)
                    return carry

                lax.fori_loop(0, nproc, online, 0)
            store(u)


def _dilated_attention(q, k, v, kn, t):
    b, h, s, dp = q.shape
    nproc = min(2 * _band_reach(t) + 1, s // t)
    bias = _band_bias(t, nproc)
    nsplit = max(n for n in (1, 2, 4) if s % (n * t) == 0)
    rows = nsplit * t
    kv = pl.BlockSpec((1, 2, s, dp), lambda bb, hh, i: (bb, hh, 0, 0))
    return pl.pallas_call(
        functools.partial(_band_kernel, t=t, nproc=nproc, rsub=min(256, t)),
        grid=(b, h // 2, s // rows),
        in_specs=[pl.BlockSpec((1, 2, rows, dp), lambda bb, hh, i: (bb, hh, i, 0)), kv, kv,
                  pl.BlockSpec((1, 1) + kn.shape[2:], lambda bb, hh, i: (bb, hh, 0, 0)),
                  pl.BlockSpec(bias.shape, lambda bb, hh, i: (0, 0, 0),
                               pipeline_mode=pl.Buffered(1))],
        out_specs=pl.BlockSpec((1, rows, LANES), lambda bb, hh, i: (bb, i, hh)),
        out_shape=jax.ShapeDtypeStruct((b, s, h * HEAD_DIM), BF16),
        scratch_shapes=[pltpu.VMEM((nsplit, 2, t, dp), BF16), pltpu.VMEM((nsplit, 2, t, dp), F32),
                        pltpu.VMEM((t, 1), F32)],
        name="dilated_attention",
        compiler_params=_cparams("parallel", "parallel", "parallel"),
    )(q, k, v, kn, bias)


def _residual_update(x, y, gate, norm_g):
    return x + gate * (_rms_rows(y) * norm_g)


def _mixer_output_even(oa_ref, ob_ref):
    return oa_ref[0], ob_ref[0]


def _mixer_output_odd(hf_ref, hb_ref, og_ref, d_ref, mhg_ref):
    ones_blk = _head_block_ones()
    parts = []
    for c in range(HALF_W // LANES):
        cols = slice(c * LANES, (c + 1) * LANES)
        hs = hf_ref[0, :, cols].astype(F32) + hb_ref[0, :, cols].astype(F32)
        hn = hs * lax.rsqrt(_head_mean_square(hs, ones_blk) + EPS) * mhg_ref[:, cols]
        parts.append((hn * jax.nn.sigmoid(og_ref[0, :, cols].astype(F32))).astype(BF16))
    return jnp.concatenate(parts, axis=-1), d_ref[0]


def _layer_tail_kernel(*refs, odd, tf):
    n_mix = 5 if odd else 2
    mix = (_mixer_output_odd if odd else _mixer_output_even)(*refs[:n_mix])
    (wo_ref, x_ref, g1_ref, ng1_ref, ng2_ref, sc_ref, sh_ref, w1_ref, w2_ref, g2_ref, ng3_ref,
     o_ref, a_ref) = refs[n_mix:]
    y = jnp.dot(mix[0], wo_ref[:HALF_W, :], preferred_element_type=F32)
    y = y + jnp.dot(mix[1], wo_ref[HALF_W:, :], preferred_element_type=F32)
    x1 = _residual_update(x_ref[0], y, g1_ref[0], ng1_ref[...])
    h = (_rms_rows(x1) * (ng2_ref[...] * (1.0 + sc_ref[0])) + sh_ref[0]).astype(BF16)
    for kc in range(w1_ref.shape[1] // tf):
        cols = slice(kc * tf, (kc + 1) * tf)
        a = jnp.maximum(jnp.dot(h, w1_ref[:, cols], preferred_element_type=F32), 0.0)
        a_ref[:, cols] = (a * a).astype(BF16)
    y2 = jnp.dot(a_ref[...], w2_ref[...], preferred_element_type=F32)
    o_ref[0] = _residual_update(x1, y2, g2_ref[0], ng3_ref[...])


def _layer_tail(mix_inputs, mix_specs, odd, wo, x, g1, ng1, ng2, sc, sh, w1, w2, g2, ng3, tm, tf):
    b, s, d = x.shape
    f = w1.shape[1]
    full = pl.BlockSpec((1, tm, d), lambda bb, i: (bb, i, 0))
    mod = pl.BlockSpec((1, 1, d), lambda bb, i: (bb, 0, 0))
    vec = pl.BlockSpec((1, d), lambda bb, i: (0, 0))

    def resident(shape):
        return pl.BlockSpec(shape, lambda bb, i: (0, 0), pipeline_mode=pl.Buffered(1))

    return pl.pallas_call(
        functools.partial(_layer_tail_kernel, odd=odd, tf=tf),
        grid=(b, s // tm),
        in_specs=mix_specs + [resident((d, d)), full, mod, vec, vec, mod, mod,
                              resident((d, f)), resident((f, d)), mod, vec],
        out_specs=full,
        out_shape=jax.ShapeDtypeStruct((b, s, d), F32),
        scratch_shapes=[pltpu.VMEM((tm, f), BF16)],
        name="layer_tail_odd" if odd else "layer_tail_even",
        compiler_params=_cparams("parallel", "parallel"),
    )(*mix_inputs, wo, x, g1, ng1, ng2, sc, sh, w1, w2, g2, ng3)


def _layer_tail_even(oa, ob, *rest, tm, tf):
    half = pl.BlockSpec((1, tm, HALF_W), lambda bb, i: (bb, i, 0))
    return _layer_tail([oa, ob], [half, half], False, *rest, tm, tf)


def _layer_tail_odd(hf, hb, z, dout, mhg, *rest, tm, tf):
    half = pl.BlockSpec((1, tm, HALF_W), lambda bb, i: (bb, i, 0))
    og = pl.BlockSpec((1, tm, HALF_W), lambda bb, i: (bb, i, 3))
    return _layer_tail([hf, hb, z, dout, mhg],
                       [half, half, og, half, pl.BlockSpec((1, HALF_W), lambda bb, i: (0, 0))],
                       True, *rest, tm, tf)


def _inproj_odd_kernel(x_ref, g_ref, sc_ref, sh_ref, w_ref, gb_ref, z_ref, gates_ref):
    x = x_ref[0]
    h = _rms_rows(x) * (g_ref[...] * (1.0 + sc_ref[0])) + sh_ref[0]
    hb = h.astype(BF16)
    k_scale = HEAD_DIM ** -0.5
    projections = [jnp.dot(hb, w_ref[:, j * MXU_N:(j + 1) * MXU_N], preferred_element_type=F32)
                   for j in range(CD_MAIN // MXU_N)]
    gates_ref[0] = jnp.dot(hb, w_ref[:, CD_MAIN:], preferred_element_type=F32) + gb_ref[...]
    for j, zz in enumerate(projections):
        if HALF_W <= j * MXU_N < 2 * HALF_W:
            zz = zz * k_scale
        z_ref[0, :, j * MXU_N:(j + 1) * MXU_N] = zz.astype(BF16)


def _inproj_odd(x, g, sc, sh, w, gb, tm):
    b, s, d = x.shape
    n = w.shape[1]
    mod = pl.BlockSpec((1, 1, d), lambda i, bb: (bb, 0, 0))
    return pl.pallas_call(
        _inproj_odd_kernel,
        grid=(s // tm, b),
        in_specs=[pl.BlockSpec((1, tm, d), lambda i, bb: (bb, i, 0)),
                  pl.BlockSpec((1, d), lambda i, bb: (0, 0)),
                  mod, mod,
                  pl.BlockSpec((d, n), lambda i, bb: (0, 0)),
                  pl.BlockSpec((1, GATE_W), lambda i, bb: (0, 0))],
        out_specs=[pl.BlockSpec((1, tm, CD_MAIN), lambda i, bb: (bb, i, 0)),
                   pl.BlockSpec((1, tm, GATE_W), lambda i, bb: (bb, i, 0))],
        out_shape=[jax.ShapeDtypeStruct((b, s, CD_MAIN), BF16),
                   jax.ShapeDtypeStruct((b, s, GATE_W), F32)],
        name="inproj_odd",
        compiler_params=_cparams("parallel", "parallel"),
    )(x, g, sc, sh, w, gb)


def _gate_lane(rev, head):
    return (C_HEADS if rev else 0) + head


def _mlstm_constants(rev, lc):
    wq = QUAD * HEAD_DIM
    t = np.arange(lc)
    sees = (t[None, :] >= t[:, None]) if rev else (t[None, :] <= t[:, None])
    nq = C_HEADS // QUAD
    sel_s = np.zeros((nq, LANES, QUAD * lc), np.float32)
    sel_v = np.zeros((nq, LANES, wq), np.float32)
    for qd in range(nq):
        for hh in range(QUAD):
            lane = _gate_lane(rev, QUAD * qd + hh)
            sel_s[qd, lane, hh * lc:(hh + 1) * lc] = 1.0
            sel_v[qd, lane, hh * HEAD_DIM:(hh + 1) * HEAD_DIM] = 1.0
    head_of_row = np.arange(QUAD * lc) // lc
    head_of_col = np.arange(wq) // HEAD_DIM
    return dict(
        tri=jnp.asarray(sees, BF16),
        sel_s=jnp.asarray(sel_s, BF16), sel_v=jnp.asarray(sel_v, BF16),
        ones_bd=jnp.asarray(np.transpose(sel_s, (0, 2, 1)), BF16),
        row_head=jnp.asarray(head_of_row[:, None] == head_of_col[None, :], BF16),
        diag=jnp.asarray(head_of_col[:, None] == head_of_col[None, :], F32),
        causal=jnp.asarray(np.tile(sees, (1, QUAD)), F32))


def _scan_max(x, rev):
    n = x.shape[0]
    row = lax.broadcasted_iota(jnp.int32, x.shape, 0)
    sh = 1
    while sh < n:
        if rev:
            x = jnp.maximum(x, jnp.where(row < n - sh, pltpu.roll(x, n - sh, axis=0), -jnp.inf))
        else:
            x = jnp.maximum(x, jnp.where(row >= sh, pltpu.roll(x, sh, axis=0), -jnp.inf))
        sh *= 2
    return x


def _mlstm_kernel(*refs, lc):
    nd = len(_MLSTM_DIR_CONSTS)
    blocks = (refs[0:4], refs[4:8])
    rowhead_ref, diag_ref = refs[8:10]
    consts = (refs[10:10 + nd], refs[10 + nd:10 + 2 * nd])
    outs = refs[10 + 2 * nd:12 + 2 * nd]
    c_sc, n_sc, m_sc = refs[12 + 2 * nd:]

    @pl.when(pl.program_id(1) == 0)
    def _():
        c_sc[...] = jnp.zeros(c_sc.shape, F32)
        n_sc[...] = jnp.zeros(n_sc.shape, F32)
        m_sc[...] = jnp.zeros(m_sc.shape, F32)

    nch = outs[0].shape[1] // lc

    wq = QUAD * HEAD_DIM
    edges = (lc - 1, 0)
    jobs = [(rev, qd) for rev in (0, 1) for qd in range(C_HEADS // QUAD)]
    dot = functools.partial(jnp.dot, preferred_element_type=F32)

    def chunk(ci, carry):
        rows = (pl.ds(pl.multiple_of(ci * lc, lc), lc),
                pl.ds(pl.multiple_of((nch - 1 - ci) * lc, lc), lc))
        gate = []
        for rev in (0, 1):
            g_ref = blocks[rev][3]
            tri_ref = consts[rev][0]
            gi = g_ref[0, rows[rev], :LANES]
            bcum = _split_dot_rhs(tri_ref[...], _log_sigmoid(g_ref[0, rows[rev], LANES:]))
            a = gi - bcum
            m_prev = m_sc[rev]
            mt_b = jnp.maximum(m_prev, _scan_max(a, bool(rev))).astype(BF16)
            mt = mt_b.astype(F32)
            w_b = jnp.exp(m_prev - mt).astype(BF16)
            b_all = bcum[edges[rev]:edges[rev] + 1, :]
            log_w = b_all + a
            m_new = jnp.maximum(b_all + m_prev, jnp.max(log_w, axis=0, keepdims=True))
            gate.append(dict(
                mt_b=mt_b, w_b=w_b, w=w_b.astype(F32),
                floor=jnp.exp(-(bcum + mt)),
                wk_b=jnp.exp(log_w - m_new).astype(BF16),
                decay=jnp.broadcast_to(jnp.exp(b_all + m_prev - m_new), (8, LANES)),
                a_t=a.T))
            m_sc[rev] = m_new

        row_head = rowhead_ref[...]

        def operand(rev, qd, j):
            return blocks[rev][j][0, rows[rev], qd * wq:(qd + 1) * wq]

        def sel_v(rev, qd):
            return consts[rev][2][qd]

        scores, log_e = [], []
        for rev, qd in jobs:
            kbd = jnp.concatenate([operand(rev, qd, 1)] * QUAD, axis=0) * row_head
            scores.append(_dot_nt(operand(rev, qd, 0), kbd))
            a_rows = jnp.concatenate(
                [jnp.broadcast_to(gate[rev]["a_t"][ln:ln + 1, :], (lc, lc))
                 for ln in (_gate_lane(bool(rev), QUAD * qd + hh) for hh in range(QUAD))], axis=1)
            log_e.append(jnp.where(consts[rev][4][...] > 0.0,
                                   a_rows - dot(gate[rev]["mt_b"], consts[rev][1][qd]), NEG))
        probs = [(s * jnp.exp(e)).astype(BF16) for s, e in zip(scores, log_e)]

        pvs, qcs = [], []
        for (rev, qd), p in zip(jobs, probs):
            v = operand(rev, qd, 2)
            vaug = jnp.concatenate(
                [jnp.concatenate([v] * QUAD, axis=0) * row_head, consts[rev][3][qd]], axis=1)
            pvs.append(dot(p, vaug))
            caug = jnp.concatenate(
                [c_sc[rev, qd].astype(BF16),
                 (sel_v(rev, qd).astype(F32) * n_sc[rev, qd]).astype(BF16)], axis=0)
            qcs.append(_dot_nt(operand(rev, qd, 0), caug))

        for (rev, qd), pv, qc in zip(jobs, pvs, qcs):
            g = gate[rev]
            den = pv[:, wq:] + g["w"] * qc[:, wq:]
            scale_b = (1.0 / jnp.maximum(jnp.abs(den), g["floor"])).astype(BF16)
            h_out = dot(scale_b, sel_v(rev, qd)) * (
                pv[:, :wq] + dot(g["w_b"], sel_v(rev, qd)) * qc[:, :wq])
            outs[rev][0, rows[rev], qd * wq:(qd + 1) * wq] = h_out.astype(outs[rev].dtype)

        for rev, qd in jobs:
            g = gate[rev]
            k = operand(rev, qd, 1)
            wk_v = dot(g["wk_b"], sel_v(rev, qd))
            dec_v = _split_dot(g["decay"], sel_v(rev, qd))[:1]
            vw = (wk_v * operand(rev, qd, 2).astype(F32)).astype(BF16)
            c_sc[rev, qd] = dec_v * c_sc[rev, qd] + diag_ref[...] * _dot_tn(vw, k)
            n_sc[rev, qd] = (dec_v * n_sc[rev, qd]
                             + jnp.sum(wk_v * k.astype(F32), axis=0, keepdims=True))
        return carry

    lax.fori_loop(0, nch, chunk, 0, unroll=2 if nch % 2 == 0 else 1)


_MLSTM_DIR_CONSTS = ("tri", "sel_s", "sel_v", "ones_bd", "causal")


def _mlstm(z, gates, tb, lc):
    b, s, _ = z.shape
    nblk = s // tb
    position = (lambda i: i, lambda i: nblk - 1 - i)

    def col(rev, j):
        return pl.BlockSpec((1, tb, HALF_W), lambda bb, i: (bb, position[rev](i), j))

    def block_specs(rev):
        return [col(rev, 0), col(rev, 1), col(rev, 2),
                pl.BlockSpec((1, tb, GATE_W), lambda bb, i: (bb, position[rev](i), 0))]

    def const_spec(a):
        return pl.BlockSpec(a.shape, lambda bb, i: (0,) * a.ndim)

    consts = (_mlstm_constants(False, lc), _mlstm_constants(True, lc))
    shared = [consts[0]["row_head"], consts[0]["diag"]]
    per_dir = [consts[rev][n] for rev in (0, 1) for n in _MLSTM_DIR_CONSTS]
    nq = C_HEADS // QUAD
    wq = QUAD * HEAD_DIM
    return pl.pallas_call(
        functools.partial(_mlstm_kernel, lc=lc),
        grid=(b, nblk),
        in_specs=block_specs(0) + block_specs(1) + [const_spec(a) for a in shared + per_dir],
        out_specs=[col(0, 0), col(1, 0)],
        out_shape=[jax.ShapeDtypeStruct((b, s, HALF_W), BF16)] * 2,
        scratch_shapes=[pltpu.VMEM((2, nq, wq, wq), F32), pltpu.VMEM((2, nq, 1, wq), F32),
                        pltpu.VMEM((2, 1, LANES), F32)],
        name="mlstm",
        compiler_params=_cparams("parallel", "arbitrary"),
    )(z, z, z, gates, z, z, z, gates, *shared, *per_dir)


def _sgu_kernel(u_ref, v_ref, lng_ref, w_ref, b_ref, o_ref):
    tb = u_ref.shape[1]
    group = lax.broadcasted_iota(jnp.int32, (D_CHUNK, HALF_W), 1) // HEAD_DIM
    for ch in range(tb // D_CHUNK):
        rows = slice(ch * D_CHUNK, (ch + 1) * D_CHUNK)
        u = jax.nn.gelu(u_ref[0, rows, :].astype(F32))
        v = jax.nn.gelu(v_ref[0, rows, :].astype(F32))
        dv = v - jnp.mean(v, axis=-1, keepdims=True)
        vn = dv * lax.rsqrt(jnp.mean(dv * dv, axis=-1, keepdims=True) + EPS) * lng_ref[...]
        full = jnp.dot(w_ref[...], vn.astype(BF16), preferred_element_type=F32)
        sg = b_ref[...]
        for g in range(D_GROUPS):
            sg = sg + jnp.where(group == g, full[g * D_CHUNK:(g + 1) * D_CHUNK, :], 0.0)
        o_ref[0, rows, :] = (u * sg).astype(BF16)


def _sgu(z, lng, w, bexp, tb):
    b, s, _ = z.shape
    col = lambda j: pl.BlockSpec((1, tb, HALF_W), lambda bb, i: (bb, i, j))
    return pl.pallas_call(
        _sgu_kernel,
        grid=(b, s // tb),
        in_specs=[col(4), col(5),
                  pl.BlockSpec((1, HALF_W), lambda bb, i: (0, 0)),
                  pl.BlockSpec((D_GROUPS * D_CHUNK, D_CHUNK), lambda bb, i: (0, 0)),
                  pl.BlockSpec((D_CHUNK, HALF_W), lambda bb, i: (0, 0))],
        out_specs=col(0),
        out_shape=jax.ShapeDtypeStruct((b, s, HALF_W), BF16),
        name="sgu",
        compiler_params=_cparams("parallel", "parallel"),
    )(z, z, lng, w, bexp)


def _tiles(s):
    return dict(tm=min(512, s), tf=1024, tq=min(256, s), tk=min(1024, s), tband=min(512, s),
                tb_mlstm=min(1024, s), lc=128, tb_sgu=min(512, s))


def _gqa_head_order():
    half = B_Q_HEADS // 2
    return [h for j in range(half) for h in (j, half + j)]


def _trunk(x, c, p):
    b, s, d = x.shape
    t = _tiles(s)
    mod = _adaln(c, p["w_ada"], p["b_ada"])
    ta, tq, tk = _rope_tables(s, p["qk_norm_g"])
    for layer in range(2):
        m = mod[:, layer * 6 * d:(layer + 1) * 6 * d].reshape(b, 6, 1, d)
        sh1, sc1, g1, sh2, sc2, g2 = (m[:, j] for j in range(6))
        ng = p["norm_g"][layer].reshape(4, 1, d)
        tail = (x, g1, ng[1], ng[2], sc2, sh2, p["w_ff1"][layer], p["w_ff2"][layer], g2, ng[3])
        if layer == 0:
            qa, ka, va, qb, kb, vb, kna, knb = _inproj_even(x, ng[0], sc1, sh1, p["w_in_ab"],
                                                            ta, tq, tk, t["tm"])
            oa = _dilated_attention(qa, ka, va, kna, t["tband"])
            ob = _gqa_attention(qb, kb, vb, knb, t["tq"], t["tk"])
            x = _layer_tail_even(oa, ob, p["w_out_ab"], *tail, tm=t["tm"], tf=t["tf"])
        else:
            z, gates = _inproj_odd(x, ng[0], sc1, sh1, p["w_in_cd"], p["gate_bias"], t["tm"])
            hf, hb = _mlstm(z, gates, t["tb_mlstm"], t["lc"])
            dout = _sgu(z, p["sg_norm_g"], p["w_spatial"], p["b_spatial"], t["tb_sgu"])
            x = _layer_tail_odd(hf, hb, z, dout, p["mh_norm_g"], p["w_out_cd"], *tail,
                                tm=t["tm"], tf=t["tf"])
    return x


def _prepare_params(w_in_ab, w_out_ab, qk_norm_g, w_in_cd, w_out_cd, gate_bias, mh_norm_g,
                    sg_norm_g, w_spatial, b_spatial, w_ada, b_ada, norm_g, w_ff1, w_ff2):
    depth, d, _ = w_ada.shape
    wcd = w_in_cd[0]
    gate_lo = 4 * HALF_W
    gate_hi = gate_lo + N_GATES

    def gate_tiles(g):
        i_fw, f_fw, i_bw, f_bw = (g[:, j * C_HEADS:(j + 1) * C_HEADS] for j in range(4))
        pad = jnp.zeros((g.shape[0], LANES - 2 * C_HEADS), g.dtype)
        return jnp.concatenate([i_fw, i_bw, pad, f_fw, f_bw, pad], axis=1)

    wcd = jnp.concatenate([wcd[:, :gate_lo], wcd[:, gate_hi:],
                           gate_tiles(wcd[:, gate_lo:gate_hi])], axis=1)
    gb = gate_tiles(gate_bias[0].reshape(1, N_GATES).astype(F32))
    bexp = jnp.repeat(b_spatial[0].astype(F32).T, HEAD_DIM, axis=1)

    qb_lo = 3 * HALF_W
    order = np.asarray(_gqa_head_order())
    head_cols = (order[:, None] * HEAD_DIM + np.arange(HEAD_DIM)[None, :]).reshape(-1)
    wab = w_in_ab[0]
    wab = jnp.concatenate([wab[:, :qb_lo], wab[:, qb_lo + head_cols], wab[:, qb_lo + HALF_W:]],
                          axis=1)
    wout_ab = w_out_ab[0]
    wout_ab = jnp.concatenate([wout_ab[:HALF_W], wout_ab[HALF_W + head_cols]], axis=0)
    return dict(
        w_ada=jnp.concatenate([w_ada[l] for l in range(depth)], axis=1).astype(BF16),
        b_ada=b_ada.reshape(1, -1).astype(F32),
        w_in_ab=wab.astype(BF16), w_out_ab=wout_ab.astype(BF16),
        qk_norm_g=qk_norm_g[0],
        w_in_cd=wcd.astype(BF16), w_out_cd=w_out_cd[0].astype(BF16), gate_bias=gb,
        mh_norm_g=mh_norm_g[0].reshape(1, HALF_W).astype(F32),
        sg_norm_g=sg_norm_g[0].reshape(1, HALF_W).astype(F32),
        w_spatial=w_spatial[0].reshape(D_GROUPS * D_CHUNK, D_CHUNK).astype(BF16),
        b_spatial=bexp,
        norm_g=norm_g.astype(F32), w_ff1=w_ff1.astype(BF16), w_ff2=w_ff2.astype(BF16))


def kernel(x_prompt, x_sample, c_prompt, c_sample, w_in_ab, w_out_ab, qk_norm_g, w_in_cd, w_out_cd, gate_bias, mh_norm_g, sg_norm_g, w_spatial, b_spatial, w_ada, b_ada, norm_g, w_ff1, w_ff2):
    p = _prepare_params(w_in_ab, w_out_ab, qk_norm_g, w_in_cd, w_out_cd, gate_bias, mh_norm_g,
                        sg_norm_g, w_spatial, b_spatial, w_ada, b_ada, norm_g, w_ff1, w_ff2)
    return (_trunk(x_prompt, c_prompt, p), _trunk(x_sample, c_sample, p))
```

```python
import functools

import numpy as np
import jax
import jax.numpy as jnp
from jax import lax
from jax.experimental import pallas as pl
from jax.experimental.pallas import tpu as pltpu

F32 = jnp.float32
BF16 = jnp.bfloat16

D_MODEL = 1024
HEAD_DIM = 64
GRID_W = 64
ROPE_THETA = 10000.0
EPS = 1e-6
A_HEADS = 8
A_PATTERNS = ((128, 1), (512, 4), (2048, 16))
B_Q_HEADS = 8
B_KV_HEADS = 2
C_HEADS = 8
D_GROUPS = 8
D_CHUNK = 128
HALF_W = D_MODEL // 2
AB_IN = 3 * HALF_W + HALF_W + 2 * B_KV_HEADS * HEAD_DIM
CD_MAIN = 6 * HALF_W
N_GATES = 4 * C_HEADS
GATE_W = 256
QUAD = 4
ONES_LANE_LOWER = HEAD_DIM
ONES_LANE_UPPER = 0

LANES = 128
MXU_N = 256
V7X_VMEM_BYTES = 64 * 1024 * 1024
VMEM_LIMIT = 3 * V7X_VMEM_BYTES // 4

NEG = -1e30


def _cparams(*sem):
    return pltpu.CompilerParams(dimension_semantics=sem, vmem_limit_bytes=VMEM_LIMIT)


def _rms_rows(x):
    return x * lax.rsqrt(jnp.mean(x * x, axis=-1, keepdims=True) + EPS)


def _split_dot(a, b):
    hi = a.astype(BF16)
    r1 = a - hi.astype(F32)
    mid = r1.astype(BF16)
    lo = (r1 - mid.astype(F32)).astype(BF16)
    dot = functools.partial(jnp.dot, preferred_element_type=F32)
    return dot(hi, b) + dot(mid, b) + dot(lo, b)


def _split_dot_rhs(b, a):
    hi = a.astype(BF16)
    r1 = a - hi.astype(F32)
    mid = r1.astype(BF16)
    lo = (r1 - mid.astype(F32)).astype(BF16)
    dot = functools.partial(jnp.dot, preferred_element_type=F32)
    return dot(b, hi) + dot(b, mid) + dot(b, lo)


def _head_mean_square(z, ones_blk, precise=True):
    z2 = z * z
    hi = z2.astype(BF16)
    total = jnp.dot(hi, ones_blk, preferred_element_type=F32)
    if precise:
        lo = (z2 - hi.astype(F32)).astype(BF16)
        total = total + jnp.dot(lo, ones_blk, preferred_element_type=F32)
    return total * (1.0 / HEAD_DIM)


def _head_block_ones(width=LANES):
    r = lax.broadcasted_iota(jnp.int32, (width, width), 0) // HEAD_DIM
    c = lax.broadcasted_iota(jnp.int32, (width, width), 1) // HEAD_DIM
    return (r == c).astype(BF16)


def _log_sigmoid(x):
    return jnp.minimum(x, 0.0) - jnp.log1p(jnp.exp(-jnp.abs(x)))


def _dot_nt(a, b):
    return lax.dot_general(a, b, (((1,), (1,)), ((), ())), preferred_element_type=F32)


def _dot_tn(a, b):
    return lax.dot_general(a, b, (((0,), (0,)), ((), ())), preferred_element_type=F32)


def _adaln_kernel(c_ref, w_ref, b_ref, o_ref):
    c = c_ref[...]
    ca = c * jax.nn.sigmoid(c)
    o_ref[...] = jnp.dot(ca.astype(BF16), w_ref[...], preferred_element_type=F32) + b_ref[...]


def _adaln(c, w, b):
    nb, d = c.shape
    n = w.shape[1]
    tn = 2048
    return pl.pallas_call(
        _adaln_kernel,
        grid=(n // tn,),
        in_specs=[pl.BlockSpec((nb, d), lambda j: (0, 0)),
                  pl.BlockSpec((d, tn), lambda j: (0, j)),
                  pl.BlockSpec((1, tn), lambda j: (0, j))],
        out_specs=pl.BlockSpec((nb, tn), lambda j: (0, j)),
        out_shape=jax.ShapeDtypeStruct((nb, n), F32),
        name="adaln",
        compiler_params=_cparams("parallel"),
    )(c, w, b)


def _swap_halves(z, half, upper):
    return jnp.where(upper, pltpu.roll(z, LANES - half, axis=1), pltpu.roll(z, half, axis=1))


def _inproj_even_kernel(x_ref, g_ref, sc_ref, sh_ref, w_ref, ta_ref, tq_ref, tk_ref,
                        qa_ref, ka_ref, va_ref, qb_ref, kb_ref, vb_ref, kna_ref, knb_ref):
    x = x_ref[0]
    h = _rms_rows(x) * (g_ref[...] * (1.0 + sc_ref[0])) + sh_ref[0]
    hb = h.astype(BF16)
    tm = x.shape[0]
    lane = lax.broadcasted_iota(jnp.int32, (tm, LANES), 1)
    up64 = (lane % 64) < 32
    up32 = (lane % 32) < 16
    ones_blk = _head_block_ones(MXU_N)
    q_scale = HEAD_DIM ** -0.5

    def rotary_1d(z):
        return z * ta_ref[:, :LANES] + _swap_halves(z, 32, up64) * ta_ref[:, LANES:]

    def rotary_axial(z, t_ref):
        return z * t_ref[:, :LANES] + _swap_halves(z, 16, up32) * t_ref[:, LANES:]

    def per_chunk(f, zz):
        return jnp.concatenate([f(zz[:, :LANES]), f(zz[:, LANES:])], axis=1)

    lower = lane < HEAD_DIM

    def store_pair(ref, first, second, z, fill, masked):
        if masked:
            lo = jnp.where(lower, z, jnp.where(lane == ONES_LANE_LOWER, fill, 0.0))
            up = jnp.where(lower, jnp.where(lane == ONES_LANE_UPPER, fill, 0.0), z)
        else:
            lo = jnp.where(lane == ONES_LANE_LOWER, fill, z)
            up = jnp.where(lane == ONES_LANE_UPPER, fill, z)
        ref[0, first] = lo.astype(BF16)
        ref[0, second] = up.astype(BF16)

    def store_queries(ref, heads, y):
        sq = HEAD_DIM * _head_mean_square(y, ones_blk, precise=False) + 1e-30
        norm = sq * lax.rsqrt(sq)
        for s, (first, second) in enumerate(heads):
            cols = slice(s * LANES, (s + 1) * LANES)
            store_pair(ref, first, second, y[:, cols],
                       pltpu.roll(norm[:, cols], HEAD_DIM, axis=1), True)

    def store_keys(ref, kn_ref, heads, y):
        ms = jnp.max(_head_mean_square(y, ones_blk, precise=False), axis=0, keepdims=True)
        for s, (first, second) in enumerate(heads):
            cols = slice(s * LANES, (s + 1) * LANES)
            store_pair(ref, first, second, y[:, cols], 1.0, False)
            kn_ref[0, first // 2] = jnp.broadcast_to(ms[:, cols], (8, LANES))

    projections = [jnp.dot(hb, w_ref[:, j * MXU_N:(j + 1) * MXU_N], preferred_element_type=F32)
                   for j in range(AB_IN // MXU_N)]
    half_b = B_Q_HEADS // 2
    for j, zz in enumerate(projections):
        pairs = [(4 * (j % 2), 4 * (j % 2) + 1), (4 * (j % 2) + 2, 4 * (j % 2) + 3)]
        if j < 2:
            store_queries(qa_ref, pairs, per_chunk(rotary_1d, zz) * q_scale)
        elif j < 4:
            store_keys(ka_ref, kna_ref, pairs, per_chunk(rotary_1d, zz))
        elif j < 6:
            for s, (first, second) in enumerate(pairs):
                store_pair(va_ref, first, second, zz[:, s * LANES:(s + 1) * LANES], 1.0, False)
        elif j < 8:
            y = per_chunk(lambda z: rotary_axial(z, tq_ref), zz)
            y = y * lax.rsqrt(_head_mean_square(zz, ones_blk, precise=False) + EPS)
            store_queries(qb_ref, [(2 * (j - 6), 2 * (j - 6) + half_b),
                                   (2 * (j - 6) + 1, 2 * (j - 6) + 1 + half_b)], y)
        else:
            ms = _head_mean_square(zz, ones_blk, precise=False)[:, :LANES]
            kb = rotary_axial(zz[:, :LANES], tk_ref) * lax.rsqrt(ms + EPS)
            store_keys(kb_ref, knb_ref, [(0, 1)], jnp.concatenate([kb, kb], axis=1))
            store_pair(vb_ref, 0, 1, zz[:, LANES:], 1.0, False)


def _inproj_even(x, g, sc, sh, w, ta, tq, tk, tm):
    b, s, d = x.shape
    n = w.shape[1]
    tab = pl.BlockSpec((tm, 2 * LANES), lambda i, bb: (i, 0))
    mod = pl.BlockSpec((1, 1, d), lambda i, bb: (bb, 0, 0))
    heads = (A_HEADS, A_HEADS, A_HEADS, B_Q_HEADS, B_KV_HEADS, B_KV_HEADS)
    return pl.pallas_call(
        _inproj_even_kernel,
        grid=(s // tm, b),
        in_specs=[pl.BlockSpec((1, tm, d), lambda i, bb: (bb, i, 0)),
                  pl.BlockSpec((1, d), lambda i, bb: (0, 0)),
                  mod, mod,
                  pl.BlockSpec((d, n), lambda i, bb: (0, 0)),
                  tab, tab, tab],
        out_specs=([pl.BlockSpec((1, nh, tm, LANES), lambda i, bb: (bb, 0, i, 0)) for nh in heads]
                   + [pl.BlockSpec((1, nc, 8, LANES), lambda i, bb: (bb, 0, i, 0))
                      for nc in (A_HEADS // 2, B_KV_HEADS // 2)]),
        out_shape=([jax.ShapeDtypeStruct((b, nh, s, LANES), BF16) for nh in heads]
                   + [jax.ShapeDtypeStruct((b, nc, 8 * (s // tm), LANES), F32)
                      for nc in (A_HEADS // 2, B_KV_HEADS // 2)]),
        name="inproj_even",
        compiler_params=_cparams("parallel", "parallel"),
    )(x, g, sc, sh, w, ta, tq, tk)


def _rope_tables(s, qk_g):
    pos = jnp.arange(s)

    def angles(p, dim):
        inv_freq = ROPE_THETA ** (-jnp.arange(0, dim, 2, dtype=F32) / dim)
        return p.astype(F32)[:, None] * inv_freq[None, :]

    a1 = angles(pos, HEAD_DIM)
    cos1, sin1 = jnp.cos(a1), jnp.sin(a1)
    direct = jnp.concatenate([cos1, cos1], axis=-1)
    swapped = jnp.concatenate([-sin1, sin1], axis=-1)
    ta = jnp.concatenate([direct, direct, swapped, swapped], axis=-1)

    ar = angles(pos // GRID_W, HEAD_DIM // 2)
    ac = angles(pos % GRID_W, HEAD_DIM // 2)
    cosb = jnp.concatenate([jnp.cos(ar), jnp.cos(ar), jnp.cos(ac), jnp.cos(ac)], axis=-1)
    sinb = jnp.concatenate([-jnp.sin(ar), jnp.sin(ar), -jnp.sin(ac), jnp.sin(ac)], axis=-1)

    def gained(gain, scale):
        gain = gain.astype(F32)
        gswap = gain.reshape(2, 2, HEAD_DIM // 4)[:, ::-1].reshape(HEAD_DIM)
        direct = cosb * gain * scale
        swapped = sinb * gswap * scale
        return jnp.concatenate([direct, direct, swapped, swapped], axis=-1)

    return ta, gained(qk_g[0], HEAD_DIM ** -0.5), gained(qk_g[1], 1.0)


LOG2E = 1.4426950408889634
LN2 = 0.6931471805599453
BOUND_SLACK = 1.02
MIN_DENOM = 2.0 ** -60


def _ones_lane(upper):
    return ONES_LANE_UPPER if upper else ONES_LANE_LOWER


def _query_coefficients(kn, upper, extra=0.0):
    ms = jnp.max(kn, axis=0, keepdims=True)
    kmax = jnp.sqrt(HEAD_DIM * (ms[:, :1] + ms[:, HEAD_DIM:HEAD_DIM + 1]) + 1.0)
    lane = lax.broadcasted_iota(jnp.int32, (1, LANES), 1)
    ones = lane == _ones_lane(upper)
    return (jnp.where(ones, -LOG2E * BOUND_SLACK * kmax, LOG2E),
            jnp.where(ones, -LOG2E * extra, 0.0))


def _bounded_queries(q, coefficients):
    scale, offset = coefficients
    return (q.astype(F32) * scale + offset).astype(BF16)


def _online_step(q, k, v, bias, m_ref, acc_ref):
    s = _dot_nt(q, k)
    if bias is not None:
        s = s + bias
    m_prev = m_ref[...]
    m_new = jnp.maximum(m_prev, jnp.max(s, axis=-1, keepdims=True))
    p = jnp.exp(s - m_new).astype(BF16)
    acc_ref[...] = jnp.exp(m_prev - m_new) * acc_ref[...] + jnp.dot(p, v, preferred_element_type=F32)
    m_ref[...] = m_new


def _online_init(m_ref, acc_ref):
    m_ref[...] = jnp.full(m_ref.shape, -jnp.inf, F32)
    acc_ref[...] = jnp.zeros(acc_ref.shape, F32)


def _denominators(acc, upper):
    lane = _ones_lane(upper)
    return acc[:, lane:lane + 1]


def _normalised_pair(acc_lower, acc_upper):
    lane = lax.broadcasted_iota(jnp.int32, acc_lower.shape, 1)
    return jnp.where(lane < HEAD_DIM, acc_lower / _denominators(acc_lower, False),
                     acc_upper / _denominators(acc_upper, True))


def _gqa_kernel(q_ref, k_ref, v_ref, kn_ref, o_ref, qa_ref, acc_ref, m_ref, *, tk, rsub):
    nh, tq, dp = q_ref.shape[1:]
    gsz = nh // 2
    grows = gsz * tq
    nkv = k_ref.shape[2] // tk

    def group_q(g):
        return q_ref[0, g * gsz:(g + 1) * gsz].reshape(grows, dp)

    for g in range(2):
        qa_ref[g * grows:(g + 1) * grows, :] = _bounded_queries(
            group_q(g), _query_coefficients(kn_ref[0, 0], g == 1))
    acc_ref[...] = jnp.zeros(acc_ref.shape, F32)

    def bounded(j, carry):
        keys = pl.ds(pl.multiple_of(j * tk, tk), tk)
        subtiles = [(g, pl.ds(g * grows + r * rsub, rsub))
                    for g in range(2) for r in range(grows // rsub)]
        scores = [_dot_nt(qa_ref[rr, :], k_ref[0, g, keys, :]) for g, rr in subtiles]
        weights = [jnp.exp2(s).astype(BF16) for s in scores]
        for (g, rr), p in zip(subtiles, weights):
            acc_ref[rr, :] += jnp.dot(p, v_ref[0, g, keys, :], preferred_element_type=F32)
        return carry

    lax.fori_loop(0, nkv, bounded, 0, unroll=2 if nkv % 2 == 0 else 1)

    smallest =jnp.minimum(jnp.min(_denominators(acc_ref[:grows, :], False)),
                           jnp.min(_denominators(acc_ref[grows:, :], True)))

    @pl.when(smallest < MIN_DENOM)
    def _():
        _online_init(m_ref, acc_ref)

        def online(j, carry):
            keys = pl.ds(pl.multiple_of(j * tk, tk), tk)
            for g in range(2):
                rr = pl.ds(g * grows, grows)
                _online_step(group_q(g), k_ref[0, g, keys, :], v_ref[0, g, keys, :], None,
                             m_ref.at[rr, :], acc_ref.at[rr, :])
            return carry

        lax.fori_loop(0, nkv, online, 0)

    for j in range(gsz):
        lo = acc_ref[j * tq:(j + 1) * tq, :]
        up = acc_ref[grows + j * tq:grows + (j + 1) * tq, :]
        o_ref[0, :, j * LANES:(j + 1) * LANES] = _normalised_pair(lo, up).astype(o_ref.dtype)


def _gqa_attention(q, k, v, kn, tq, tk):
    b, hq, s, dp = q.shape
    assert k.shape[1] == 2 and hq % 2 == 0
    rows = hq * tq
    kv = pl.BlockSpec((1, 2, s, dp), lambda bb, i: (bb, 0, 0, 0))
    return pl.pallas_call(
        functools.partial(_gqa_kernel, tk=tk, rsub=min(256, rows // 2)),
        grid=(b, s // tq),
        in_specs=[pl.BlockSpec((1, hq, tq, dp), lambda bb, i: (bb, 0, i, 0)), kv, kv,
                  pl.BlockSpec((1,) + kn.shape[1:], lambda bb, i: (bb, 0, 0, 0))],
        out_specs=pl.BlockSpec((1, tq, hq * HEAD_DIM), lambda bb, i: (bb, i, 0)),
        out_shape=jax.ShapeDtypeStruct((b, s, hq * HEAD_DIM), BF16),
        scratch_shapes=[pltpu.VMEM((rows, dp), BF16), pltpu.VMEM((rows, dp), F32),
                        pltpu.VMEM((rows, 1), F32)],
        name="gqa_attention",
        compiler_params=_cparams("parallel", "parallel"),
    )(q, k, v, kn)


def _band_reach(t):
    return -(-max(w // 2 for w, _ in A_PATTERNS) // t)


def _band_bias(t, nproc):
    offs = range(-(nproc - 1), nproc)
    a = np.arange(t)[:, None]
    bcol = np.arange(t)[None, :]
    out = np.zeros((len(offs), t, t), np.float64)
    for n, d in enumerate(offs):
        j = d * t + bcol - a
        mult = np.zeros((t, t), np.float64)
        for window, dil in A_PATTERNS:
            mult += ((j % dil) == 0) & (np.abs(j) <= window // 2)
        out[n] = np.where(mult > 0, LOG2E * np.log(np.maximum(mult, 1.0)), NEG)
    return jnp.asarray(out, F32)


def _band_kernel(q_ref, k_ref, v_ref, kn_ref, bias_ref, o_ref, qa_ref, acc_ref, m_ref, *,
                 t, nproc, rsub):
    nsplit = q_ref.shape[2] // t
    nblk = k_ref.shape[2] // t
    max_bias = float(np.log(len(A_PATTERNS)))
    subtiles = [pl.ds(r * rsub, rsub) for r in range(t // rsub)]

    def block_index(u):
        return pl.program_id(2) * nsplit + u

    def key_block(u, n):
        i = block_index(u)
        kb = jnp.clip(i - _band_reach(t), 0, nblk - nproc) + n
        return pl.ds(pl.multiple_of(kb * t, t), t), bias_ref.at[kb - i + nproc - 1]

    def store(u):
        o_ref[0, u * t:(u + 1) * t, :] = _normalised_pair(
            acc_ref[u, 0], acc_ref[u, 1]).astype(o_ref.dtype)

    smallest = []
    for u in range(nsplit):
        rows = slice(u * t, (u + 1) * t)
        for h in range(2):
            qa_ref[u, h] = _bounded_queries(
                q_ref[0, h, rows, :], _query_coefficients(kn_ref[0, 0], h == 1, max_bias))
            for n in range(nproc):
                keys, bias = key_block(u, n)
                scores = [_dot_nt(qa_ref[u, h, rr, :], k_ref[0, h, keys, :]) + bias[rr, :]
                          for rr in subtiles]
                weights = [jnp.exp2(s).astype(BF16) for s in scores]
                for rr, p in zip(subtiles, weights):
                    update = jnp.dot(p, v_ref[0, h, keys, :], preferred_element_type=F32)
                    if n == 0:
                        acc_ref[u, h, rr, :] = update
                    else:
                        acc_ref[u, h, rr, :] += update
        smallest.append(jnp.minimum(jnp.min(_denominators(acc_ref[u, 0], False)),
                                    jnp.min(_denominators(acc_ref[u, 1], True))))
        store(u)

    for u in range(nsplit):
        @pl.when(smallest[u] < MIN_DENOM)
        def _():
            for h in range(2):
                acc = acc_ref.at[u, h]
                _online_init(m_ref, acc)

                def online(n, carry):
                    keys, bias = key_block(u, n)
                    _online_step(q_ref[0, h, u * t:(u + 1) * t, :], k_ref[0, h, keys, :],
                                 v_ref[0, h, keys, :], LN2 * bias[...], m_ref, acc)
                    return carry

                lax.fori_loop(0, nproc, online, 0)
            store(u)


def _dilated_attention(q, k, v, kn, t):
    b, h, s, dp = q.shape
    nproc = min(2 * _band_reach(t) + 1, s // t)
    bias = _band_bias(t, nproc)
    nsplit = max(n for n in (1, 2, 4) if s % (n * t) == 0)
    rows = nsplit * t
    kv = pl.BlockSpec((1, 2, s, dp), lambda bb, hh, i: (bb, hh, 0, 0))
    return pl.pallas_call(
        functools.partial(_band_kernel, t=t, nproc=nproc, rsub=min(256, t)),
        grid=(b, h // 2, s // rows),
        in_specs=[pl.BlockSpec((1, 2, rows, dp), lambda bb, hh, i: (bb, hh, i, 0)), kv, kv,
                  pl.BlockSpec((1, 1) + kn.shape[2:], lambda bb, hh, i: (bb, hh, 0, 0)),
                  pl.BlockSpec(bias.shape, lambda bb, hh, i: (0, 0, 0),
                               pipeline_mode=pl.Buffered(1))],
        out_specs=pl.BlockSpec((1, rows, LANES), lambda bb, hh, i: (bb, i, hh)),
        out_shape=jax.ShapeDtypeStruct((b, s, h * HEAD_DIM), BF16),
        scratch_shapes=[pltpu.VMEM((nsplit, 2, t, dp), BF16), pltpu.VMEM((nsplit, 2, t, dp), F32),
                        pltpu.VMEM((t, 1), F32)],
        name="dilated_attention",
        compiler_params=_cparams("parallel", "parallel", "parallel"),
    )(q, k, v, kn, bias)


def _residual_update(x, y, gate, norm_g):
    return x + gate * (_rms_rows(y) * norm_g)


def _mixer_output_even(rows, oa_ref, ob_ref):
    return oa_ref[0, rows, :], ob_ref[0, rows, :]


def _mixer_output_odd(rows, hf_ref, hb_ref, og_ref, d_ref, mhg_ref):
    ones_blk = _head_block_ones()
    parts = []
    for c in range(HALF_W // LANES):
        cols = slice(c * LANES, (c + 1) * LANES)
        hs = hf_ref[0, rows, cols].astype(F32) + hb_ref[0, rows, cols].astype(F32)
        hn = hs * lax.rsqrt(_head_mean_square(hs, ones_blk) + EPS) * mhg_ref[:, cols]
        parts.append((hn * jax.nn.sigmoid(og_ref[0, rows, cols].astype(F32))).astype(BF16))
    return jnp.concatenate(parts, axis=-1), d_ref[0, rows, :]


def _layer_tail_kernel(*refs, odd, tf, nsplit):
    n_mix = 5 if odd else 2
    mixer_output = _mixer_output_odd if odd else _mixer_output_even
    (wo_ref, x_ref, g1_ref, ng1_ref, ng2_ref, sc_ref, sh_ref, w1_ref, w2_ref, g2_ref, ng3_ref,
     o_ref, a_ref) = refs[n_mix:]
    tm = x_ref.shape[1]
    groups = [slice(r * tm // nsplit, (r + 1) * tm // nsplit) for r in range(nsplit)]
    dot = functools.partial(jnp.dot, preferred_element_type=F32)

    mixes = [mixer_output(rows, *refs[:n_mix]) for rows in groups]
    ys = [dot(m[0], wo_ref[:HALF_W, :]) + dot(m[1], wo_ref[HALF_W:, :]) for m in mixes]
    x1s = [_residual_update(x_ref[0, rows, :], y, g1_ref[0], ng1_ref[...])
           for rows, y in zip(groups, ys)]
    hs = [(_rms_rows(x1) * (ng2_ref[...] * (1.0 + sc_ref[0])) + sh_ref[0]).astype(BF16)
          for x1 in x1s]
    for rows, h in zip(groups, hs):
        for kc in range(w1_ref.shape[1] // tf):
            cols = slice(kc * tf, (kc + 1) * tf)
            a = jnp.maximum(dot(h, w1_ref[:, cols]), 0.0)
            a_ref[rows, cols] = (a * a).astype(BF16)
    y2s = [dot(a_ref[rows, :], w2_ref[...]) for rows in groups]
    for rows, x1, y2 in zip(groups, x1s, y2s):
        o_ref[0, rows, :] = _residual_update(x1, y2, g2_ref[0], ng3_ref[...])


def _layer_tail(mix_inputs, mix_specs, odd, wo, x, g1, ng1, ng2, sc, sh, w1, w2, g2, ng3, tm, tf):
    b, s, d = x.shape
    f = w1.shape[1]
    full = pl.BlockSpec((1, tm, d), lambda bb, i: (bb, i, 0))
    mod = pl.BlockSpec((1, 1, d), lambda bb, i: (bb, 0, 0))
    vec = pl.BlockSpec((1, d), lambda bb, i: (0, 0))

    def resident(shape):
        return pl.BlockSpec(shape, lambda bb, i: (0, 0), pipeline_mode=pl.Buffered(1))

    return pl.pallas_call(
        functools.partial(_layer_tail_kernel, odd=odd, tf=tf, nsplit=2),
        grid=(b, s // tm),
        in_specs=mix_specs + [resident((d, d)), full, mod, vec, vec, mod, mod,
                              resident((d, f)), resident((f, d)), mod, vec],
        out_specs=full,
        out_shape=jax.ShapeDtypeStruct((b, s, d), F32),
        scratch_shapes=[pltpu.VMEM((tm, f), BF16)],
        name="layer_tail_odd" if odd else "layer_tail_even",
        compiler_params=_cparams("parallel", "parallel"),
    )(*mix_inputs, wo, x, g1, ng1, ng2, sc, sh, w1, w2, g2, ng3)


def _layer_tail_even(oa, ob, *rest, tm, tf):
    half = pl.BlockSpec((1, tm, HALF_W), lambda bb, i: (bb, i, 0))
    return _layer_tail([oa, ob], [half, half], False, *rest, tm, tf)


def _layer_tail_odd(hf, hb, z, dout, mhg, *rest, tm, tf):
    half = pl.BlockSpec((1, tm, HALF_W), lambda bb, i: (bb, i, 0))
    og = pl.BlockSpec((1, tm, HALF_W), lambda bb, i: (bb, i, 3))
    return _layer_tail([hf, hb, z, dout, mhg],
                       [half, half, og, half, pl.BlockSpec((1, HALF_W), lambda bb, i: (0, 0))],
                       True, *rest, tm, tf)


def _inproj_odd_kernel(x_ref, g_ref, sc_ref, sh_ref, w_ref, gb_ref, z_ref, gates_ref):
    x = x_ref[0]
    h = _rms_rows(x) * (g_ref[...] * (1.0 + sc_ref[0])) + sh_ref[0]
    hb = h.astype(BF16)
    k_scale = HEAD_DIM ** -0.5
    projections = [jnp.dot(hb, w_ref[:, j * MXU_N:(j + 1) * MXU_N], preferred_element_type=F32)
                   for j in range(CD_MAIN // MXU_N)]
    gates_ref[0] = jnp.dot(hb, w_ref[:, CD_MAIN:], preferred_element_type=F32) + gb_ref[...]
    for j, zz in enumerate(projections):
        if HALF_W <= j * MXU_N < 2 * HALF_W:
            zz = zz * k_scale
        z_ref[0, :, j * MXU_N:(j + 1) * MXU_N] = zz.astype(BF16)


def _inproj_odd(x, g, sc, sh, w, gb, tm):
    b, s, d = x.shape
    n = w.shape[1]
    mod = pl.BlockSpec((1, 1, d), lambda i, bb: (bb, 0, 0))
    return pl.pallas_call(
        _inproj_odd_kernel,
        grid=(s // tm, b),
        in_specs=[pl.BlockSpec((1, tm, d), lambda i, bb: (bb, i, 0)),
                  pl.BlockSpec((1, d), lambda i, bb: (0, 0)),
                  mod, mod,
                  pl.BlockSpec((d, n), lambda i, bb: (0, 0)),
                  pl.BlockSpec((1, GATE_W), lambda i, bb: (0, 0))],
        out_specs=[pl.BlockSpec((1, tm, CD_MAIN), lambda i, bb: (bb, i, 0)),
                   pl.BlockSpec((1, tm, GATE_W), lambda i, bb: (bb, i, 0))],
        out_shape=[jax.ShapeDtypeStruct((b, s, CD_MAIN), BF16),
                   jax.ShapeDtypeStruct((b, s, GATE_W), F32)],
        name="inproj_odd",
        compiler_params=_cparams("parallel", "parallel"),
    )(x, g, sc, sh, w, gb)


def _gate_lane(rev, head):
    return (C_HEADS if rev else 0) + head


def _mlstm_constants(rev, lc):
    wq = QUAD * HEAD_DIM
    t = np.arange(lc)
    sees = (t[None, :] >= t[:, None]) if rev else (t[None, :] <= t[:, None])
    nq = C_HEADS // QUAD
    sel_s = np.zeros((nq, LANES, QUAD * lc), np.float32)
    sel_v = np.zeros((nq, LANES, wq), np.float32)
    for qd in range(nq):
        for hh in range(QUAD):
            lane = _gate_lane(rev, QUAD * qd + hh)
            sel_s[qd, lane, hh * lc:(hh + 1) * lc] = 1.0
            sel_v[qd, lane, hh * HEAD_DIM:(hh + 1) * HEAD_DIM] = 1.0
    head_of_row = np.arange(QUAD * lc) // lc
    head_of_col = np.arange(wq) // HEAD_DIM
    return dict(
        tri=jnp.asarray(sees, BF16),
        sel_s=jnp.asarray(sel_s, BF16), sel_v=jnp.asarray(sel_v, BF16),
        ones_bd=jnp.asarray(np.transpose(sel_s, (0, 2, 1)), BF16),
        row_head=jnp.asarray(head_of_row[:, None] == head_of_col[None, :], BF16),
        diag=jnp.asarray(head_of_col[:, None] == head_of_col[None, :], F32),
        causal=jnp.asarray(np.tile(sees, (1, QUAD)), F32))


def _scan_max(x, rev):
    n = x.shape[0]
    row = lax.broadcasted_iota(jnp.int32, x.shape, 0)
    sh = 1
    while sh < n:
        if rev:
            x = jnp.maximum(x, jnp.where(row < n - sh, pltpu.roll(x, n - sh, axis=0), -jnp.inf))
        else:
            x = jnp.maximum(x, jnp.where(row >= sh, pltpu.roll(x, sh, axis=0), -jnp.inf))
        sh *= 2
    return x


def _mlstm_kernel(*refs, lc):
    nd = len(_MLSTM_DIR_CONSTS)
    blocks = (refs[0:4], refs[4:8])
    rowhead_ref, diag_ref = refs[8:10]
    consts = (refs[10:10 + nd], refs[10 + nd:10 + 2 * nd])
    outs = refs[10 + 2 * nd:12 + 2 * nd]
    c_sc, n_sc, m_sc = refs[12 + 2 * nd:]

    @pl.when(pl.program_id(1) == 0)
    def _():
        c_sc[...] = jnp.zeros(c_sc.shape, F32)
        n_sc[...] = jnp.zeros(n_sc.shape, F32)
        m_sc[...] = jnp.zeros(m_sc.shape, F32)

    nch = outs[0].shape[1] // lc

    wq = QUAD * HEAD_DIM
    edges = (lc - 1, 0)
    jobs = [(rev, qd) for rev in (0, 1) for qd in range(C_HEADS // QUAD)]
    dot = functools.partial(jnp.dot, preferred_element_type=F32)

    def chunk(ci, carry):
        rows = (pl.ds(pl.multiple_of(ci * lc, lc), lc),
                pl.ds(pl.multiple_of((nch - 1 - ci) * lc, lc), lc))
        gate = []
        for rev in (0, 1):
            g_ref = blocks[rev][3]
            tri_ref = consts[rev][0]
            gi = g_ref[0, rows[rev], :LANES]
            bcum = _split_dot_rhs(tri_ref[...], _log_sigmoid(g_ref[0, rows[rev], LANES:]))
            a = gi - bcum
            m_prev = m_sc[rev]
            mt_b = jnp.maximum(m_prev, _scan_max(a, bool(rev))).astype(BF16)
            mt = mt_b.astype(F32)
            w_b = jnp.exp(m_prev - mt).astype(BF16)
            b_all = bcum[edges[rev]:edges[rev] + 1, :]
            log_w = b_all + a
            m_new = jnp.maximum(b_all + m_prev, jnp.max(log_w, axis=0, keepdims=True))
            gate.append(dict(
                mt_b=mt_b, w_b=w_b, w=w_b.astype(F32),
                floor=jnp.exp(-(bcum + mt)),
                wk_b=jnp.exp(log_w - m_new).astype(BF16),
                decay=jnp.broadcast_to(jnp.exp(b_all + m_prev - m_new), (8, LANES)),
                a_t=a.T))
            m_sc[rev] = m_new

        row_head = rowhead_ref[...]

        def operand(rev, qd, j):
            return blocks[rev][j][0, rows[rev], qd * wq:(qd + 1) * wq]

        def sel_v(rev, qd):
            return consts[rev][2][qd]

        scores, log_e = [], []
        for rev, qd in jobs:
            kbd = jnp.concatenate([operand(rev, qd, 1)] * QUAD, axis=0) * row_head
            scores.append(_dot_nt(operand(rev, qd, 0), kbd))
            a_rows = jnp.concatenate(
                [jnp.broadcast_to(gate[rev]["a_t"][ln:ln + 1, :], (lc, lc))
                 for ln in (_gate_lane(bool(rev), QUAD * qd + hh) for hh in range(QUAD))], axis=1)
            log_e.append(jnp.where(consts[rev][4][...] > 0.0,
                                   a_rows - dot(gate[rev]["mt_b"], consts[rev][1][qd]), NEG))
        probs = [(s * jnp.exp(e)).astype(BF16) for s, e in zip(scores, log_e)]

        pvs, qcs = [], []
        for (rev, qd), p in zip(jobs, probs):
            v = operand(rev, qd, 2)
            vaug = jnp.concatenate(
                [jnp.concatenate([v] * QUAD, axis=0) * row_head, consts[rev][3][qd]], axis=1)
            pvs.append(dot(p, vaug))
            caug = jnp.concatenate(
                [c_sc[rev, qd].astype(BF16),
                 (sel_v(rev, qd).astype(F32) * n_sc[rev, qd]).astype(BF16)], axis=0)
            qcs.append(_dot_nt(operand(rev, qd, 0), caug))

        for (rev, qd), pv, qc in zip(jobs, pvs, qcs):
            g = gate[rev]
            den = pv[:, wq:] + g["w"] * qc[:, wq:]
            scale_b = (1.0 / jnp.maximum(jnp.abs(den), g["floor"])).astype(BF16)
            h_out = dot(scale_b, sel_v(rev, qd)) * (
                pv[:, :wq] + dot(g["w_b"], sel_v(rev, qd)) * qc[:, :wq])
            outs[rev][0, rows[rev], qd * wq:(qd + 1) * wq] = h_out.astype(outs[rev].dtype)

        for rev, qd in jobs:
            g = gate[rev]
            k = operand(rev, qd, 1)
            wk_v = dot(g["wk_b"], sel_v(rev, qd))
            dec_v = _split_dot(g["decay"], sel_v(rev, qd))[:1]
            vw = (wk_v * operand(rev, qd, 2).astype(F32)).astype(BF16)
            c_sc[rev, qd] = dec_v * c_sc[rev, qd] + diag_ref[...] * _dot_tn(vw, k)
            n_sc[rev, qd] = (dec_v * n_sc[rev, qd]
                             + jnp.sum(wk_v * k.astype(F32), axis=0, keepdims=True))
        return carry

    lax.fori_loop(0, nch, chunk, 0, unroll=2 if nch % 2 == 0 else 1)


_MLSTM_DIR_CONSTS = ("tri", "sel_s", "sel_v", "ones_bd", "causal")


def _mlstm(z, gates, tb, lc):
    b, s, _ = z.shape
    nblk = s // tb
    position = (lambda i: i, lambda i: nblk - 1 - i)

    def col(rev, j):
        return pl.BlockSpec((1, tb, HALF_W), lambda bb, i: (bb, position[rev](i), j))

    def block_specs(rev):
        return [col(rev, 0), col(rev, 1), col(rev, 2),
                pl.BlockSpec((1, tb, GATE_W), lambda bb, i: (bb, position[rev](i), 0))]

    def const_spec(a):
        return pl.BlockSpec(a.shape, lambda bb, i: (0,) * a.ndim)

    consts = (_mlstm_constants(False, lc), _mlstm_constants(True, lc))
    shared = [consts[0]["row_head"], consts[0]["diag"]]
    per_dir = [consts[rev][n] for rev in (0, 1) for n in _MLSTM_DIR_CONSTS]
    nq = C_HEADS // QUAD
    wq = QUAD * HEAD_DIM
    return pl.pallas_call(
        functools.partial(_mlstm_kernel, lc=lc),
        grid=(b, nblk),
        in_specs=block_specs(0) + block_specs(1) + [const_spec(a) for a in shared + per_dir],
        out_specs=[col(0, 0), col(1, 0)],
        out_shape=[jax.ShapeDtypeStruct((b, s, HALF_W), BF16)] * 2,
        scratch_shapes=[pltpu.VMEM((2, nq, wq, wq), F32), pltpu.VMEM((2, nq, 1, wq), F32),
                        pltpu.VMEM((2, 1, LANES), F32)],
        name="mlstm",
        compiler_params=_cparams("parallel", "arbitrary"),
    )(z, z, z, gates, z, z, z, gates, *shared, *per_dir)


def _sgu_kernel(u_ref, v_ref, lng_ref, w_ref, b_ref, o_ref):
    tb = u_ref.shape[1]
    first_group = lax.broadcasted_iota(jnp.int32, (D_CHUNK, LANES), 1) < HEAD_DIM
    for ch in range(tb // D_CHUNK):
        rows = slice(ch * D_CHUNK, (ch + 1) * D_CHUNK)
        u = jax.nn.gelu(u_ref[0, rows, :].astype(F32))
        v = jax.nn.gelu(v_ref[0, rows, :].astype(F32))
        dv = v - jnp.mean(v, axis=-1, keepdims=True)
        vn = (dv * lax.rsqrt(jnp.mean(dv * dv, axis=-1, keepdims=True) + EPS)
              * lng_ref[...]).astype(BF16)
        mixed = []
        for c in range(HALF_W // LANES):
            pair = jnp.dot(w_ref[2 * c * D_CHUNK:(2 * c + 2) * D_CHUNK, :],
                           vn[:, c * LANES:(c + 1) * LANES], preferred_element_type=F32)
            mixed.append(jnp.where(first_group, pair[:D_CHUNK], pair[D_CHUNK:]))
        sg = jnp.concatenate(mixed, axis=1) + b_ref[...]
        o_ref[0, rows, :] = (u * sg).astype(BF16)


def _sgu(z, lng, w, bexp, tb):
    b, s, _ = z.shape
    col = lambda j: pl.BlockSpec((1, tb, HALF_W), lambda bb, i: (bb, i, j))
    return pl.pallas_call(
        _sgu_kernel,
        grid=(b, s // tb),
        in_specs=[col(4), col(5),
                  pl.BlockSpec((1, HALF_W), lambda bb, i: (0, 0)),
                  pl.BlockSpec((D_GROUPS * D_CHUNK, D_CHUNK), lambda bb, i: (0, 0)),
                  pl.BlockSpec((D_CHUNK, HALF_W), lambda bb, i: (0, 0))],
        out_specs=col(0),
        out_shape=jax.ShapeDtypeStruct((b, s, HALF_W), BF16),
        name="sgu",
        compiler_params=_cparams("parallel", "parallel"),
    )(z, z, lng, w, bexp)


def _tiles(s):
    return dict(tm=min(512, s), tf=1024, tq=min(256, s), tk=min(1024, s), tband=min(512, s),
                tb_mlstm=min(1024, s), lc=128, tb_sgu=min(512, s))


def _gqa_head_order():
    half = B_Q_HEADS // 2
    return [h for j in range(half) for h in (j, half + j)]


def _trunk(x, c, p):
    b, s, d = x.shape
    t = _tiles(s)
    mod = _adaln(c, p["w_ada"], p["b_ada"])
    ta, tq, tk = _rope_tables(s, p["qk_norm_g"])
    for layer in range(2):
        m = mod[:, layer * 6 * d:(layer + 1) * 6 * d].reshape(b, 6, 1, d)
        sh1, sc1, g1, sh2, sc2, g2 = (m[:, j] for j in range(6))
        ng = p["norm_g"][layer].reshape(4, 1, d)
        tail = (x, g1, ng[1], ng[2], sc2, sh2, p["w_ff1"][layer], p["w_ff2"][layer], g2, ng[3])
        if layer == 0:
            qa, ka, va, qb, kb, vb, kna, knb = _inproj_even(x, ng[0], sc1, sh1, p["w_in_ab"],
                                                            ta, tq, tk, t["tm"])
            oa = _dilated_attention(qa, ka, va, kna, t["tband"])
            ob = _gqa_attention(qb, kb, vb, knb, t["tq"], t["tk"])
            x = _layer_tail_even(oa, ob, p["w_out_ab"], *tail, tm=t["tm"], tf=t["tf"])
        else:
            z, gates = _inproj_odd(x, ng[0], sc1, sh1, p["w_in_cd"], p["gate_bias"], t["tm"])
            hf, hb = _mlstm(z, gates, t["tb_mlstm"], t["lc"])
            dout = _sgu(z, p["sg_norm_g"], p["w_spatial"], p["b_spatial"], t["tb_sgu"])
            x = _layer_tail_odd(hf, hb, z, dout, p["mh_norm_g"], p["w_out_cd"], *tail,
                                tm=t["tm"], tf=t["tf"])
    return x


def _prepare_params(w_in_ab, w_out_ab, qk_norm_g, w_in_cd, w_out_cd, gate_bias, mh_norm_g,
                    sg_norm_g, w_spatial, b_spatial, w_ada, b_ada, norm_g, w_ff1, w_ff2):
    depth, d, _ = w_ada.shape
    wcd = w_in_cd[0]
    gate_lo = 4 * HALF_W
    gate_hi = gate_lo + N_GATES

    def gate_tiles(g):
        i_fw, f_fw, i_bw, f_bw = (g[:, j * C_HEADS:(j + 1) * C_HEADS] for j in range(4))
        pad = jnp.zeros((g.shape[0], LANES - 2 * C_HEADS), g.dtype)
        return jnp.concatenate([i_fw, i_bw, pad, f_fw, f_bw, pad], axis=1)

    wcd = jnp.concatenate([wcd[:, :gate_lo], wcd[:, gate_hi:],
                           gate_tiles(wcd[:, gate_lo:gate_hi])], axis=1)
    gb = gate_tiles(gate_bias[0].reshape(1, N_GATES).astype(F32))
    bexp = jnp.repeat(b_spatial[0].astype(F32).T, HEAD_DIM, axis=1)

    qb_lo = 3 * HALF_W
    order = np.asarray(_gqa_head_order())
    head_cols = (order[:, None] * HEAD_DIM + np.arange(HEAD_DIM)[None, :]).reshape(-1)
    wab = w_in_ab[0]
    wab = jnp.concatenate([wab[:, :qb_lo], wab[:, qb_lo + head_cols], wab[:, qb_lo + HALF_W:]],
                          axis=1)
    wout_ab = w_out_ab[0]
    wout_ab = jnp.concatenate([wout_ab[:HALF_W], wout_ab[HALF_W + head_cols]], axis=0)
    return dict(
        w_ada=jnp.concatenate([w_ada[l] for l in range(depth)], axis=1).astype(BF16),
        b_ada=b_ada.reshape(1, -1).astype(F32),
        w_in_ab=wab.astype(BF16), w_out_ab=wout_ab.astype(BF16),
        qk_norm_g=qk_norm_g[0],
        w_in_cd=wcd.astype(BF16), w_out_cd=w_out_cd[0].astype(BF16), gate_bias=gb,
        mh_norm_g=mh_norm_g[0].reshape(1, HALF_W).astype(F32),
        sg_norm_g=sg_norm_g[0].reshape(1, HALF_W).astype(F32),
        w_spatial=w_spatial[0].reshape(D_GROUPS * D_CHUNK, D_CHUNK).astype(BF16),
        b_spatial=bexp,
        norm_g=norm_g.astype(F32), w_ff1=w_ff1.astype(BF16), w_ff2=w_ff2.astype(BF16))


def kernel(x_prompt, x_sample, c_prompt, c_sample, w_in_ab, w_out_ab, qk_norm_g, w_in_cd, w_out_cd, gate_bias, mh_norm_g, sg_norm_g, w_spatial, b_spatial, w_ada, b_ada, norm_g, w_ff1, w_ff2):
    p = _prepare_params(w_in_ab, w_out_ab, qk_norm_g, w_in_cd, w_out_cd, gate_bias, mh_norm_g,
                        sg_norm_g, w_spatial, b_spatial, w_ada, b_ada, norm_g, w_ff1, w_ff2)
    return (_trunk(x_prompt, c_prompt, p), _trunk(x_sample, c_sample, p))
```

```python
import functools

import numpy as np
import jax
import jax.numpy as jnp
from jax import lax
from jax.experimental import pallas as pl
from jax.experimental.pallas import tpu as pltpu

F32 = jnp.float32
BF16 = jnp.bfloat16

D_MODEL = 1024
HEAD_DIM = 64
GRID_W = 64
ROPE_THETA = 10000.0
EPS = 1e-6
A_HEADS = 8
A_PATTERNS = ((128, 1), (512, 4), (2048, 16))
B_Q_HEADS = 8
B_KV_HEADS = 2
C_HEADS = 8
D_GROUPS = 8
D_CHUNK = 128
HALF_W = D_MODEL // 2
AB_IN = 3 * HALF_W + HALF_W + 2 * B_KV_HEADS * HEAD_DIM
CD_MAIN = 6 * HALF_W
N_GATES = 4 * C_HEADS
GATE_W = 256
QUAD = 4
ONES_LANE_LOWER = HEAD_DIM
ONES_LANE_UPPER = 0

LANES = 128
MXU_N = 256
V7X_VMEM_BYTES = 64 * 1024 * 1024
VMEM_LIMIT = 3 * V7X_VMEM_BYTES // 4

NEG = -1e30


def _cparams(*sem):
    return pltpu.CompilerParams(dimension_semantics=sem, vmem_limit_bytes=VMEM_LIMIT)


def _rms_rows(x):
    return x * lax.rsqrt(jnp.mean(x * x, axis=-1, keepdims=True) + EPS)


def _split_dot(a, b):
    hi = a.astype(BF16)
    r1 = a - hi.astype(F32)
    mid = r1.astype(BF16)
    lo = (r1 - mid.astype(F32)).astype(BF16)
    dot = functools.partial(jnp.dot, preferred_element_type=F32)
    return dot(hi, b) + dot(mid, b) + dot(lo, b)


def _split_dot_rhs(b, a):
    hi = a.astype(BF16)
    r1 = a - hi.astype(F32)
    mid = r1.astype(BF16)
    lo = (r1 - mid.astype(F32)).astype(BF16)
    dot = functools.partial(jnp.dot, preferred_element_type=F32)
    return dot(b, hi) + dot(b, mid) + dot(b, lo)


def _head_mean_square(z, ones_blk, precise=True):
    z2 = z * z
    hi = z2.astype(BF16)
    total = jnp.dot(hi, ones_blk, preferred_element_type=F32)
    if precise:
        lo = (z2 - hi.astype(F32)).astype(BF16)
        total = total + jnp.dot(lo, ones_blk, preferred_element_type=F32)
    return total * (1.0 / HEAD_DIM)


def _head_block_ones(width=LANES):
    r = lax.broadcasted_iota(jnp.int32, (width, width), 0) // HEAD_DIM
    c = lax.broadcasted_iota(jnp.int32, (width, width), 1) // HEAD_DIM
    return (r == c).astype(BF16)


def _log_sigmoid(x):
    return jnp.minimum(x, 0.0) - jnp.log1p(jnp.exp(-jnp.abs(x)))


def _dot_nt(a, b):
    return lax.dot_general(a, b, (((1,), (1,)), ((), ())), preferred_element_type=F32)


def _dot_tn(a, b):
    return lax.dot_general(a, b, (((0,), (0,)), ((), ())), preferred_element_type=F32)


def _adaln_kernel(c_ref, w_ref, b_ref, o_ref):
    c = c_ref[...]
    ca = c * jax.nn.sigmoid(c)
    o_ref[...] = jnp.dot(ca.astype(BF16), w_ref[...], preferred_element_type=F32) + b_ref[...]


def _adaln(c, w, b):
    nb, d = c.shape
    n = w.shape[1]
    tn = 2048
    return pl.pallas_call(
        _adaln_kernel,
        grid=(n // tn,),
        in_specs=[pl.BlockSpec((nb, d), lambda j: (0, 0)),
                  pl.BlockSpec((d, tn), lambda j: (0, j)),
                  pl.BlockSpec((1, tn), lambda j: (0, j))],
        out_specs=pl.BlockSpec((nb, tn), lambda j: (0, j)),
        out_shape=jax.ShapeDtypeStruct((nb, n), F32),
        name="adaln",
        compiler_params=_cparams("parallel"),
    )(c, w, b)


def _swap_halves(z, half, upper):
    return jnp.where(upper, pltpu.roll(z, LANES - half, axis=1), pltpu.roll(z, half, axis=1))


def _inproj_even_kernel(x_ref, g_ref, sc_ref, sh_ref, w_ref, ta_ref, tq_ref, tk_ref,
                        qa_ref, ka_ref, va_ref, qb_ref, kb_ref, vb_ref, kna_ref, knb_ref):
    x = x_ref[0]
    h = _rms_rows(x) * (g_ref[...] * (1.0 + sc_ref[0])) + sh_ref[0]
    hb = h.astype(BF16)
    tm = x.shape[0]
    lane = lax.broadcasted_iota(jnp.int32, (tm, LANES), 1)
    up64 = (lane % 64) < 32
    up32 = (lane % 32) < 16
    ones_blk = _head_block_ones(MXU_N)
    q_scale = HEAD_DIM ** -0.5

    def rotary_1d(z):
        return z * ta_ref[:, :LANES] + _swap_halves(z, 32, up64) * ta_ref[:, LANES:]

    def rotary_axial(z, t_ref):
        return z * t_ref[:, :LANES] + _swap_halves(z, 16, up32) * t_ref[:, LANES:]

    def per_chunk(f, zz):
        return jnp.concatenate([f(zz[:, :LANES]), f(zz[:, LANES:])], axis=1)

    lower = lane < HEAD_DIM

    def store_pair(ref, first, second, z, fill, masked):
        if masked:
            lo = jnp.where(lower, z, jnp.where(lane == ONES_LANE_LOWER, fill, 0.0))
            up = jnp.where(lower, jnp.where(lane == ONES_LANE_UPPER, fill, 0.0), z)
        else:
            lo = jnp.where(lane == ONES_LANE_LOWER, fill, z)
            up = jnp.where(lane == ONES_LANE_UPPER, fill, z)
        ref[0, first] = lo.astype(BF16)
        ref[0, second] = up.astype(BF16)

    def store_queries(ref, heads, y):
        sq = HEAD_DIM * _head_mean_square(y, ones_blk, precise=False) + 1e-30
        norm = sq * lax.rsqrt(sq)
        for s, (first, second) in enumerate(heads):
            cols = slice(s * LANES, (s + 1) * LANES)
            store_pair(ref, first, second, y[:, cols],
                       pltpu.roll(norm[:, cols], HEAD_DIM, axis=1), True)

    def store_keys(ref, kn_ref, heads, y):
        ms = jnp.max(_head_mean_square(y, ones_blk, precise=False), axis=0, keepdims=True)
        for s, (first, second) in enumerate(heads):
            cols = slice(s * LANES, (s + 1) * LANES)
            store_pair(ref, first, second, y[:, cols], 1.0, False)
            kn_ref[0, first // 2] = jnp.broadcast_to(ms[:, cols], (8, LANES))

    projections = [jnp.dot(hb, w_ref[:, j * MXU_N:(j + 1) * MXU_N], preferred_element_type=F32)
                   for j in range(AB_IN // MXU_N)]
    half_b = B_Q_HEADS // 2
    for j, zz in enumerate(projections):
        pairs = [(4 * (j % 2), 4 * (j % 2) + 1), (4 * (j % 2) + 2, 4 * (j % 2) + 3)]
        if j < 2:
            store_queries(qa_ref, pairs, per_chunk(rotary_1d, zz) * q_scale)
        elif j < 4:
            store_keys(ka_ref, kna_ref, pairs, per_chunk(rotary_1d, zz))
        elif j < 6:
            for s, (first, second) in enumerate(pairs):
                store_pair(va_ref, first, second, zz[:, s * LANES:(s + 1) * LANES], 1.0, False)
        elif j < 8:
            y = per_chunk(lambda z: rotary_axial(z, tq_ref), zz)
            y = y * lax.rsqrt(_head_mean_square(zz, ones_blk, precise=False) + EPS)
            store_queries(qb_ref, [(2 * (j - 6), 2 * (j - 6) + half_b),
                                   (2 * (j - 6) + 1, 2 * (j - 6) + 1 + half_b)], y)
        else:
            ms = _head_mean_square(zz, ones_blk, precise=False)[:, :LANES]
            kb = rotary_axial(zz[:, :LANES], tk_ref) * lax.rsqrt(ms + EPS)
            store_keys(kb_ref, knb_ref, [(0, 1)], jnp.concatenate([kb, kb], axis=1))
            store_pair(vb_ref, 0, 1, zz[:, LANES:], 1.0, False)


def _inproj_even(x, g, sc, sh, w, ta, tq, tk, tm):
    b, s, d = x.shape
    n = w.shape[1]
    tab = pl.BlockSpec((tm, 2 * LANES), lambda i, bb: (i, 0))
    mod = pl.BlockSpec((1, 1, d), lambda i, bb: (bb, 0, 0))
    heads = (A_HEADS, A_HEADS, A_HEADS, B_Q_HEADS, B_KV_HEADS, B_KV_HEADS)
    return pl.pallas_call(
        _inproj_even_kernel,
        grid=(s // tm, b),
        in_specs=[pl.BlockSpec((1, tm, d), lambda i, bb: (bb, i, 0)),
                  pl.BlockSpec((1, d), lambda i, bb: (0, 0)),
                  mod, mod,
                  pl.BlockSpec((d, n), lambda i, bb: (0, 0)),
                  tab, tab, tab],
        out_specs=([pl.BlockSpec((1, nh, tm, LANES), lambda i, bb: (bb, 0, i, 0)) for nh in heads]
                   + [pl.BlockSpec((1, nc, 8, LANES), lambda i, bb: (bb, 0, i, 0))
                      for nc in (A_HEADS // 2, B_KV_HEADS // 2)]),
        out_shape=([jax.ShapeDtypeStruct((b, nh, s, LANES), BF16) for nh in heads]
                   + [jax.ShapeDtypeStruct((b, nc, 8 * (s // tm), LANES), F32)
                      for nc in (A_HEADS // 2, B_KV_HEADS // 2)]),
        name="inproj_even",
        compiler_params=_cparams("parallel", "parallel"),
    )(x, g, sc, sh, w, ta, tq, tk)


def _rope_tables(s, qk_g):
    pos = jnp.arange(s)

    def angles(p, dim):
        inv_freq = ROPE_THETA ** (-jnp.arange(0, dim, 2, dtype=F32) / dim)
        return p.astype(F32)[:, None] * inv_freq[None, :]

    a1 = angles(pos, HEAD_DIM)
    cos1, sin1 = jnp.cos(a1), jnp.sin(a1)
    direct = jnp.concatenate([cos1, cos1], axis=-1)
    swapped = jnp.concatenate([-sin1, sin1], axis=-1)
    ta = jnp.concatenate([direct, direct, swapped, swapped], axis=-1)

    ar = angles(pos // GRID_W, HEAD_DIM // 2)
    ac = angles(pos % GRID_W, HEAD_DIM // 2)
    cosb = jnp.concatenate([jnp.cos(ar), jnp.cos(ar), jnp.cos(ac), jnp.cos(ac)], axis=-1)
    sinb = jnp.concatenate([-jnp.sin(ar), jnp.sin(ar), -jnp.sin(ac), jnp.sin(ac)], axis=-1)

    def gained(gain, scale):
        gain = gain.astype(F32)
        gswap = gain.reshape(2, 2, HEAD_DIM // 4)[:, ::-1].reshape(HEAD_DIM)
        direct = cosb * gain * scale
        swapped = sinb * gswap * scale
        return jnp.concatenate([direct, direct, swapped, swapped], axis=-1)

    return ta, gained(qk_g[0], HEAD_DIM ** -0.5), gained(qk_g[1], 1.0)


LOG2E = 1.4426950408889634
LN2 = 0.6931471805599453
BOUND_SLACK = 1.02
MIN_DENOM = 2.0 ** -60


def _ones_lane(upper):
    return ONES_LANE_UPPER if upper else ONES_LANE_LOWER


def _query_coefficients(kn, upper, extra=0.0):
    ms = jnp.max(kn, axis=0, keepdims=True)
    kmax = jnp.sqrt(HEAD_DIM * (ms[:, :1] + ms[:, HEAD_DIM:HEAD_DIM + 1]) + 1.0)
    lane = lax.broadcasted_iota(jnp.int32, (1, LANES), 1)
    ones = lane == _ones_lane(upper)
    return (jnp.where(ones, -LOG2E * BOUND_SLACK * kmax, LOG2E),
            jnp.where(ones, -LOG2E * extra, 0.0))


def _bounded_queries(q, coefficients):
    scale, offset = coefficients
    return (q.astype(F32) * scale + offset).astype(BF16)


def _online_step(q, k, v, bias, m_ref, acc_ref):
    s = _dot_nt(q, k)
    if bias is not None:
        s = s + bias
    m_prev = m_ref[...]
    m_new = jnp.maximum(m_prev, jnp.max(s, axis=-1, keepdims=True))
    p = jnp.exp(s - m_new).astype(BF16)
    acc_ref[...] = jnp.exp(m_prev - m_new) * acc_ref[...] + jnp.dot(p, v, preferred_element_type=F32)
    m_ref[...] = m_new


def _online_init(m_ref, acc_ref):
    m_ref[...] = jnp.full(m_ref.shape, -jnp.inf, F32)
    acc_ref[...] = jnp.zeros(acc_ref.shape, F32)


def _denominators(acc, upper):
    lane = _ones_lane(upper)
    return acc[:, lane:lane + 1]


def _normalised_pair(acc_lower, acc_upper):
    lane = lax.broadcasted_iota(jnp.int32, acc_lower.shape, 1)
    return jnp.where(lane < HEAD_DIM, acc_lower / _denominators(acc_lower, False),
                     acc_upper / _denominators(acc_upper, True))


def _gqa_kernel(q_ref, k_ref, v_ref, kn_ref, o_ref, qa_ref, acc_ref, m_ref, *, tk, rsub):
    nh, tq, dp = q_ref.shape[1:]
    gsz = nh // 2
    grows = gsz * tq
    nkv = k_ref.shape[2] // tk

    def group_q(g):
        return q_ref[0, g * gsz:(g + 1) * gsz].reshape(grows, dp)

    for g in range(2):
        qa_ref[g * grows:(g + 1) * grows, :] = _bounded_queries(
            group_q(g), _query_coefficients(kn_ref[0, 0], g == 1))
    acc_ref[...] = jnp.zeros(acc_ref.shape, F32)

    def bounded(j, carry):
        keys = pl.ds(pl.multiple_of(j * tk, tk), tk)
        subtiles = [(g, pl.ds(g * grows + r * rsub, rsub))
                    for g in range(2) for r in range(grows // rsub)]
        scores = [_dot_nt(qa_ref[rr, :], k_ref[0, g, keys, :]) for g, rr in subtiles]
        weights = [jnp.exp2(s).astype(BF16) for s in scores]
        for (g, rr), p in zip(subtiles, weights):
            acc_ref[rr, :] += jnp.dot(p, v_ref[0, g, keys, :], preferred_element_type=F32)
        return carry

    lax.fori_loop(0, nkv, bounded, 0, unroll=2 if nkv % 2 == 0 else 1)

    smallest =jnp.minimum(jnp.min(_denominators(acc_ref[:grows, :], False)),
                           jnp.min(_denominators(acc_ref[grows:, :], True)))

    @pl.when(smallest < MIN_DENOM)
    def _():
        _online_init(m_ref, acc_ref)

        def online(j, carry):
            keys = pl.ds(pl.multiple_of(j * tk, tk), tk)
            for g in range(2):
                rr = pl.ds(g * grows, grows)
                _online_step(group_q(g), k_ref[0, g, keys, :], v_ref[0, g, keys, :], None,
                             m_ref.at[rr, :], acc_ref.at[rr, :])
            return carry

        lax.fori_loop(0, nkv, online, 0)

    for j in range(gsz):
        lo = acc_ref[j * tq:(j + 1) * tq, :]
        up = acc_ref[grows + j * tq:grows + (j + 1) * tq, :]
        o_ref[0, :, j * LANES:(j + 1) * LANES] = _normalised_pair(lo, up).astype(o_ref.dtype)


def _gqa_attention(q, k, v, kn, tq, tk):
    b, hq, s, dp = q.shape
    assert k.shape[1] == 2 and hq % 2 == 0
    rows = hq * tq
    kv = pl.BlockSpec((1, 2, s, dp), lambda bb, i: (bb, 0, 0, 0))
    return pl.pallas_call(
        functools.partial(_gqa_kernel, tk=tk, rsub=min(256, rows // 2)),
        grid=(b, s // tq),
        in_specs=[pl.BlockSpec((1, hq, tq, dp), lambda bb, i: (bb, 0, i, 0)), kv, kv,
                  pl.BlockSpec((1,) + kn.shape[1:], lambda bb, i: (bb, 0, 0, 0))],
        out_specs=pl.BlockSpec((1, tq, hq * HEAD_DIM), lambda bb, i: (bb, i, 0)),
        out_shape=jax.ShapeDtypeStruct((b, s, hq * HEAD_DIM), BF16),
        scratch_shapes=[pltpu.VMEM((rows, dp), BF16), pltpu.VMEM((rows, dp), F32),
                        pltpu.VMEM((rows, 1), F32)],
        name="gqa_attention",
        compiler_params=_cparams("parallel", "parallel"),
    )(q, k, v, kn)


def _band_reach(t):
    return -(-max(w // 2 for w, _ in A_PATTERNS) // t)


def _band_bias(t, nproc):
    offs = range(-(nproc - 1), nproc)
    a = np.arange(t)[:, None]
    bcol = np.arange(t)[None, :]
    out = np.zeros((len(offs), t, t), np.float64)
    for n, d in enumerate(offs):
        j = d * t + bcol - a
        mult = np.zeros((t, t), np.float64)
        for window, dil in A_PATTERNS:
            mult += ((j % dil) == 0) & (np.abs(j) <= window // 2)
        out[n] = np.where(mult > 0, LOG2E * np.log(np.maximum(mult, 1.0)), NEG)
    return jnp.asarray(out, F32)


def _band_kernel(q_ref, k_ref, v_ref, kn_ref, bias_ref, o_ref, qa_ref, acc_ref, m_ref, *,
                 t, nproc, rsub):
    nsplit = q_ref.shape[2] // t
    nblk = k_ref.shape[2] // t
    max_bias = float(np.log(len(A_PATTERNS)))
    subtiles = [pl.ds(r * rsub, rsub) for r in range(t // rsub)]

    def block_index(u):
        return pl.program_id(2) * nsplit + u

    def key_block(u, n):
        i = block_index(u)
        kb = jnp.clip(i - _band_reach(t), 0, nblk - nproc) + n
        return pl.ds(pl.multiple_of(kb * t, t), t), bias_ref.at[kb - i + nproc - 1]

    def store(u):
        o_ref[0, u * t:(u + 1) * t, :] = _normalised_pair(
            acc_ref[u, 0], acc_ref[u, 1]).astype(o_ref.dtype)

    smallest = []
    for u in range(nsplit):
        rows = slice(u * t, (u + 1) * t)
        for h in range(2):
            qa_ref[u, h] = _bounded_queries(
                q_ref[0, h, rows, :], _query_coefficients(kn_ref[0, 0], h == 1, max_bias))
            for n in range(nproc):
                keys, bias = key_block(u, n)
                scores = [_dot_nt(qa_ref[u, h, rr, :], k_ref[0, h, keys, :]) + bias[rr, :]
                          for rr in subtiles]
                weights = [jnp.exp2(s).astype(BF16) for s in scores]
                for rr, p in zip(subtiles, weights):
                    update = jnp.dot(p, v_ref[0, h, keys, :], preferred_element_type=F32)
                    if n == 0:
                        acc_ref[u, h, rr, :] = update
                    else:
                        acc_ref[u, h, rr, :] += update
        smallest.append(jnp.minimum(jnp.min(_denominators(acc_ref[u, 0], False)),
                                    jnp.min(_denominators(acc_ref[u, 1], True))))
        store(u)

    for u in range(nsplit):
        @pl.when(smallest[u] < MIN_DENOM)
        def _():
            for h in range(2):
                acc = acc_ref.at[u, h]
                _online_init(m_ref, acc)

                def online(n, carry):
                    keys, bias = key_block(u, n)
                    _online_step(q_ref[0, h, u * t:(u + 1) * t, :], k_ref[0, h, keys, :],
                                 v_ref[0, h, keys, :], LN2 * bias[...], m_ref, acc)
                    return carry

                lax.fori_loop(0, nproc, online, 0)
            store(u)


def _dilated_attention(q, k, v, kn, t):
    b, h, s, dp = q.shape
    nproc = min(2 * _band_reach(t) + 1, s // t)
    bias = _band_bias(t, nproc)
    nsplit = max(n for n in (1, 2, 4) if s % (n * t) == 0)
    rows = nsplit * t
    kv = pl.BlockSpec((1, 2, s, dp), lambda bb, hh, i: (bb, hh, 0, 0))
    return pl.pallas_call(
        functools.partial(_band_kernel, t=t, nproc=nproc, rsub=min(256, t)),
        grid=(b, h // 2, s // rows),
        in_specs=[pl.BlockSpec((1, 2, rows, dp), lambda bb, hh, i: (bb, hh, i, 0)), kv, kv,
                  pl.BlockSpec((1, 1) + kn.shape[2:], lambda bb, hh, i: (bb, hh, 0, 0)),
                  pl.BlockSpec(bias.shape, lambda bb, hh, i: (0, 0, 0),
                               pipeline_mode=pl.Buffered(1))],
        out_specs=pl.BlockSpec((1, rows, LANES), lambda bb, hh, i: (bb, i, hh)),
        out_shape=jax.ShapeDtypeStruct((b, s, h * HEAD_DIM), BF16),
        scratch_shapes=[pltpu.VMEM((nsplit, 2, t, dp), BF16), pltpu.VMEM((nsplit, 2, t, dp), F32),
                        pltpu.VMEM((t, 1), F32)],
        name="dilated_attention",
        compiler_params=_cparams("parallel", "parallel", "parallel"),
    )(q, k, v, kn, bias)


def _residual_update(x, y, gate, norm_g):
    return x + gate * (_rms_rows(y) * norm_g)


def _mixer_output_even(rows, oa_ref, ob_ref):
    return oa_ref[0, rows, :], ob_ref[0, rows, :]


def _mixer_output_odd(rows, hf_ref, hb_ref, og_ref, d_ref, mhg_ref):
    ones_blk = _head_block_ones()
    parts = []
    for c in range(HALF_W // LANES):
        cols = slice(c * LANES, (c + 1) * LANES)
        hs = hf_ref[0, rows, cols].astype(F32) + hb_ref[0, rows, cols].astype(F32)
        hn = hs * lax.rsqrt(_head_mean_square(hs, ones_blk) + EPS) * mhg_ref[:, cols]
        parts.append((hn * jax.nn.sigmoid(og_ref[0, rows, cols].astype(F32))).astype(BF16))
    return jnp.concatenate(parts, axis=-1), d_ref[0, rows, :]


def _layer_tail_kernel(*refs, odd, tf, nsplit):
    n_mix = 5 if odd else 2
    mixer_output = _mixer_output_odd if odd else _mixer_output_even
    (wo_ref, x_ref, g1_ref, ng1_ref, ng2_ref, sc_ref, sh_ref, w1_ref, w2_ref, g2_ref, ng3_ref,
     o_ref, a_ref) = refs[n_mix:]
    tm = x_ref.shape[1]
    groups = [slice(r * tm // nsplit, (r + 1) * tm // nsplit) for r in range(nsplit)]
    dot = functools.partial(jnp.dot, preferred_element_type=F32)

    mixes = [mixer_output(rows, *refs[:n_mix]) for rows in groups]
    ys = [dot(m[0], wo_ref[:HALF_W, :]) + dot(m[1], wo_ref[HALF_W:, :]) for m in mixes]
    x1s = [_residual_update(x_ref[0, rows, :], y, g1_ref[0], ng1_ref[...])
           for rows, y in zip(groups, ys)]
    hs = [(_rms_rows(x1) * (ng2_ref[...] * (1.0 + sc_ref[0])) + sh_ref[0]).astype(BF16)
          for x1 in x1s]
    for rows, h in zip(groups, hs):
        for kc in range(w1_ref.shape[1] // tf):
            cols = slice(kc * tf, (kc + 1) * tf)
            a = jnp.maximum(dot(h, w1_ref[:, cols]), 0.0)
            a_ref[rows, cols] = (a * a).astype(BF16)
    y2s = [dot(a_ref[rows, :], w2_ref[...]) for rows in groups]
    for rows, x1, y2 in zip(groups, x1s, y2s):
        o_ref[0, rows, :] = _residual_update(x1, y2, g2_ref[0], ng3_ref[...])


def _layer_tail(mix_inputs, mix_specs, odd, wo, x, g1, ng1, ng2, sc, sh, w1, w2, g2, ng3, tm, tf):
    b, s, d = x.shape
    f = w1.shape[1]
    full = pl.BlockSpec((1, tm, d), lambda bb, i: (bb, i, 0))
    mod = pl.BlockSpec((1, 1, d), lambda bb, i: (bb, 0, 0))
    vec = pl.BlockSpec((1, d), lambda bb, i: (0, 0))

    def resident(shape):
        return pl.BlockSpec(shape, lambda bb, i: (0, 0), pipeline_mode=pl.Buffered(1))

    return pl.pallas_call(
        functools.partial(_layer_tail_kernel, odd=odd, tf=tf, nsplit=2),
        grid=(b, s // tm),
        in_specs=mix_specs + [resident((d, d)), full, mod, vec, vec, mod, mod,
                              resident((d, f)), resident((f, d)), mod, vec],
        out_specs=full,
        out_shape=jax.ShapeDtypeStruct((b, s, d), F32),
        scratch_shapes=[pltpu.VMEM((tm, f), BF16)],
        name="layer_tail_odd" if odd else "layer_tail_even",
        compiler_params=_cparams("parallel", "parallel"),
    )(*mix_inputs, wo, x, g1, ng1, ng2, sc, sh, w1, w2, g2, ng3)


def _layer_tail_even(oa, ob, *rest, tm, tf):
    half = pl.BlockSpec((1, tm, HALF_W), lambda bb, i: (bb, i, 0))
    return _layer_tail([oa, ob], [half, half], False, *rest, tm, tf)


def _layer_tail_odd(hf, hb, z, dout, mhg, *rest, tm, tf):
    half = pl.BlockSpec((1, tm, HALF_W), lambda bb, i: (bb, i, 0))
    og = pl.BlockSpec((1, tm, HALF_W), lambda bb, i: (bb, i, 3))
    return _layer_tail([hf, hb, z, dout, mhg],
                       [half, half, og, half, pl.BlockSpec((1, HALF_W), lambda bb, i: (0, 0))],
                       True, *rest, tm, tf)


def _inproj_odd_kernel(x_ref, g_ref, sc_ref, sh_ref, w_ref, gb_ref, z_ref, gates_ref):
    x = x_ref[0]
    h = _rms_rows(x) * (g_ref[...] * (1.0 + sc_ref[0])) + sh_ref[0]
    hb = h.astype(BF16)
    k_scale = HEAD_DIM ** -0.5
    projections = [jnp.dot(hb, w_ref[:, j * MXU_N:(j + 1) * MXU_N], preferred_element_type=F32)
                   for j in range(CD_MAIN // MXU_N)]
    gates_ref[0] = jnp.dot(hb, w_ref[:, CD_MAIN:], preferred_element_type=F32) + gb_ref[...]
    for j, zz in enumerate(projections):
        if HALF_W <= j * MXU_N < 2 * HALF_W:
            zz = zz * k_scale
        z_ref[0, :, j * MXU_N:(j + 1) * MXU_N] = zz.astype(BF16)


def _inproj_odd(x, g, sc, sh, w, gb, tm):
    b, s, d = x.shape
    n = w.shape[1]
    mod = pl.BlockSpec((1, 1, d), lambda i, bb: (bb, 0, 0))
    return pl.pallas_call(
        _inproj_odd_kernel,
        grid=(s // tm, b),
        in_specs=[pl.BlockSpec((1, tm, d), lambda i, bb: (bb, i, 0)),
                  pl.BlockSpec((1, d), lambda i, bb: (0, 0)),
                  mod, mod,
                  pl.BlockSpec((d, n), lambda i, bb: (0, 0)),
                  pl.BlockSpec((1, GATE_W), lambda i, bb: (0, 0))],
        out_specs=[pl.BlockSpec((1, tm, CD_MAIN), lambda i, bb: (bb, i, 0)),
                   pl.BlockSpec((1, tm, GATE_W), lambda i, bb: (bb, i, 0))],
        out_shape=[jax.ShapeDtypeStruct((b, s, CD_MAIN), BF16),
                   jax.ShapeDtypeStruct((b, s, GATE_W), F32)],
        name="inproj_odd",
        compiler_params=_cparams("parallel", "parallel"),
    )(x, g, sc, sh, w, gb)


def _gate_lane(rev, head):
    return (C_HEADS if rev else 0) + head


def _mlstm_constants(rev, lc):
    wq = QUAD * HEAD_DIM
    t = np.arange(lc)
    sees = (t[None, :] >= t[:, None]) if rev else (t[None, :] <= t[:, None])
    nq = C_HEADS // QUAD
    sel_s = np.zeros((nq, LANES, QUAD * lc), np.float32)
    sel_v = np.zeros((nq, LANES, wq), np.float32)
    for qd in range(nq):
        for hh in range(QUAD):
            lane = _gate_lane(rev, QUAD * qd + hh)
            sel_s[qd, lane, hh * lc:(hh + 1) * lc] = 1.0
            sel_v[qd, lane, hh * HEAD_DIM:(hh + 1) * HEAD_DIM] = 1.0
    head_of_row = np.arange(QUAD * lc) // lc
    head_of_col = np.arange(wq) // HEAD_DIM
    return dict(
        tri=jnp.asarray(sees, BF16),
        sel_s=jnp.asarray(sel_s, BF16), sel_v=jnp.asarray(sel_v, BF16),
        ones_bd=jnp.asarray(np.transpose(sel_s, (0, 2, 1)), BF16),
        row_head=jnp.asarray(head_of_row[:, None] == head_of_col[None, :], BF16),
        diag=jnp.asarray(head_of_col[:, None] == head_of_col[None, :], F32),
        causal=jnp.asarray(np.tile(sees, (1, QUAD)), F32))


def _scan_max(x, rev):
    n = x.shape[0]
    row = lax.broadcasted_iota(jnp.int32, x.shape, 0)
    sh = 1
    while sh < n:
        if rev:
            x = jnp.maximum(x, jnp.where(row < n - sh, pltpu.roll(x, n - sh, axis=0), -jnp.inf))
        else:
            x = jnp.maximum(x, jnp.where(row >= sh, pltpu.roll(x, sh, axis=0), -jnp.inf))
        sh *= 2
    return x


def _mlstm_kernel(*refs, lc):
    nd = len(_MLSTM_DIR_CONSTS)
    blocks = (refs[0:4], refs[4:8])
    rowhead_ref, diag_ref = refs[8:10]
    consts = (refs[10:10 + nd], refs[10 + nd:10 + 2 * nd])
    outs = refs[10 + 2 * nd:12 + 2 * nd]
    c_sc, n_sc, m_sc = refs[12 + 2 * nd:]

    @pl.when(pl.program_id(1) == 0)
    def _():
        c_sc[...] = jnp.zeros(c_sc.shape, F32)
        n_sc[...] = jnp.zeros(n_sc.shape, F32)
        m_sc[...] = jnp.zeros(m_sc.shape, F32)

    nch = outs[0].shape[1] // lc

    wq = QUAD * HEAD_DIM
    edges = (lc - 1, 0)
    jobs = [(rev, qd) for rev in (0, 1) for qd in range(C_HEADS // QUAD)]
    dot = functools.partial(jnp.dot, preferred_element_type=F32)

    def chunk(ci, carry):
        rows = (pl.ds(pl.multiple_of(ci * lc, lc), lc),
                pl.ds(pl.multiple_of((nch - 1 - ci) * lc, lc), lc))
        gate = []
        for rev in (0, 1):
            g_ref = blocks[rev][3]
            tri_ref = consts[rev][0]
            gi = g_ref[0, rows[rev], :LANES]
            bcum = _split_dot_rhs(tri_ref[...], _log_sigmoid(g_ref[0, rows[rev], LANES:]))
            a = gi - bcum
            m_prev = m_sc[rev]
            mt_b = jnp.maximum(m_prev, _scan_max(a, bool(rev))).astype(BF16)
            mt = mt_b.astype(F32)
            w_b = jnp.exp(m_prev - mt).astype(BF16)
            b_all = bcum[edges[rev]:edges[rev] + 1, :]
            log_w = b_all + a
            m_new = jnp.maximum(b_all + m_prev, jnp.max(log_w, axis=0, keepdims=True))
            gate.append(dict(
                mt_b=mt_b, w_b=w_b, w=w_b.astype(F32),
                floor=jnp.exp(-(bcum + mt)),
                wk_b=jnp.exp(log_w - m_new).astype(BF16),
                decay=jnp.broadcast_to(jnp.exp(b_all + m_prev - m_new), (8, LANES)),
                a_t=a.T))
            m_sc[rev] = m_new

        row_head = rowhead_ref[...]

        def operand(rev, qd, j):
            return blocks[rev][j][0, rows[rev], qd * wq:(qd + 1) * wq]

        def sel_v(rev, qd):
            return consts[rev][2][qd]

        scores, log_e = [], []
        for rev, qd in jobs:
            kbd = jnp.concatenate([operand(rev, qd, 1)] * QUAD, axis=0) * row_head
            scores.append(_dot_nt(operand(rev, qd, 0), kbd))
            a_rows = jnp.concatenate(
                [jnp.broadcast_to(gate[rev]["a_t"][ln:ln + 1, :], (lc, lc))
                 for ln in (_gate_lane(bool(rev), QUAD * qd + hh) for hh in range(QUAD))], axis=1)
            log_e.append(jnp.where(consts[rev][4][...] > 0.0,
                                   a_rows - dot(gate[rev]["mt_b"], consts[rev][1][qd]), NEG))
        probs = [(s * jnp.exp(e)).astype(BF16) for s, e in zip(scores, log_e)]

        pvs, qcs = [], []
        for (rev, qd), p in zip(jobs, probs):
            v = operand(rev, qd, 2)
            vaug = jnp.concatenate(
                [jnp.concatenate([v] * QUAD, axis=0) * row_head, consts[rev][3][qd]], axis=1)
            pvs.append(dot(p, vaug))
            caug = jnp.concatenate(
                [c_sc[rev, qd].astype(BF16),
                 (sel_v(rev, qd).astype(F32) * n_sc[rev, qd]).astype(BF16)], axis=0)
            qcs.append(_dot_nt(operand(rev, qd, 0), caug))

        for (rev, qd), pv, qc in zip(jobs, pvs, qcs):
            g = gate[rev]
            den = pv[:, wq:] + g["w"] * qc[:, wq:]
            scale_b = (1.0 / jnp.maximum(jnp.abs(den), g["floor"])).astype(BF16)
            h_out = dot(scale_b, sel_v(rev, qd)) * (
                pv[:, :wq] + dot(g["w_b"], sel_v(rev, qd)) * qc[:, :wq])
            outs[rev][0, rows[rev], qd * wq:(qd + 1) * wq] = h_out.astype(outs[rev].dtype)

        for rev, qd in jobs:
            g = gate[rev]
            k = operand(rev, qd, 1)
            wk_v = dot(g["wk_b"], sel_v(rev, qd))
            dec_v = _split_dot(g["decay"], sel_v(rev, qd))[:1]
            vw = (wk_v * operand(rev, qd, 2).astype(F32)).astype(BF16)
            c_sc[rev, qd] = dec_v * c_sc[rev, qd] + diag_ref[...] * _dot_tn(vw, k)
            n_sc[rev, qd] = (dec_v * n_sc[rev, qd]
                             + jnp.sum(wk_v * k.astype(F32), axis=0, keepdims=True))
        return carry

    lax.fori_loop(0, nch, chunk, 0, unroll=4 if nch % 4 == 0 else 1)


_MLSTM_DIR_CONSTS = ("tri", "sel_s", "sel_v", "ones_bd", "causal")


def _mlstm(z, gates, tb, lc):
    b, s, _ = z.shape
    nblk = s // tb
    position = (lambda i: i, lambda i: nblk - 1 - i)

    def col(rev, j):
        return pl.BlockSpec((1, tb, HALF_W), lambda bb, i: (bb, position[rev](i), j))

    def block_specs(rev):
        return [col(rev, 0), col(rev, 1), col(rev, 2),
                pl.BlockSpec((1, tb, GATE_W), lambda bb, i: (bb, position[rev](i), 0))]

    def const_spec(a):
        return pl.BlockSpec(a.shape, lambda bb, i: (0,) * a.ndim)

    consts = (_mlstm_constants(False, lc), _mlstm_constants(True, lc))
    shared = [consts[0]["row_head"], consts[0]["diag"]]
    per_dir = [consts[rev][n] for rev in (0, 1) for n in _MLSTM_DIR_CONSTS]
    nq = C_HEADS // QUAD
    wq = QUAD * HEAD_DIM
    return pl.pallas_call(
        functools.partial(_mlstm_kernel, lc=lc),
        grid=(b, nblk),
        in_specs=block_specs(0) + block_specs(1) + [const_spec(a) for a in shared + per_dir],
        out_specs=[col(0, 0), col(1, 0)],
        out_shape=[jax.ShapeDtypeStruct((b, s, HALF_W), BF16)] * 2,
        scratch_shapes=[pltpu.VMEM((2, nq, wq, wq), F32), pltpu.VMEM((2, nq, 1, wq), F32),
                        pltpu.VMEM((2, 1, LANES), F32)],
        name="mlstm",
        compiler_params=_cparams("parallel", "arbitrary"),
    )(z, z, z, gates, z, z, z, gates, *shared, *per_dir)


def _sgu_kernel(u_ref, v_ref, lng_ref, w_ref, b_ref, o_ref):
    tb = u_ref.shape[1]
    first_group = lax.broadcasted_iota(jnp.int32, (D_CHUNK, LANES), 1) < HEAD_DIM
    for ch in range(tb // D_CHUNK):
        rows = slice(ch * D_CHUNK, (ch + 1) * D_CHUNK)
        u = jax.nn.gelu(u_ref[0, rows, :].astype(F32))
        v = jax.nn.gelu(v_ref[0, rows, :].astype(F32))
        dv = v - jnp.mean(v, axis=-1, keepdims=True)
        vn = (dv * lax.rsqrt(jnp.mean(dv * dv, axis=-1, keepdims=True) + EPS)
              * lng_ref[...]).astype(BF16)
        mixed = []
        for c in range(HALF_W // LANES):
            pair = jnp.dot(w_ref[2 * c * D_CHUNK:(2 * c + 2) * D_CHUNK, :],
                           vn[:, c * LANES:(c + 1) * LANES], preferred_element_type=F32)
            mixed.append(jnp.where(first_group, pair[:D_CHUNK], pair[D_CHUNK:]))
        sg = jnp.concatenate(mixed, axis=1) + b_ref[...]
        o_ref[0, rows, :] = (u * sg).astype(BF16)


def _sgu(z, lng, w, bexp, tb):
    b, s, _ = z.shape
    col = lambda j: pl.BlockSpec((1, tb, HALF_W), lambda bb, i: (bb, i, j))
    return pl.pallas_call(
        _sgu_kernel,
        grid=(b, s // tb),
        in_specs=[col(4), col(5),
                  pl.BlockSpec((1, HALF_W), lambda bb, i: (0, 0)),
                  pl.BlockSpec((D_GROUPS * D_CHUNK, D_CHUNK), lambda bb, i: (0, 0)),
                  pl.BlockSpec((D_CHUNK, HALF_W), lambda bb, i: (0, 0))],
        out_specs=col(0),
        out_shape=jax.ShapeDtypeStruct((b, s, HALF_W), BF16),
        name="sgu",
        compiler_params=_cparams("parallel", "parallel"),
    )(z, z, lng, w, bexp)


def _tiles(s):
    return dict(tm=min(512, s), tf=1024, tq=min(256, s), tk=min(1024, s), tband=min(512, s),
                tb_mlstm=min(1024, s), lc=128, tb_sgu=min(512, s))


def _gqa_head_order():
    half = B_Q_HEADS // 2
    return [h for j in range(half) for h in (j, half + j)]


def _trunk(x, c, p):
    b, s, d = x.shape
    t = _tiles(s)
    mod = _adaln(c, p["w_ada"], p["b_ada"])
    ta, tq, tk = _rope_tables(s, p["qk_norm_g"])
    for layer in range(2):
        m = mod[:, layer * 6 * d:(layer + 1) * 6 * d].reshape(b, 6, 1, d)
        sh1, sc1, g1, sh2, sc2, g2 = (m[:, j] for j in range(6))
        ng = p["norm_g"][layer].reshape(4, 1, d)
        tail = (x, g1, ng[1], ng[2], sc2, sh2, p["w_ff1"][layer], p["w_ff2"][layer], g2, ng[3])
        if layer == 0:
            qa, ka, va, qb, kb, vb, kna, knb = _inproj_even(x, ng[0], sc1, sh1, p["w_in_ab"],
                                                            ta, tq, tk, t["tm"])
            oa = _dilated_attention(qa, ka, va, kna, t["tband"])
            ob = _gqa_attention(qb, kb, vb, knb, t["tq"], t["tk"])
            x = _layer_tail_even(oa, ob, p["w_out_ab"], *tail, tm=t["tm"], tf=t["tf"])
        else:
            z, gates = _inproj_odd(x, ng[0], sc1, sh1, p["w_in_cd"], p["gate_bias"], t["tm"])
            hf, hb = _mlstm(z, gates, t["tb_mlstm"], t["lc"])
            dout = _sgu(z, p["sg_norm_g"], p["w_spatial"], p["b_spatial"], t["tb_sgu"])
            x = _layer_tail_odd(hf, hb, z, dout, p["mh_norm_g"], p["w_out_cd"], *tail,
                                tm=t["tm"], tf=t["tf"])
    return x


def _prepare_params(w_in_ab, w_out_ab, qk_norm_g, w_in_cd, w_out_cd, gate_bias, mh_norm_g,
                    sg_norm_g, w_spatial, b_spatial, w_ada, b_ada, norm_g, w_ff1, w_ff2):
    depth, d, _ = w_ada.shape
    wcd = w_in_cd[0]
    gate_lo = 4 * HALF_W
    gate_hi = gate_lo + N_GATES

    def gate_tiles(g):
        i_fw, f_fw, i_bw, f_bw = (g[:, j * C_HEADS:(j + 1) * C_HEADS] for j in range(4))
        pad = jnp.zeros((g.shape[0], LANES - 2 * C_HEADS), g.dtype)
        return jnp.concatenate([i_fw, i_bw, pad, f_fw, f_bw, pad], axis=1)

    wcd = jnp.concatenate([wcd[:, :gate_lo], wcd[:, gate_hi:],
                           gate_tiles(wcd[:, gate_lo:gate_hi])], axis=1)
    gb = gate_tiles(gate_bias[0].reshape(1, N_GATES).astype(F32))
    bexp = jnp.repeat(b_spatial[0].astype(F32).T, HEAD_DIM, axis=1)

    qb_lo = 3 * HALF_W
    order = np.asarray(_gqa_head_order())
    head_cols = (order[:, None] * HEAD_DIM + np.arange(HEAD_DIM)[None, :]).reshape(-1)
    wab = w_in_ab[0]
    wab = jnp.concatenate([wab[:, :qb_lo], wab[:, qb_lo + head_cols], wab[:, qb_lo + HALF_W:]],
                          axis=1)
    wout_ab = w_out_ab[0]
    wout_ab = jnp.concatenate([wout_ab[:HALF_W], wout_ab[HALF_W + head_cols]], axis=0)
    return dict(
        w_ada=jnp.concatenate([w_ada[l] for l in range(depth)], axis=1).astype(BF16),
        b_ada=b_ada.reshape(1, -1).astype(F32),
        w_in_ab=wab.astype(BF16), w_out_ab=wout_ab.astype(BF16),
        qk_norm_g=qk_norm_g[0],
        w_in_cd=wcd.astype(BF16), w_out_cd=w_out_cd[0].astype(BF16), gate_bias=gb,
        mh_norm_g=mh_norm_g[0].reshape(1, HALF_W).astype(F32),
        sg_norm_g=sg_norm_g[0].reshape(1, HALF_W).astype(F32),
        w_spatial=w_spatial[0].reshape(D_GROUPS * D_CHUNK, D_CHUNK).astype(BF16),
        b_spatial=bexp,
        norm_g=norm_g.astype(F32), w_ff1=w_ff1.astype(BF16), w_ff2=w_ff2.astype(BF16))


def kernel(x_prompt, x_sample, c_prompt, c_sample, w_in_ab, w_out_ab, qk_norm_g, w_in_cd, w_out_cd, gate_bias, mh_norm_g, sg_norm_g, w_spatial, b_spatial, w_ada, b_ada, norm_g, w_ff1, w_ff2):
    p = _prepare_params(w_in_ab, w_out_ab, qk_norm_g, w_in_cd, w_out_cd, gate_bias, mh_norm_g,
                        sg_norm_g, w_spatial, b_spatial, w_ada, b_ada, norm_g, w_ff1, w_ff2)
    return (_trunk(x_prompt, c_prompt, p), _trunk(x_sample, c_sample, p))
```
